```python
import jax, jax.numpy as jnp
from jax import lax
import numpy as np

D_MODEL = 2048
BATCH = 4
SEQ = 4096
DEPTH = 4
DEC_BATCH = 32
DEC_SEQ = 32
PAST_LEN = 4096

CHUNK = 64
Q_BLOCK = 128
N_MIXERS = 2
N_A = (DEPTH + 1) // 2
N_B = DEPTH // 2
N_HEADS = 16
QK_NOPE = 128
QK_ROPE = 64
V_HEAD = 128
Q_LORA = 512
KV_LORA = 512
ROPE_THETA = 10000.0
CONV_W = 3
D_FF = 4 * D_MODEL
EPS = 1e-6
SM_SCALE = (QK_NOPE + QK_ROPE) ** -0.5

kernel_name = "mla_shortconv_hybrid_stream_step"


def rmsnorm(x, g):
    xf = x.astype(jnp.float32)
    y = xf * lax.rsqrt(jnp.mean(xf * xf, axis=-1, keepdims=True) + EPS)
    return (y * g.astype(jnp.float32)).astype(x.dtype)


def rope(x, pos):
    half = x.shape[-1] // 2
    inv = ROPE_THETA ** (-jnp.arange(0, half, dtype=jnp.float32) / half)
    ang = pos.astype(jnp.float32)[:, None] * inv[None, :]
    shape = (pos.shape[0],) + (1,) * (x.ndim - 3) + (half,)
    cos, sin = jnp.cos(ang).reshape(shape), jnp.sin(ang).reshape(shape)
    xf = x.astype(jnp.float32)
    x1, x2 = xf[..., :half], xf[..., half:]
    return jnp.concatenate([x1 * cos - x2 * sin, x1 * sin + x2 * cos], axis=-1).astype(x.dtype)


def mla_project(h, pos, w_dq, g_q, w_uq, w_dkv, g_kv):
    b, t, _ = h.shape
    c_q = rmsnorm(h @ w_dq, g_q)
    q = (c_q @ w_uq).reshape(b, t, N_HEADS, QK_NOPE + QK_ROPE)
    q_nope, q_rope = q[..., :QK_NOPE], rope(q[..., QK_NOPE:], pos)
    kv = h @ w_dkv
    c_kv = rmsnorm(kv[..., :KV_LORA], g_kv)
    k_rope = rope(kv[..., KV_LORA:], pos)
    return q_nope, q_rope, c_kv, k_rope


def mla_block(q_nope, q_rope, q_pos, c_kv, k_rope, k_pos, w_uk, w_uv):
    q_lat = jnp.einsum('bqhn,lhn->bqhl', q_nope, w_uk)
    s = (jnp.einsum('bqhl,bkl->bhqk', q_lat, c_kv)
         + jnp.einsum('bqhr,bkr->bhqk', q_rope, k_rope)).astype(jnp.float32) * SM_SCALE
    allowed = (k_pos[None, :] // CHUNK) <= (q_pos[:, None] // CHUNK)
    s = jnp.where(allowed[None, None], s, -jnp.inf)
    p = jax.nn.softmax(s, axis=-1).astype(c_kv.dtype)
    o_lat = jnp.einsum('bhqk,bkl->bqhl', p, c_kv)
    return jnp.einsum('bqhl,lhv->bqhv', o_lat, w_uv)


def mla_prompt(q_nope, q_rope, c_kv, k_rope, w_uk, w_uv):
    b, s = q_nope.shape[:2]
    nblk = s // Q_BLOCK
    k_pos = jnp.arange(s, dtype=jnp.int32)

    def to_blocks(a):
        return a.reshape((b, nblk, Q_BLOCK) + a.shape[2:]).swapaxes(0, 1)

    def body(args):
        qn, qr, qp = args
        return mla_block(qn, qr, qp, c_kv, k_rope, k_pos, w_uk, w_uv)

    q_pos = jnp.arange(s, dtype=jnp.int32).reshape(nblk, Q_BLOCK)
    o = lax.map(body, (to_blocks(q_nope), to_blocks(q_rope), q_pos))
    return o.swapaxes(0, 1).reshape(b, s, N_HEADS, V_HEAD)


def mla_mixer(h, pos, past_c, past_r, w_dq, g_q, w_uq, w_dkv, g_kv, w_uk, w_uv, w_o):
    b, t, _ = h.shape
    q_nope, q_rope, c_kv, k_rope = mla_project(h, pos, w_dq, g_q, w_uq, w_dkv, g_kv)
    if past_c is None:
        o = mla_prompt(q_nope, q_rope, c_kv, k_rope, w_uk, w_uv)
    else:
        c_all = jnp.concatenate([past_c.astype(c_kv.dtype), c_kv], axis=1)
        r_all = jnp.concatenate([past_r.astype(k_rope.dtype), k_rope], axis=1)
        k_pos = jnp.arange(c_all.shape[1], dtype=jnp.int32)
        o = mla_block(q_nope, q_rope, pos, c_all, r_all, k_pos, w_uk, w_uv)
    return o.reshape(b, t, N_HEADS * V_HEAD) @ w_o, c_kv, k_rope


def conv_mixer(h, past_u, w_in, conv_w, w_out):
    b, t, d = h.shape
    gate_b, gate_c, v = jnp.split(h @ w_in, 3, axis=-1)
    u = gate_c * v
    if past_u is None:
        past_u = jnp.zeros((b, CONV_W - 1, d), u.dtype)
    u_pad = jnp.concatenate([past_u.astype(u.dtype), u], axis=1)
    conv = conv_w[0] * u_pad[:, 0:t]
    for k in range(1, CONV_W):
        conv = conv + conv_w[k] * u_pad[:, k:k + t]
    return (gate_b * conv) @ w_out, u_pad[:, -(CONV_W - 1):]


def trunk(x, pos, past_c, past_r, past_u, p):
    new_c, new_r, new_u = [], [], []
    for i in range(DEPTH):
        j = i // N_MIXERS
        h = rmsnorm(x, p['norm_mix'][i])
        if i % N_MIXERS == 0:
            y, c, r = mla_mixer(h, pos,
                                None if past_c is None else past_c[j],
                                None if past_r is None else past_r[j],
                                p['w_dq'][j], p['g_q'][j], p['w_uq'][j], p['w_dkv'][j],
                                p['g_kv'][j], p['w_uk'][j], p['w_uv'][j], p['w_o'][j])
            new_c.append(c)
            new_r.append(r)
        else:
            y, u = conv_mixer(h, None if past_u is None else past_u[j],
                              p['w_in'][j], p['conv_w'][j], p['w_out'][j])
            new_u.append(u)
        x = x + y
        h = rmsnorm(x, p['norm_ffn'][i])
        x = x + jnp.square(jax.nn.relu(h @ p['w_ff1'][i])) @ p['w_ff2'][i]
    return rmsnorm(x, p['norm_final']), jnp.stack(new_c), jnp.stack(new_r), jnp.stack(new_u)


def setup_inputs(seed: int = 0) -> dict:
    key = jax.random.key(seed)
    ks = jax.random.split(key, 24)
    f32 = jnp.float32

    def nrm(k, shape, scale=1.0):
        return jax.random.normal(k, shape, f32) * scale

    def gain(k, shape):
        return 1.0 + 0.02 * jax.random.normal(k, shape, f32)

    return {
        "x_prompt": nrm(ks[0], (BATCH, SEQ, D_MODEL)),
        "x_sample": nrm(ks[1], (DEC_BATCH, DEC_SEQ, D_MODEL)),
        "cache_ckv": nrm(ks[2], (N_A, DEC_BATCH, PAST_LEN, KV_LORA)),
        "cache_krope": nrm(ks[3], (N_A, DEC_BATCH, PAST_LEN, QK_ROPE)),
        "state_conv": nrm(ks[4], (N_B, DEC_BATCH, CONV_W - 1, D_MODEL)),
        "norm_mix": gain(ks[5], (DEPTH, D_MODEL)),
        "w_dq": nrm(ks[6], (N_A, D_MODEL, Q_LORA), D_MODEL ** -0.5),
        "g_q": gain(ks[7], (N_A, Q_LORA)),
        "w_uq": nrm(ks[8], (N_A, Q_LORA, N_HEADS * (QK_NOPE + QK_ROPE)), Q_LORA ** -0.5),
        "w_dkv": nrm(ks[9], (N_A, D_MODEL, KV_LORA + QK_ROPE), D_MODEL ** -0.5),
        "g_kv": gain(ks[10], (N_A, KV_LORA)),
        "w_uk": nrm(ks[11], (N_A, KV_LORA, N_HEADS, QK_NOPE), KV_LORA ** -0.5),
        "w_uv": nrm(ks[12], (N_A, KV_LORA, N_HEADS, V_HEAD), KV_LORA ** -0.5),
        "w_o": nrm(ks[13], (N_A, N_HEADS * V_HEAD, D_MODEL), (N_HEADS * V_HEAD) ** -0.5),
        "w_in": nrm(ks[14], (N_B, D_MODEL, 3 * D_MODEL), D_MODEL ** -0.5),
        "conv_w": nrm(ks[15], (N_B, CONV_W, D_MODEL), CONV_W ** -0.5),
        "w_out": nrm(ks[16], (N_B, D_MODEL, D_MODEL), D_MODEL ** -0.5),
        "norm_ffn": gain(ks[17], (DEPTH, D_MODEL)),
        "w_ff1": nrm(ks[18], (DEPTH, D_MODEL, D_FF), D_MODEL ** -0.5),
        "w_ff2": nrm(ks[19], (DEPTH, D_FF, D_MODEL), D_FF ** -0.5),
        "norm_final": gain(ks[20], (D_MODEL,)),
    }


def reference(x_prompt, x_sample, cache_ckv, cache_krope, state_conv,
              norm_mix, w_dq, g_q, w_uq, w_dkv, g_kv, w_uk, w_uv, w_o,
              w_in, conv_w, w_out, norm_ffn, w_ff1, w_ff2, norm_final):
    params = dict(norm_mix=norm_mix, w_dq=w_dq, g_q=g_q, w_uq=w_uq, w_dkv=w_dkv,
                  g_kv=g_kv, w_uk=w_uk, w_uv=w_uv, w_o=w_o, w_in=w_in,
                  conv_w=conv_w, w_out=w_out, norm_ffn=norm_ffn, w_ff1=w_ff1,
                  w_ff2=w_ff2, norm_final=norm_final)
    pos_p = jnp.arange(x_prompt.shape[1], dtype=jnp.int32)
    y_prompt, new_ckv_prompt, new_krope_prompt, new_conv_prompt = trunk(
        x_prompt, pos_p, None, None, None, params)
    pos_s = PAST_LEN + jnp.arange(x_sample.shape[1], dtype=jnp.int32)
    y_sample, new_ckv_sample, new_krope_sample, new_conv_sample = trunk(
        x_sample, pos_s, cache_ckv, cache_krope, state_conv, params)
    return (y_prompt, y_sample, new_ckv_prompt, new_krope_prompt, new_conv_prompt,
            new_ckv_sample, new_krope_sample, new_conv_sample)
```

```python
import functools

import jax
import jax.numpy as jnp
from jax import lax
from jax.experimental import pallas as pl
from jax.experimental.pallas import tpu as pltpu

CHUNK = 64
ROPE_THETA = 10000.0
EPS = 1e-6
HEAD_PAD = 256
ROPE_PAD = 128
VMEM_LIMIT = 56 * 1024 * 1024

F32 = jnp.float32
BF16 = jnp.bfloat16


def _params(*sem):
    return pltpu.CompilerParams(dimension_semantics=sem, vmem_limit_bytes=VMEM_LIMIT)


def _tile(total, pref):
    t = min(total, pref)
    assert total % t == 0, (total, pref)
    return t


def _resident(shape):
    nd = len(shape)
    return pl.BlockSpec(shape, lambda *_: (0,) * nd, pipeline_mode=pl.Buffered(1))


def _rms(xf, g):
    return xf * lax.rsqrt(jnp.mean(xf * xf, axis=-1, keepdims=True) + EPS) * g


def _dot(a, b):
    return jnp.dot(a, b, preferred_element_type=F32)


def _dot_nt(a, b):
    return lax.dot_general(a, b, (((1,), (1,)), ((), ())), preferred_element_type=F32)


def _mla_down_kernel(x_ref, g_ref, w_ref, gq_ref, gkv_ref, cos_ref, sin_ref,
                     cq_ref, ckv_ref, ckvb_ref, kr_ref, krb_ref, *, q_lora, kv_lora, rope, sm_scale):
    h = _rms(x_ref[...], g_ref[...]).astype(BF16)
    y = _dot(h, w_ref[...])
    cq_ref[...] = (_rms(y[:, :q_lora], gq_ref[...]) * sm_scale).astype(BF16)
    ckv = _rms(y[:, q_lora:q_lora + kv_lora], gkv_ref[...])
    ckv_ref[...] = ckv
    ckvb_ref[...] = ckv.astype(BF16)
    o = q_lora + kv_lora
    kr = y[:, o:o + ROPE_PAD] * cos_ref[...] + y[:, o + ROPE_PAD:o + 2 * ROPE_PAD] * sin_ref[...]
    kr_ref[...] = kr[:, :rope]
    krb_ref[...] = kr.astype(BF16)


def _mla_q_kernel(cq_ref, wa_ref, wb_ref, cos_ref, sin_ref, q_ref, *, n_heads, nope):
    cq = cq_ref[...]
    cos = cos_ref[...]
    sin = sin_ref[...]
    for hp in range(n_heads // 2):
        a = _dot(cq, wa_ref[:, hp * 2 * HEAD_PAD:(hp + 1) * 2 * HEAD_PAD])
        b = _dot(cq, wb_ref[:, hp * 2 * ROPE_PAD:(hp + 1) * 2 * ROPE_PAD])
        for s in range(2):
            c0 = (2 * hp + s) * HEAD_PAD
            q_ref[:, c0:c0 + nope] = a[:, s * HEAD_PAD:s * HEAD_PAD + nope].astype(BF16)
            ar = a[:, s * HEAD_PAD + nope:(s + 1) * HEAD_PAD]
            br = b[:, s * ROPE_PAD:(s + 1) * ROPE_PAD]
            q_ref[:, c0 + nope:c0 + HEAD_PAD] = (ar * cos + br * sin).astype(BF16)


def _mla_kv_kernel(ckvb_ref, krb_ref, wk_ref, wv_ref, k_ref, v_ref, *, n_heads, nope):
    c = ckvb_ref[...]
    krb = krb_ref[...]
    v_ref[...] = _dot(c, wv_ref[...]).astype(BF16)
    for hp in range(n_heads // 2):
        kn = _dot(c, wk_ref[:, hp * 2 * nope:(hp + 1) * 2 * nope])
        for s in range(2):
            c0 = (2 * hp + s) * HEAD_PAD
            k_ref[:, c0:c0 + nope] = kn[:, s * nope:(s + 1) * nope].astype(BF16)
            k_ref[:, c0 + nope:c0 + HEAD_PAD] = krb


def _flash_kernel(q_ref, k_ref, v_ref, o_ref, m_ref, l_ref, acc_ref, *, blk):
    qi = pl.program_id(2)
    q = q_ref[...]
    m_ref[...] = jnp.full(m_ref.shape, -jnp.inf, F32)
    l_ref[...] = jnp.zeros(l_ref.shape, F32)
    acc_ref[...] = jnp.zeros(acc_ref.shape, F32)

    def step(j, masked):
        start = pl.multiple_of(j * blk, blk)
        k = k_ref[pl.ds(start, blk), :]
        v = v_ref[pl.ds(start, blk), :]
        s = _dot_nt(q, k)
        if masked:
            qc = lax.broadcasted_iota(jnp.int32, (blk, blk), 0) // CHUNK
            kc = lax.broadcasted_iota(jnp.int32, (blk, blk), 1) // CHUNK
            s = jnp.where(kc <= qc, s, -jnp.inf)
        m_prev = m_ref[...]
        m_new = jnp.maximum(m_prev, jnp.max(s, axis=-1, keepdims=True))
        alpha = jnp.exp(m_prev - m_new)
        p = jnp.exp(s - m_new)
        l_ref[...] = alpha * l_ref[...] + jnp.sum(p, axis=-1, keepdims=True)
        acc_ref[...] = alpha * acc_ref[...] + _dot(p.astype(BF16), v)
        m_ref[...] = m_new

    def body(j, carry):
        step(j, False)
        return carry

    lax.fori_loop(0, qi, body, 0)
    step(qi, True)
    o_ref[...] = (acc_ref[...] / l_ref[...]).astype(BF16)


def _sample_attn_kernel(q_ref, wukt_ref, wuv_ref, cc_ref, cr_ref, nc_ref, nr_ref, o_in_ref, o_ref,
                        ql_ref, qr_ref, m_ref, l_ref, acc_ref, *, n_heads, nope, rope, tk, t_new):
    del o_in_ref
    for h in range(n_heads):
        c0 = h * HEAD_PAD
        ql_ref[h * t_new:(h + 1) * t_new, :] = _dot(q_ref[:, c0:c0 + nope], wukt_ref[h]).astype(BF16)
        qr_ref[h * t_new:(h + 1) * t_new, :] = q_ref[:, c0 + nope:c0 + nope + rope]
    ql = ql_ref[...]
    qr = qr_ref[...]
    m_ref[...] = jnp.full(m_ref.shape, -jnp.inf, F32)
    l_ref[...] = jnp.zeros(l_ref.shape, F32)
    acc_ref[...] = jnp.zeros(acc_ref.shape, F32)

    def update(c, r):
        s = _dot_nt(ql, c) + _dot_nt(qr, r)
        m_prev = m_ref[...]
        m_new = jnp.maximum(m_prev, jnp.max(s, axis=-1, keepdims=True))
        alpha = jnp.exp(m_prev - m_new)
        p = jnp.exp(s - m_new)
        l_ref[...] = alpha * l_ref[...] + jnp.sum(p, axis=-1, keepdims=True)
        acc_ref[...] = alpha * acc_ref[...] + _dot(p.astype(BF16), c)
        m_ref[...] = m_new

    def body(j, carry):
        start = pl.multiple_of(j * tk, tk)
        update(cc_ref[0, pl.ds(start, tk), :].astype(BF16), cr_ref[0, pl.ds(start, tk), :].astype(BF16))
        return carry

    lax.fori_loop(0, cc_ref.shape[1] // tk, body, 0)
    update(nc_ref[...], nr_ref[:, :rope])
    ol = (acc_ref[...] / l_ref[...]).astype(BF16)
    for h in range(n_heads):
        o_ref[:, h * nope:(h + 1) * nope] = _dot(ol[h * t_new:(h + 1) * t_new, :], wuv_ref[h]).astype(BF16)


def _proj_res_kernel(a_ref, w_ref, x_ref, o_ref):
    o_ref[...] = x_ref[...] + _dot(a_ref[...], w_ref[...])


def _ffn_kernel(x_ref, g_ref, w1_ref, w2_ref, o_ref, h_ref):
    @pl.when(pl.program_id(1) == 0)
    def _():
        x = x_ref[...]
        h_ref[...] = _rms(x, g_ref[...]).astype(BF16)
        o_ref[...] = x

    a = jnp.maximum(_dot(h_ref[...], w1_ref[...]), 0.0)
    o_ref[...] += _dot((a * a).astype(BF16), w2_ref[...])


def _conv_in_kernel(x_ref, g_ref, wb_ref, wc_ref, wv_ref, u_ref, gb_ref, h_ref):
    @pl.when(pl.program_id(1) == 0)
    def _():
        h_ref[...] = _rms(x_ref[...], g_ref[...]).astype(BF16)

    h = h_ref[...]
    gb_ref[...] = _dot(h, wb_ref[...]).astype(BF16)
    u_ref[...] = _dot(h, wc_ref[...]) * _dot(h, wv_ref[...])


def _conv_mix(u, p1, p2, first, second, gb, cw, wout, x):
    u1 = jnp.where(first, p1, pltpu.roll(u, 1, axis=0))
    u2 = jnp.where(first, p2, jnp.where(second, p1, pltpu.roll(u, 2, axis=0)))
    conv = cw[0:1] * u2 + cw[1:2] * u1 + cw[2:3] * u
    return x + _dot((gb.astype(F32) * conv).astype(BF16), wout)


def _conv_out_prompt_kernel(u_ref, halo_ref, gb_ref, x_ref, cw_ref, wout_ref, o_ref):
    tm = u_ref.shape[0]
    has_past = pl.program_id(1) > 0
    halo = halo_ref[...]
    p1 = jnp.where(has_past, halo[7:8], 0.0)
    p2 = jnp.where(has_past, halo[6:7], 0.0)
    row = lax.broadcasted_iota(jnp.int32, (tm, 1), 0)
    o_ref[...] = _conv_mix(u_ref[...], p1, p2, row == 0, row == 1, gb_ref[...], cw_ref[...],
                           wout_ref[...], x_ref[...])


def _conv_out_sample_kernel(u_ref, st_ref, gb_ref, x_ref, cw_ref, wout_ref, xp_ref, o_ref, *, t_new):
    del xp_ref
    tm, d = u_ref.shape
    nb = tm // t_new
    st = st_ref[...]
    p1 = jnp.broadcast_to(st[:, 1:2, :], (nb, t_new, d)).reshape(tm, d)
    p2 = jnp.broadcast_to(st[:, 0:1, :], (nb, t_new, d)).reshape(tm, d)
    t = lax.broadcasted_iota(jnp.int32, (tm, 1), 0) % t_new
    o_ref[...] = _conv_mix(u_ref[...], p1, p2, t == 0, t == 1, gb_ref[...], cw_ref[...],
                           wout_ref[...], x_ref[...])


def _final_norm_kernel(x_ref, g_ref, o_ref):
    o_ref[...] = _rms(x_ref[...], g_ref[...])


def _row_spec(tm, n, off=0):
    return pl.BlockSpec((tm, n), lambda i, *_: (i + off, 0))


def _mla_layer(x, n_prompt, seq, t_new, cos, sin, cache_c, cache_r, g_mix, w_dq, g_q, w_uq, w_dkv, g_kv,
               w_uk, w_uv, w_o):
    t_all, d = x.shape
    q_lora = w_dq.shape[1]
    kv_lora, n_heads, nope = w_uk.shape
    rope = w_dkv.shape[1] - kv_lora
    half = rope // 2
    sm_scale = float((nope + rope) ** -0.5)
    n_sample = t_all - n_prompt

    wr = w_dkv[:, kv_lora:]
    zr = jnp.zeros((d, ROPE_PAD - rope), F32)
    w_down = jnp.concatenate(
        [w_dq, w_dkv[:, :kv_lora], wr, zr, -wr[:, half:], wr[:, :half], zr], axis=1).astype(BF16)
    wq = w_uq.reshape(q_lora, n_heads, nope + rope)
    wq_r = wq[..., nope:]
    w_qa = jnp.concatenate(
        [wq, jnp.zeros((q_lora, n_heads, HEAD_PAD - nope - rope), F32)], axis=-1
    ).reshape(q_lora, n_heads * HEAD_PAD).astype(BF16)
    w_qb = jnp.concatenate(
        [-wq_r[..., half:], wq_r[..., :half], jnp.zeros((q_lora, n_heads, ROPE_PAD - rope), F32)], axis=-1
    ).reshape(q_lora, n_heads * ROPE_PAD).astype(BF16)
    w_k = w_uk.reshape(kv_lora, n_heads * nope).astype(BF16)
    w_v = w_uv.reshape(kv_lora, n_heads * nope).astype(BF16)
    w_ukt = jnp.transpose(w_uk, (1, 2, 0)).astype(BF16)
    w_uvh = jnp.transpose(w_uv, (1, 0, 2)).astype(BF16)
    w_ob = w_o.astype(BF16)

    tm = _tile(t_all, 512)
    n_down = q_lora + kv_lora + 2 * ROPE_PAD
    cq, ckv, ckvb, kr, krb = pl.pallas_call(
        functools.partial(_mla_down_kernel, q_lora=q_lora, kv_lora=kv_lora, rope=rope, sm_scale=sm_scale),
        grid=(t_all // tm,),
        in_specs=[_row_spec(tm, d), _resident((1, d)), _resident((d, n_down)), _resident((1, q_lora)),
                  _resident((1, kv_lora)), _row_spec(tm, ROPE_PAD), _row_spec(tm, ROPE_PAD)],
        out_specs=[_row_spec(tm, q_lora), _row_spec(tm, kv_lora), _row_spec(tm, kv_lora),
                   _row_spec(tm, rope), _row_spec(tm, ROPE_PAD)],
        out_shape=[jax.ShapeDtypeStruct((t_all, q_lora), BF16), jax.ShapeDtypeStruct((t_all, kv_lora), F32),
                   jax.ShapeDtypeStruct((t_all, kv_lora), BF16), jax.ShapeDtypeStruct((t_all, rope), F32),
                   jax.ShapeDtypeStruct((t_all, ROPE_PAD), BF16)],
        compiler_params=_params("parallel"), name="mla_down",
    )(x, g_mix.reshape(1, d), w_down, g_q.reshape(1, q_lora), g_kv.reshape(1, kv_lora), cos, sin)

    q = pl.pallas_call(
        functools.partial(_mla_q_kernel, n_heads=n_heads, nope=nope),
        grid=(t_all // tm,),
        in_specs=[_row_spec(tm, q_lora), _resident(w_qa.shape), _resident(w_qb.shape),
                  _row_spec(tm, ROPE_PAD), _row_spec(tm, ROPE_PAD)],
        out_specs=_row_spec(tm, n_heads * HEAD_PAD),
        out_shape=jax.ShapeDtypeStruct((t_all, n_heads * HEAD_PAD), BF16),
        compiler_params=_params("parallel"), name="mla_q",
    )(cq, w_qa, w_qb, cos, sin)

    tp = _tile(n_prompt, 512)
    k, v = pl.pallas_call(
        functools.partial(_mla_kv_kernel, n_heads=n_heads, nope=nope),
        grid=(n_prompt // tp,),
        in_specs=[_row_spec(tp, kv_lora), _row_spec(tp, ROPE_PAD), _resident(w_k.shape), _resident(w_v.shape)],
        out_specs=[_row_spec(tp, n_heads * HEAD_PAD), _row_spec(tp, n_heads * nope)],
        out_shape=[jax.ShapeDtypeStruct((n_prompt, n_heads * HEAD_PAD), BF16),
                   jax.ShapeDtypeStruct((n_prompt, n_heads * nope), BF16)],
        compiler_params=_params("parallel"), name="mla_kv",
    )(ckvb, krb, w_k, w_v)

    blk = _tile(seq, 512)
    nq = seq // blk
    n_batch = n_prompt // seq
    o = pl.pallas_call(
        functools.partial(_flash_kernel, blk=blk),
        grid=(n_batch, n_heads, nq),
        in_specs=[pl.BlockSpec((blk, HEAD_PAD), lambda b, h, i: (b * nq + i, h)),
                  pl.BlockSpec((seq, HEAD_PAD), lambda b, h, i: (b, h)),
                  pl.BlockSpec((seq, nope), lambda b, h, i: (b, h))],
        out_specs=pl.BlockSpec((blk, nope), lambda b, h, i: (b * nq + i, h)),
        out_shape=jax.ShapeDtypeStruct((t_all, n_heads * nope), BF16),
        scratch_shapes=[pltpu.VMEM((blk, 1), F32), pltpu.VMEM((blk, 1), F32), pltpu.VMEM((blk, nope), F32)],
        compiler_params=_params("parallel", "parallel", "arbitrary"), name="mla_flash",
    )(q, k, v)

    past = cache_c.shape[1]
    n_sb = n_sample // t_new
    row0 = n_prompt // t_new
    tk = _tile(past, 512)
    o = pl.pallas_call(
        functools.partial(_sample_attn_kernel, n_heads=n_heads, nope=nope, rope=rope, tk=tk, t_new=t_new),
        grid=(n_sb,),
        in_specs=[pl.BlockSpec((t_new, n_heads * HEAD_PAD), lambda b: (row0 + b, 0)),
                  _resident(w_ukt.shape), _resident(w_uvh.shape),
                  pl.BlockSpec((1, past, kv_lora), lambda b: (b, 0, 0)),
                  pl.BlockSpec((1, past, rope), lambda b: (b, 0, 0)),
                  pl.BlockSpec((t_new, kv_lora), lambda b: (row0 + b, 0)),
                  pl.BlockSpec((t_new, ROPE_PAD), lambda b: (row0 + b, 0)),
                  pl.BlockSpec(memory_space=pl.ANY)],
        out_specs=pl.BlockSpec((t_new, n_heads * nope), lambda b: (row0 + b, 0)),
        out_shape=jax.ShapeDtypeStruct((t_all, n_heads * nope), BF16),
        input_output_aliases={7: 0},
        scratch_shapes=[pltpu.VMEM((n_heads * t_new, kv_lora), BF16), pltpu.VMEM((n_heads * t_new, rope), BF16),
                        pltpu.VMEM((n_heads * t_new, 1), F32), pltpu.VMEM((n_heads * t_new, 1), F32),
                        pltpu.VMEM((n_heads * t_new, kv_lora), F32)],
        compiler_params=_params("parallel"), name="mla_sample_attn",
    )(q, w_ukt, w_uvh, cache_c, cache_r, ckvb, krb, o)

    x = pl.pallas_call(
        _proj_res_kernel,
        grid=(t_all // tm,),
        in_specs=[_row_spec(tm, n_heads * nope), _resident(w_ob.shape), _row_spec(tm, d)],
        out_specs=_row_spec(tm, d),
        out_shape=jax.ShapeDtypeStruct((t_all, d), F32),
        compiler_params=_params("parallel"), name="mla_out",
    )(o, w_ob, x)
    return x, ckv, kr


def _conv_layer(x, n_prompt, seq, t_new, state, g_mix, w_in, conv_w, w_out):
    t_all, d = x.shape
    w_inb = w_in.astype(BF16)
    w_outb = w_out.astype(BF16)
    tm = _tile(t_all, 512)
    tn = _tile(d, 512)
    nn = d // tn
    u, gb = pl.pallas_call(
        _conv_in_kernel,
        grid=(t_all // tm, nn),
        in_specs=[pl.BlockSpec((tm, d), lambda i, n: (i, 0)), _resident((1, d)),
                  pl.BlockSpec((d, tn), lambda i, n: (0, n)),
                  pl.BlockSpec((d, tn), lambda i, n: (0, nn + n)),
                  pl.BlockSpec((d, tn), lambda i, n: (0, 2 * nn + n))],
        out_specs=[pl.BlockSpec((tm, tn), lambda i, n: (i, n)), pl.BlockSpec((tm, tn), lambda i, n: (i, n))],
        out_shape=[jax.ShapeDtypeStruct((t_all, d), F32), jax.ShapeDtypeStruct((t_all, d), BF16)],
        scratch_shapes=[pltpu.VMEM((tm, d), BF16)],
        compiler_params=_params("parallel", "arbitrary"), name="conv_in",
    )(x, g_mix.reshape(1, d), w_inb, w_inb, w_inb)

    tp = _tile(seq, 512)
    ns = seq // tp
    n_batch = n_prompt // seq
    hb = tp // 8
    x_p = pl.pallas_call(
        _conv_out_prompt_kernel,
        grid=(n_batch, ns),
        in_specs=[pl.BlockSpec((tp, d), lambda b, i: (b * ns + i, 0)),
                  pl.BlockSpec((8, d), lambda b, i: (jnp.maximum((b * ns + i) * hb - 1, 0), 0)),
                  pl.BlockSpec((tp, d), lambda b, i: (b * ns + i, 0)),
                  pl.BlockSpec((tp, d), lambda b, i: (b * ns + i, 0)),
                  _resident((conv_w.shape[0], d)), _resident((d, d))],
        out_specs=pl.BlockSpec((tp, d), lambda b, i: (b * ns + i, 0)),
        out_shape=jax.ShapeDtypeStruct((t_all, d), F32),
        compiler_params=_params("parallel", "arbitrary"), name="conv_out_prompt",
    )(u, u, gb, x, conv_w, w_outb)

    n_sample = t_all - n_prompt
    ts = _tile(n_sample, 512)
    off = n_prompt // ts
    nb = ts // t_new
    x_new = pl.pallas_call(
        functools.partial(_conv_out_sample_kernel, t_new=t_new),
        grid=(n_sample // ts,),
        in_specs=[_row_spec(ts, d, off), pl.BlockSpec((nb, state.shape[1], d), lambda i: (i, 0, 0)),
                  _row_spec(ts, d, off), _row_spec(ts, d, off),
                  _resident((conv_w.shape[0], d)), _resident((d, d)), pl.BlockSpec(memory_space=pl.ANY)],
        out_specs=_row_spec(ts, d, off),
        out_shape=jax.ShapeDtypeStruct((t_all, d), F32),
        input_output_aliases={6: 0},
        compiler_params=_params("parallel"), name="conv_out_sample",
    )(u, state, gb, x, conv_w, w_outb, x_p)
    return x_new, u


def _ffn_layer(x, g, w1, w2):
    t_all, d = x.shape
    f = w1.shape[1]
    tm = _tile(t_all, 512)
    tf = _tile(f, 1024)
    return pl.pallas_call(
        _ffn_kernel,
        grid=(t_all // tm, f // tf),
        in_specs=[pl.BlockSpec((tm, d), lambda i, j: (i, 0)), _resident((1, d)),
                  pl.BlockSpec((d, tf), lambda i, j: (0, j)), pl.BlockSpec((tf, d), lambda i, j: (j, 0))],
        out_specs=pl.BlockSpec((tm, d), lambda i, j: (i, 0)),
        out_shape=jax.ShapeDtypeStruct((t_all, d), F32),
        scratch_shapes=[pltpu.VMEM((tm, d), BF16)],
        compiler_params=_params("parallel", "arbitrary"), name="ffn",
    )(x, g.reshape(1, d), w1.astype(BF16), w2.astype(BF16))


def _final_norm(x, g, rows, off_rows):
    d = x.shape[1]
    tm = _tile(rows, 512)
    return pl.pallas_call(
        _final_norm_kernel,
        grid=(rows // tm,),
        in_specs=[_row_spec(tm, d, off_rows // tm), _resident((1, d))],
        out_specs=_row_spec(tm, d),
        out_shape=jax.ShapeDtypeStruct((rows, d), F32),
        compiler_params=_params("parallel"), name="final_norm",
    )(x, g.reshape(1, d))


def kernel(x_prompt, x_sample, cache_ckv, cache_krope, state_conv, norm_mix, w_dq, g_q, w_uq, w_dkv, g_kv,
           w_uk, w_uv, w_o, w_in, conv_w, w_out, norm_ffn, w_ff1, w_ff2, norm_final):
    n_batch, seq, d = x_prompt.shape
    s_batch, t_new, _ = x_sample.shape
    past = cache_ckv.shape[2]
    depth = norm_mix.shape[0]
    kv_lora = w_uk.shape[1]
    rope = w_dkv.shape[2] - kv_lora
    half = rope // 2
    n_prompt = n_batch * seq
    n_sample = s_batch * t_new

    x = jnp.concatenate([x_prompt.reshape(n_prompt, d), x_sample.reshape(n_sample, d)], axis=0)

    pos = jnp.concatenate([jnp.tile(jnp.arange(seq, dtype=jnp.int32), n_batch),
                           jnp.tile(past + jnp.arange(t_new, dtype=jnp.int32), s_batch)])
    inv = ROPE_THETA ** (-jnp.arange(0, half, dtype=F32) / half)
    ang = pos.astype(F32)[:, None] * inv[None, :]
    zpad = jnp.zeros((pos.shape[0], ROPE_PAD - rope), F32)
    cos = jnp.concatenate([jnp.cos(ang), jnp.cos(ang), zpad], axis=1)
    sin = jnp.concatenate([jnp.sin(ang), jnp.sin(ang), zpad], axis=1)

    new_c, new_r, new_u = [], [], []
    for i in range(depth):
        j = i // 2
        if i % 2 == 0:
            x, ckv, kr = _mla_layer(x, n_prompt, seq, t_new, cos, sin, cache_ckv[j], cache_krope[j],
                                    norm_mix[i], w_dq[j], g_q[j], w_uq[j], w_dkv[j], g_kv[j],
                                    w_uk[j], w_uv[j], w_o[j])
            new_c.append(ckv)
            new_r.append(kr)
        else:
            x, u = _conv_layer(x, n_prompt, seq, t_new, state_conv[j], norm_mix[i], w_in[j], conv_w[j], w_out[j])
            new_u.append(u)
        x = _ffn_layer(x, norm_ffn[i], w_ff1[i], w_ff2[i])

    y_prompt = _final_norm(x, norm_final, n_prompt, 0).reshape(n_batch, seq, d)
    y_sample = _final_norm(x, norm_final, n_sample, n_prompt).reshape(s_batch, t_new, d)

    def split(a, lead, inner):
        return a.reshape((lead, inner) + a.shape[1:])

    keep = conv_w.shape[1] - 1
    ckv_p = jnp.stack([split(c[:n_prompt], n_batch, seq) for c in new_c])
    kr_p = jnp.stack([split(r[:n_prompt], n_batch, seq) for r in new_r])
    u_p = jnp.stack([split(u[:n_prompt], n_batch, seq)[:, seq - keep:] for u in new_u])
    ckv_s = jnp.stack([split(c[n_prompt:], s_batch, t_new) for c in new_c])
    kr_s = jnp.stack([split(r[n_prompt:], s_batch, t_new) for r in new_r])
    u_s = jnp.stack([split(u[n_prompt:], s_batch, t_new)[:, t_new - keep:] for u in new_u])
    return (y_prompt, y_sample, ckv_p, kr_p, u_p, ckv_s, kr_s, u_s)
```

```python
import functools
import math

import jax
import jax.numpy as jnp
from jax import lax
from jax.experimental import pallas as pl
from jax.experimental.pallas import tpu as pltpu

CHUNK = 64
ROPE_THETA = 10000.0
EPS = 1e-6
HEAD_PAD = 256
ROPE_PAD = 128
VMEM_LIMIT = 56 * 1024 * 1024

F32 = jnp.float32
BF16 = jnp.bfloat16


def _params(*sem):
    return pltpu.CompilerParams(dimension_semantics=sem, vmem_limit_bytes=VMEM_LIMIT)


def _tile(total, pref):
    t = min(total, pref)
    assert total % t == 0, (total, pref)
    return t


def _resident(shape):
    nd = len(shape)
    return pl.BlockSpec(shape, lambda *_: (0,) * nd, pipeline_mode=pl.Buffered(1))


def _rms(xf, g):
    return xf * lax.rsqrt(jnp.mean(xf * xf, axis=-1, keepdims=True) + EPS) * g


def _dot(a, b):
    return jnp.dot(a, b, preferred_element_type=F32)


def _dot_nt(a, b):
    return lax.dot_general(a, b, (((1,), (1,)), ((), ())), preferred_element_type=F32)


def _mla_down_kernel(x_ref, g_ref, w_ref, gq_ref, gkv_ref, cos_ref, sin_ref,
                     cq_ref, cqt_ref, ckv_ref, ckvb_ref, ckvt_ref, kr_ref, krb_ref,
                     *, q_lora, kv_lora, rope, q_scale):
    h = _rms(x_ref[...], g_ref[...]).astype(BF16)
    y = _dot(h, w_ref[...])
    cq = _rms(y[:, :q_lora], gq_ref[...]) * q_scale
    cq_ref[...] = cq.astype(BF16)
    cqt_ref[...] = cq.T.astype(BF16)
    ckv = _rms(y[:, q_lora:q_lora + kv_lora], gkv_ref[...])
    ckv_ref[...] = ckv
    ckvb_ref[...] = ckv.astype(BF16)
    ckvt_ref[...] = ckv.T.astype(BF16)
    o = q_lora + kv_lora
    kr = y[:, o:o + ROPE_PAD] * cos_ref[...] + y[:, o + ROPE_PAD:o + 2 * ROPE_PAD] * sin_ref[...]
    kr_ref[...] = kr[:, :rope]
    krb_ref[...] = kr.astype(BF16)


def _mla_q_kernel(cq_ref, wa_ref, wb_ref, cos_ref, sin_ref, q_ref, *, n_heads, nope):
    cq = cq_ref[...]
    cos = cos_ref[...]
    sin = sin_ref[...]
    for hp in range(n_heads // 2):
        a = _dot(cq, wa_ref[:, hp * 2 * HEAD_PAD:(hp + 1) * 2 * HEAD_PAD])
        b = _dot(cq, wb_ref[:, hp * 2 * ROPE_PAD:(hp + 1) * 2 * ROPE_PAD])
        for s in range(2):
            c0 = (2 * hp + s) * HEAD_PAD
            q_ref[:, c0:c0 + nope] = a[:, s * HEAD_PAD:s * HEAD_PAD + nope].astype(BF16)
            ar = a[:, s * HEAD_PAD + nope:(s + 1) * HEAD_PAD]
            br = b[:, s * ROPE_PAD:(s + 1) * ROPE_PAD]
            q_ref[:, c0 + nope:c0 + HEAD_PAD] = (ar * cos + br * sin).astype(BF16)


def _mla_qt_kernel(cqt_ref, wat_ref, wbt_ref, cost_ref, sint_ref, qt_ref, *, n_heads, nope):
    cqt = cqt_ref[...]
    cost = cost_ref[...]
    sint = sint_ref[...]
    for hp in range(n_heads // 2):
        a = _dot(wat_ref[hp * 2 * HEAD_PAD:(hp + 1) * 2 * HEAD_PAD, :], cqt)
        b = _dot(wbt_ref[hp * 2 * ROPE_PAD:(hp + 1) * 2 * ROPE_PAD, :], cqt)
        for s in range(2):
            r0 = (2 * hp + s) * HEAD_PAD
            qt_ref[r0:r0 + nope, :] = a[s * HEAD_PAD:s * HEAD_PAD + nope].astype(BF16)
            ar = a[s * HEAD_PAD + nope:(s + 1) * HEAD_PAD]
            br = b[s * ROPE_PAD:(s + 1) * ROPE_PAD]
            qt_ref[r0 + nope:r0 + HEAD_PAD, :] = (ar * cost + br * sint).astype(BF16)


def _mla_kv_kernel(ckvb_ref, ckvt_ref, krb_ref, wk_ref, wvt_ref, k_ref, vt_ref, *, n_heads, nope):
    c = ckvb_ref[...]
    krb = krb_ref[...]
    vt_ref[...] = _dot(wvt_ref[...], ckvt_ref[...]).astype(BF16)
    for hp in range(n_heads // 2):
        kn = _dot(c, wk_ref[:, hp * 2 * nope:(hp + 1) * 2 * nope])
        for s in range(2):
            c0 = (2 * hp + s) * HEAD_PAD
            k_ref[:, c0:c0 + nope] = kn[:, s * nope:(s + 1) * nope].astype(BF16)
            k_ref[:, c0 + nope:c0 + HEAD_PAD] = krb


def _flash_kernel(qt_ref, k_ref, vt_ref, o_ref, *, tq, kb, cw):
    qi = pl.program_id(2)
    ncol = tq // cw
    nsub = tq // kb
    v_head = vt_ref.shape[0]

    def scores(base, t, c_lo):
        start = pl.multiple_of(base + t * kb, kb)
        return _dot(k_ref[pl.ds(start, kb), :], qt_ref[:, c_lo * cw:])

    def update(state, s, base, t, c_lo, diag):
        start = pl.multiple_of(base + t * kb, kb)
        vt = vt_ref[:, pl.ds(start, kb)]
        new = list(state)
        for c in range(c_lo, ncol):
            sc = s[:, (c - c_lo) * cw:(c - c_lo + 1) * cw]
            if diag and (t + 1) * kb > c * cw:
                kc = (lax.broadcasted_iota(jnp.int32, (kb, cw), 0) + t * kb) // CHUNK
                qc = (lax.broadcasted_iota(jnp.int32, (kb, cw), 1) + c * cw) // CHUNK
                sc = jnp.where(kc <= qc, sc, -jnp.inf)
            m_prev, l_prev, acc_prev = state[c]
            m_new = jnp.maximum(m_prev, jnp.max(sc, axis=0, keepdims=True))
            alpha = jnp.exp2(m_prev - m_new)
            p = jnp.exp2(sc - m_new)
            l_new = alpha * l_prev + jnp.sum(p, axis=0, keepdims=True)
            acc_new = alpha * acc_prev + _dot(vt, p.astype(BF16))
            new[c] = (m_new, l_new, acc_new)
        return tuple(new)

    def trip(base, state, diag):
        def c_lo(t):
            return (t * kb) // cw if diag else 0

        s = scores(base, 0, c_lo(0))
        for t in range(nsub):
            s_next = scores(base, t + 1, c_lo(t + 1)) if t + 1 < nsub else None
            state = update(state, s, base, t, c_lo(t), diag)
            s = s_next
        return state

    init = tuple((jnp.full((1, cw), -jnp.inf, F32), jnp.zeros((1, cw), F32), jnp.zeros((v_head, cw), F32))
                 for _ in range(ncol))
    state = lax.fori_loop(0, qi, lambda j, st: trip(j * tq, st, False), init)
    state = trip(qi * tq, state, True)
    for c in range(ncol):
        _, l_fin, acc_fin = state[c]
        o_ref[c * cw:(c + 1) * cw, :] = (acc_fin / l_fin).T.astype(BF16)


def _sample_attn_kernel(q_ref, wukt_ref, wuv_ref, cc_ref, cr_ref, nc_ref, nr_ref, o_in_ref, o_ref,
                        ql_ref, qr_ref, m_ref, l_ref, acc_ref, *, n_heads, nope, rope, tk, t_new):
    del o_in_ref
    for h in range(n_heads):
        c0 = h * HEAD_PAD
        ql_ref[h * t_new:(h + 1) * t_new, :] = _dot(q_ref[:, c0:c0 + nope], wukt_ref[h]).astype(BF16)
        qr_ref[h * t_new:(h + 1) * t_new, :] = q_ref[:, c0 + nope:c0 + nope + rope]
    ql = ql_ref[...]
    qr = qr_ref[...]
    m_ref[...] = jnp.full(m_ref.shape, -jnp.inf, F32)
    l_ref[...] = jnp.zeros(l_ref.shape, F32)
    acc_ref[...] = jnp.zeros(acc_ref.shape, F32)

    def update(c, r):
        s = _dot_nt(ql, c) + _dot_nt(qr, r)
        m_prev = m_ref[...]
        m_new = jnp.maximum(m_prev, jnp.max(s, axis=-1, keepdims=True))
        alpha = jnp.exp2(m_prev - m_new)
        p = jnp.exp2(s - m_new)
        l_ref[...] = alpha * l_ref[...] + jnp.sum(p, axis=-1, keepdims=True)
        acc_ref[...] = alpha * acc_ref[...] + _dot(p.astype(BF16), c)
        m_ref[...] = m_new

    def body(j, carry):
        start = pl.multiple_of(j * tk, tk)
        update(cc_ref[pl.ds(start, tk), :].astype(BF16), cr_ref[pl.ds(start, tk), :].astype(BF16))
        return carry

    lax.fori_loop(0, cc_ref.shape[0] // tk, body, 0)
    update(nc_ref[...], nr_ref[:, :rope])
    ol = (acc_ref[...] / l_ref[...]).astype(BF16)
    for h in range(n_heads):
        o_ref[:, h * nope:(h + 1) * nope] = _dot(ol[h * t_new:(h + 1) * t_new, :], wuv_ref[h]).astype(BF16)


def _proj_res_kernel(a_ref, w_ref, x_ref, o_ref):
    o_ref[...] = x_ref[...] + _dot(a_ref[...], w_ref[...])


def _ffn_kernel(x_ref, g_ref, w1_ref, w2_ref, o_ref, h_ref):
    @pl.when(pl.program_id(1) == 0)
    def _():
        x = x_ref[...]
        h_ref[...] = _rms(x, g_ref[...]).astype(BF16)
        o_ref[...] = x

    a = jnp.maximum(_dot(h_ref[...], w1_ref[...]), 0.0)
    o_ref[...] += _dot((a * a).astype(BF16), w2_ref[...])


def _conv_in_kernel(x_ref, g_ref, wb_ref, wc_ref, wv_ref, u_ref, gb_ref, h_ref):
    @pl.when(pl.program_id(1) == 0)
    def _():
        h_ref[...] = _rms(x_ref[...], g_ref[...]).astype(BF16)

    h = h_ref[...]
    gb_ref[...] = _dot(h, wb_ref[...]).astype(BF16)
    u_ref[...] = _dot(h, wc_ref[...]) * _dot(h, wv_ref[...])


def _conv_mix(u, p1, p2, first, second, gb, cw, wout, x):
    u1 = jnp.where(first, p1, pltpu.roll(u, 1, axis=0))
    u2 = jnp.where(first, p2, jnp.where(second, p1, pltpu.roll(u, 2, axis=0)))
    conv = cw[0:1] * u2 + cw[1:2] * u1 + cw[2:3] * u
    return x + _dot((gb.astype(F32) * conv).astype(BF16), wout)


def _conv_out_prompt_kernel(u_ref, halo_ref, gb_ref, x_ref, cw_ref, wout_ref, o_ref):
    tm = u_ref.shape[0]
    has_past = pl.program_id(1) > 0
    halo = halo_ref[...]
    p1 = jnp.where(has_past, halo[7:8], 0.0)
    p2 = jnp.where(has_past, halo[6:7], 0.0)
    row = lax.broadcasted_iota(jnp.int32, (tm, 1), 0)
    o_ref[...] = _conv_mix(u_ref[...], p1, p2, row == 0, row == 1, gb_ref[...], cw_ref[...],
                           wout_ref[...], x_ref[...])


def _conv_out_sample_kernel(u_ref, st_ref, gb_ref, x_ref, cw_ref, wout_ref, xp_ref, o_ref, *, t_new):
    del xp_ref
    tm, d = u_ref.shape
    nb = tm // t_new
    st = st_ref[...]
    p1 = jnp.broadcast_to(st[:, 1:2, :], (nb, t_new, d)).reshape(tm, d)
    p2 = jnp.broadcast_to(st[:, 0:1, :], (nb, t_new, d)).reshape(tm, d)
    t = lax.broadcasted_iota(jnp.int32, (tm, 1), 0) % t_new
    o_ref[...] = _conv_mix(u_ref[...], p1, p2, t == 0, t == 1, gb_ref[...], cw_ref[...],
                           wout_ref[...], x_ref[...])


def _final_norm_kernel(x_ref, g_ref, o_ref):
    o_ref[...] = _rms(x_ref[...], g_ref[...])


def _row_spec(tm, n, off=0):
    return pl.BlockSpec((tm, n), lambda i, *_: (i + off, 0))


def _col_spec(n, tm):
    return pl.BlockSpec((n, tm), lambda i, *_: (0, i))


def _mla_layer(x, j, n_prompt, seq, t_new, tabs, cache_c, cache_r, g_mix, w_dq, g_q, w_uq, w_dkv, g_kv,
               w_uk, w_uv, w_o):
    cos, sin, cost, sint = tabs
    t_all, d = x.shape
    q_lora = w_dq.shape[1]
    kv_lora, n_heads, nope = w_uk.shape
    rope = w_dkv.shape[1] - kv_lora
    half = rope // 2
    q_scale = float((nope + rope) ** -0.5 * math.log2(math.e))
    n_sample = t_all - n_prompt

    wr = w_dkv[:, kv_lora:]
    zr = jnp.zeros((d, ROPE_PAD - rope), F32)
    w_down = jnp.concatenate(
        [w_dq, w_dkv[:, :kv_lora], wr, zr, -wr[:, half:], wr[:, :half], zr], axis=1).astype(BF16)
    wq = w_uq.reshape(q_lora, n_heads, nope + rope)
    wq_r = wq[..., nope:]
    w_qa = jnp.concatenate(
        [wq, jnp.zeros((q_lora, n_heads, HEAD_PAD - nope - rope), F32)], axis=-1
    ).reshape(q_lora, n_heads * HEAD_PAD).astype(BF16)
    w_qb = jnp.concatenate(
        [-wq_r[..., half:], wq_r[..., :half], jnp.zeros((q_lora, n_heads, ROPE_PAD - rope), F32)], axis=-1
    ).reshape(q_lora, n_heads * ROPE_PAD).astype(BF16)
    w_k = w_uk.reshape(kv_lora, n_heads * nope).astype(BF16)
    w_vt = w_uv.reshape(kv_lora, n_heads * nope).T.astype(BF16)
    w_ukt = jnp.transpose(w_uk, (1, 2, 0)).astype(BF16)
    w_uvh = jnp.transpose(w_uv, (1, 0, 2)).astype(BF16)
    w_ob = w_o.astype(BF16)

    tm = _tile(t_all, 512)
    n_down = q_lora + kv_lora + 2 * ROPE_PAD
    cq, cqt, ckv, ckvb, ckvt, kr, krb = pl.pallas_call(
        functools.partial(_mla_down_kernel, q_lora=q_lora, kv_lora=kv_lora, rope=rope, q_scale=q_scale),
        grid=(t_all // tm,),
        in_specs=[_row_spec(tm, d), _resident((1, d)), _resident((d, n_down)), _resident((1, q_lora)),
                  _resident((1, kv_lora)), _row_spec(tm, ROPE_PAD), _row_spec(tm, ROPE_PAD)],
        out_specs=[_row_spec(tm, q_lora), _col_spec(q_lora, tm), _row_spec(tm, kv_lora),
                   _row_spec(tm, kv_lora), _col_spec(kv_lora, tm), _row_spec(tm, rope),
                   _row_spec(tm, ROPE_PAD)],
        out_shape=[jax.ShapeDtypeStruct((t_all, q_lora), BF16), jax.ShapeDtypeStruct((q_lora, t_all), BF16),
                   jax.ShapeDtypeStruct((t_all, kv_lora), F32), jax.ShapeDtypeStruct((t_all, kv_lora), BF16),
                   jax.ShapeDtypeStruct((kv_lora, t_all), BF16), jax.ShapeDtypeStruct((t_all, rope), F32),
                   jax.ShapeDtypeStruct((t_all, ROPE_PAD), BF16)],
        compiler_params=_params("parallel"), name="mla_down",
    )(x, g_mix.reshape(1, d), w_down, g_q.reshape(1, q_lora), g_kv.reshape(1, kv_lora), cos, sin)

    tp = _tile(n_prompt, 512)
    qt = pl.pallas_call(
        functools.partial(_mla_qt_kernel, n_heads=n_heads, nope=nope),
        grid=(n_prompt // tp,),
        in_specs=[_col_spec(q_lora, tp), _resident((n_heads * HEAD_PAD, q_lora)),
                  _resident((n_heads * ROPE_PAD, q_lora)), _col_spec(ROPE_PAD, tp), _col_spec(ROPE_PAD, tp)],
        out_specs=_col_spec(n_heads * HEAD_PAD, tp),
        out_shape=jax.ShapeDtypeStruct((n_heads * HEAD_PAD, n_prompt), BF16),
        compiler_params=_params("parallel"), name="mla_qt",
    )(cqt, w_qa.T, w_qb.T, cost, sint)

    ts = _tile(n_sample, 512)
    off = n_prompt // ts
    q_s = pl.pallas_call(
        functools.partial(_mla_q_kernel, n_heads=n_heads, nope=nope),
        grid=(n_sample // ts,),
        in_specs=[_row_spec(ts, q_lora, off), _resident(w_qa.shape), _resident(w_qb.shape),
                  _row_spec(ts, ROPE_PAD, off), _row_spec(ts, ROPE_PAD, off)],
        out_specs=_row_spec(ts, n_heads * HEAD_PAD),
        out_shape=jax.ShapeDtypeStruct((n_sample, n_heads * HEAD_PAD), BF16),
        compiler_params=_params("parallel"), name="mla_q",
    )(cq, w_qa, w_qb, cos, sin)

    k, vt = pl.pallas_call(
        functools.partial(_mla_kv_kernel, n_heads=n_heads, nope=nope),
        grid=(n_prompt // tp,),
        in_specs=[_row_spec(tp, kv_lora), _col_spec(kv_lora, tp), _row_spec(tp, ROPE_PAD),
                  _resident(w_k.shape), _resident(w_vt.shape)],
        out_specs=[_row_spec(tp, n_heads * HEAD_PAD), _col_spec(n_heads * nope, tp)],
        out_shape=[jax.ShapeDtypeStruct((n_prompt, n_heads * HEAD_PAD), BF16),
                   jax.ShapeDtypeStruct((n_heads * nope, n_prompt), BF16)],
        compiler_params=_params("parallel"), name="mla_kv",
    )(ckvb, ckvt, krb, w_k, w_vt)

    blk = _tile(seq, 1024)
    nq = seq // blk
    n_batch = n_prompt // seq
    o = pl.pallas_call(
        functools.partial(_flash_kernel, tq=blk, kb=_tile(blk, 256), cw=_tile(blk, 256)),
        grid=(n_batch, n_heads, nq),
        in_specs=[pl.BlockSpec((HEAD_PAD, blk), lambda b, h, i: (h, b * nq + i)),
                  pl.BlockSpec((seq, HEAD_PAD), lambda b, h, i: (b, h)),
                  pl.BlockSpec((nope, seq), lambda b, h, i: (h, b))],
        out_specs=pl.BlockSpec((blk, nope), lambda b, h, i: (b * nq + i, h)),
        out_shape=jax.ShapeDtypeStruct((t_all, n_heads * nope), BF16),
        compiler_params=_params("parallel", "parallel", "arbitrary"), name="mla_flash",
    )(qt, k, vt)

    past = cache_c.shape[2]
    n_sb = n_sample // t_new
    row0 = n_prompt // t_new
    tk = _tile(past, 512)
    o = pl.pallas_call(
        functools.partial(_sample_attn_kernel, n_heads=n_heads, nope=nope, rope=rope, tk=tk, t_new=t_new),
        grid=(n_sb,),
        in_specs=[pl.BlockSpec((t_new, n_heads * HEAD_PAD), lambda b: (b, 0)),
                  _resident(w_ukt.shape), _resident(w_uvh.shape),
                  pl.BlockSpec((None, None, past, kv_lora), lambda b: (j, b, 0, 0)),
                  pl.BlockSpec((None, None, past, rope), lambda b: (j, b, 0, 0)),
                  pl.BlockSpec((t_new, kv_lora), lambda b: (row0 + b, 0)),
                  pl.BlockSpec((t_new, ROPE_PAD), lambda b: (row0 + b, 0)),
                  pl.BlockSpec(memory_space=pl.ANY)],
        out_specs=pl.BlockSpec((t_new, n_heads * nope), lambda b: (row0 + b, 0)),
        out_shape=jax.ShapeDtypeStruct((t_all, n_heads * nope), BF16),
        input_output_aliases={7: 0},
        scratch_shapes=[pltpu.VMEM((n_heads * t_new, kv_lora), BF16), pltpu.VMEM((n_heads * t_new, rope), BF16),
                        pltpu.VMEM((n_heads * t_new, 1), F32), pltpu.VMEM((n_heads * t_new, 1), F32),
                        pltpu.VMEM((n_heads * t_new, kv_lora), F32)],
        compiler_params=_params("parallel"), name="mla_sample_attn",
    )(q_s, w_ukt, w_uvh, cache_c, cache_r, ckvb, krb, o)

    x = pl.pallas_call(
        _proj_res_kernel,
        grid=(t_all // tm,),
        in_specs=[_row_spec(tm, n_heads * nope), _resident(w_ob.shape), _row_spec(tm, d)],
        out_specs=_row_spec(tm, d),
        out_shape=jax.ShapeDtypeStruct((t_all, d), F32),
        compiler_params=_params("parallel"), name="mla_out",
    )(o, w_ob, x)
    return x, ckv, kr


def _conv_layer(x, n_prompt, seq, t_new, state, g_mix, w_in, conv_w, w_out):
    assert conv_w.shape[0] == 3 and state.shape[1] == 2
    t_all, d = x.shape
    w_inb = w_in.astype(BF16)
    w_outb = w_out.astype(BF16)
    tm = _tile(t_all, 512)
    tn = _tile(d, 512)
    nn = d // tn
    u, gb = pl.pallas_call(
        _conv_in_kernel,
        grid=(t_all // tm, nn),
        in_specs=[pl.BlockSpec((tm, d), lambda i, n: (i, 0)), _resident((1, d)),
                  pl.BlockSpec((d, tn), lambda i, n: (0, n)),
                  pl.BlockSpec((d, tn), lambda i, n: (0, nn + n)),
                  pl.BlockSpec((d, tn), lambda i, n: (0, 2 * nn + n))],
        out_specs=[pl.BlockSpec((tm, tn), lambda i, n: (i, n)), pl.BlockSpec((tm, tn), lambda i, n: (i, n))],
        out_shape=[jax.ShapeDtypeStruct((t_all, d), F32), jax.ShapeDtypeStruct((t_all, d), BF16)],
        scratch_shapes=[pltpu.VMEM((tm, d), BF16)],
        compiler_params=_params("parallel", "arbitrary"), name="conv_in",
    )(x, g_mix.reshape(1, d), w_inb, w_inb, w_inb)

    tp = _tile(seq, 512)
    ns = seq // tp
    n_batch = n_prompt // seq
    hb = tp // 8
    x_p = pl.pallas_call(
        _conv_out_prompt_kernel,
        grid=(n_batch, ns),
        in_specs=[pl.BlockSpec((tp, d), lambda b, i: (b * ns + i, 0)),
                  pl.BlockSpec((8, d), lambda b, i: (jnp.maximum((b * ns + i) * hb - 1, 0), 0)),
                  pl.BlockSpec((tp, d), lambda b, i: (b * ns + i, 0)),
                  pl.BlockSpec((tp, d), lambda b, i: (b * ns + i, 0)),
                  _resident((conv_w.shape[0], d)), _resident((d, d))],
        out_specs=pl.BlockSpec((tp, d), lambda b, i: (b * ns + i, 0)),
        out_shape=jax.ShapeDtypeStruct((t_all, d), F32),
        compiler_params=_params("parallel", "arbitrary"), name="conv_out_prompt",
    )(u, u, gb, x, conv_w, w_outb)

    n_sample = t_all - n_prompt
    ts = _tile(n_sample, 512)
    off = n_prompt // ts
    nb = ts // t_new
    x_new = pl.pallas_call(
        functools.partial(_conv_out_sample_kernel, t_new=t_new),
        grid=(n_sample // ts,),
        in_specs=[_row_spec(ts, d, off), pl.BlockSpec((nb, state.shape[1], d), lambda i: (i, 0, 0)),
                  _row_spec(ts, d, off), _row_spec(ts, d, off),
                  _resident((conv_w.shape[0], d)), _resident((d, d)), pl.BlockSpec(memory_space=pl.ANY)],
        out_specs=_row_spec(ts, d, off),
        out_shape=jax.ShapeDtypeStruct((t_all, d), F32),
        input_output_aliases={6: 0},
        compiler_params=_params("parallel"), name="conv_out_sample",
    )(u, state, gb, x, conv_w, w_outb, x_p)
    return x_new, u


def _ffn_layer(x, layer, g, w1b, w2b):
    t_all, d = x.shape
    f = w1b.shape[2]
    tm = _tile(t_all, 512)
    tf = _tile(f, 1024)
    return pl.pallas_call(
        _ffn_kernel,
        grid=(t_all // tm, f // tf),
        in_specs=[pl.BlockSpec((tm, d), lambda i, c: (i, 0)), _resident((1, d)),
                  pl.BlockSpec((None, d, tf), lambda i, c: (layer, 0, c)),
                  pl.BlockSpec((None, tf, d), lambda i, c: (layer, c, 0))],
        out_specs=pl.BlockSpec((tm, d), lambda i, c: (i, 0)),
        out_shape=jax.ShapeDtypeStruct((t_all, d), F32),
        scratch_shapes=[pltpu.VMEM((tm, d), BF16)],
        compiler_params=_params("parallel", "arbitrary"), name="ffn",
    )(x, g.reshape(1, d), w1b, w2b)


def _final_norm(x, g, rows, off_rows):
    d = x.shape[1]
    tm = _tile(rows, 512)
    return pl.pallas_call(
        _final_norm_kernel,
        grid=(rows // tm,),
        in_specs=[_row_spec(tm, d, off_rows // tm), _resident((1, d))],
        out_specs=_row_spec(tm, d),
        out_shape=jax.ShapeDtypeStruct((rows, d), F32),
        compiler_params=_params("parallel"), name="final_norm",
    )(x, g.reshape(1, d))


def kernel(x_prompt, x_sample, cache_ckv, cache_krope, state_conv, norm_mix, w_dq, g_q, w_uq, w_dkv, g_kv,
           w_uk, w_uv, w_o, w_in, conv_w, w_out, norm_ffn, w_ff1, w_ff2, norm_final):
    n_batch, seq, d = x_prompt.shape
    s_batch, t_new, _ = x_sample.shape
    past = cache_ckv.shape[2]
    depth = norm_mix.shape[0]
    kv_lora = w_uk.shape[1]
    rope = w_dkv.shape[2] - kv_lora
    half = rope // 2
    n_prompt = n_batch * seq
    n_sample = s_batch * t_new

    x = jnp.concatenate([x_prompt.reshape(n_prompt, d), x_sample.reshape(n_sample, d)], axis=0)

    pos = jnp.concatenate([jnp.tile(jnp.arange(seq, dtype=jnp.int32), n_batch),
                           jnp.tile(past + jnp.arange(t_new, dtype=jnp.int32), s_batch)])
    inv = ROPE_THETA ** (-jnp.arange(0, half, dtype=F32) / half)
    ang = pos.astype(F32)[:, None] * inv[None, :]
    zpad = jnp.zeros((pos.shape[0], ROPE_PAD - rope), F32)
    cos = jnp.concatenate([jnp.cos(ang), jnp.cos(ang), zpad], axis=1)
    sin = jnp.concatenate([jnp.sin(ang), jnp.sin(ang), zpad], axis=1)
    tabs = (cos, sin, cos[:n_prompt].T, sin[:n_prompt].T)

    w1b = w_ff1.astype(BF16)
    w2b = w_ff2.astype(BF16)

    new_c, new_r, new_u = [], [], []
    for i in range(depth):
        j = i // 2
        if i % 2 == 0:
            x, ckv, kr = _mla_layer(x, j, n_prompt, seq, t_new, tabs, cache_ckv, cache_krope,
                                    norm_mix[i], w_dq[j], g_q[j], w_uq[j], w_dkv[j], g_kv[j],
                                    w_uk[j], w_uv[j], w_o[j])
            new_c.append(ckv)
            new_r.append(kr)
        else:
            x, u = _conv_layer(x, n_prompt, seq, t_new, state_conv[j], norm_mix[i], w_in[j], conv_w[j], w_out[j])
            new_u.append(u)
        x = _ffn_layer(x, i, norm_ffn[i], w1b, w2b)

    y_prompt = _final_norm(x, norm_final, n_prompt, 0).reshape(n_batch, seq, d)
    y_sample = _final_norm(x, norm_final, n_sample, n_prompt).reshape(s_batch, t_new, d)

    def split(a, lo, lead, inner):
        return lax.slice_in_dim(a, lo, lo + lead * inner, axis=0).reshape((lead, inner) + a.shape[1:])

    def tails(u, lo, lead, inner, keep):
        return jnp.stack([lax.slice_in_dim(u, lo + (b + 1) * inner - keep, lo + (b + 1) * inner, axis=0)
                          for b in range(lead)])

    keep = conv_w.shape[1] - 1
    ckv_p = jnp.stack([split(c, 0, n_batch, seq) for c in new_c])
    kr_p = jnp.stack([split(r, 0, n_batch, seq) for r in new_r])
    u_p = jnp.stack([tails(u, 0, n_batch, seq, keep) for u in new_u])
    ckv_s = jnp.stack([split(c, n_prompt, s_batch, t_new) for c in new_c])
    kr_s = jnp.stack([split(r, n_prompt, s_batch, t_new) for r in new_r])
    u_s = jnp.stack([tails(u, n_prompt, s_batch, t_new, keep) for u in new_u])
    return (y_prompt, y_sample, ckv_p, kr_p, u_p, ckv_s, kr_s, u_s)
```

```python
import functools
import math

import jax
import jax.numpy as jnp
from jax import lax
from jax.experimental import pallas as pl
from jax.experimental.pallas import tpu as pltpu

CHUNK = 64
ROPE_THETA = 10000.0
EPS = 1e-6
HEAD_PAD = 256
ROPE_PAD = 128
VT_PAD = 144
VMEM_LIMIT = 56 * 1024 * 1024

F32 = jnp.float32
BF16 = jnp.bfloat16


def _params(*sem):
    return pltpu.CompilerParams(dimension_semantics=sem, vmem_limit_bytes=VMEM_LIMIT)


def _tile(total, pref):
    t = min(total, pref)
    assert total % t == 0, (total, pref)
    return t


def _resident(shape):
    nd = len(shape)
    return pl.BlockSpec(shape, lambda *_: (0,) * nd, pipeline_mode=pl.Buffered(1))


def _rms(xf, g):
    return xf * lax.rsqrt(jnp.mean(xf * xf, axis=-1, keepdims=True) + EPS) * g


def _dot(a, b):
    return jnp.dot(a, b, preferred_element_type=F32)


def _dot_nt(a, b):
    return lax.dot_general(a, b, (((1,), (1,)), ((), ())), preferred_element_type=F32)


def _mla_down_kernel(x_ref, g_ref, w_ref, gq_ref, gkv_ref, cos_ref, sin_ref,
                     cq_ref, cqt_ref, ckv_ref, ckvb_ref, ckvt_ref, kr_ref, krb_ref,
                     *, q_lora, kv_lora, rope, q_scale):
    h = _rms(x_ref[...], g_ref[...]).astype(BF16)
    y = _dot(h, w_ref[...])
    cq = _rms(y[:, :q_lora], gq_ref[...]) * q_scale
    cq_ref[...] = cq.astype(BF16)
    cqt_ref[...] = cq.T.astype(BF16)
    ckv = _rms(y[:, q_lora:q_lora + kv_lora], gkv_ref[...])
    ckv_ref[...] = ckv
    ckvb_ref[...] = ckv.astype(BF16)
    ckvt_ref[...] = ckv.T.astype(BF16)
    o = q_lora + kv_lora
    kr = y[:, o:o + ROPE_PAD] * cos_ref[...] + y[:, o + ROPE_PAD:o + 2 * ROPE_PAD] * sin_ref[...]
    kr_ref[...] = kr[:, :rope]
    krb_ref[...] = kr.astype(BF16)


def _mla_q_kernel(cq_ref, wa_ref, wb_ref, cos_ref, sin_ref, q_ref, *, n_heads, nope):
    cq = cq_ref[...]
    cos = cos_ref[...]
    sin = sin_ref[...]
    for hp in range(n_heads // 2):
        a = _dot(cq, wa_ref[:, hp * 2 * HEAD_PAD:(hp + 1) * 2 * HEAD_PAD])
        b = _dot(cq, wb_ref[:, hp * 2 * ROPE_PAD:(hp + 1) * 2 * ROPE_PAD])
        for s in range(2):
            c0 = (2 * hp + s) * HEAD_PAD
            q_ref[:, c0:c0 + nope] = a[:, s * HEAD_PAD:s * HEAD_PAD + nope].astype(BF16)
            ar = a[:, s * HEAD_PAD + nope:(s + 1) * HEAD_PAD]
            br = b[:, s * ROPE_PAD:(s + 1) * ROPE_PAD]
            q_ref[:, c0 + nope:c0 + HEAD_PAD] = (ar * cos + br * sin).astype(BF16)


def _mla_qt_kernel(cqt_ref, wat_ref, wbt_ref, cost_ref, sint_ref, qt_ref, *, n_heads, nope):
    cqt = cqt_ref[...]
    cost = cost_ref[...]
    sint = sint_ref[...]
    for hp in range(n_heads // 2):
        a = _dot(wat_ref[hp * 2 * HEAD_PAD:(hp + 1) * 2 * HEAD_PAD, :], cqt)
        b = _dot(wbt_ref[hp * 2 * ROPE_PAD:(hp + 1) * 2 * ROPE_PAD, :], cqt)
        for s in range(2):
            r0 = (2 * hp + s) * HEAD_PAD
            qt_ref[r0:r0 + nope, :] = a[s * HEAD_PAD:s * HEAD_PAD + nope].astype(BF16)
            ar = a[s * HEAD_PAD + nope:(s + 1) * HEAD_PAD]
            br = b[s * ROPE_PAD:(s + 1) * ROPE_PAD]
            qt_ref[r0 + nope:r0 + HEAD_PAD, :] = (ar * cost + br * sint).astype(BF16)


def _mla_kv_kernel(ckvb_ref, ckvt_ref, krb_ref, wk_ref, wvt_ref, k_ref, vt_ref, *, n_heads, nope):
    c = ckvb_ref[...]
    krb = krb_ref[...]
    tm = c.shape[0]
    vt = _dot(wvt_ref[...], ckvt_ref[...]).astype(BF16)
    ones_row = (lax.broadcasted_iota(jnp.int32, (VT_PAD - nope, tm), 0) == 0).astype(BF16)
    for h in range(n_heads):
        vt_ref[h * VT_PAD:h * VT_PAD + nope, :] = vt[h * nope:(h + 1) * nope]
        vt_ref[h * VT_PAD + nope:(h + 1) * VT_PAD, :] = ones_row
    for hp in range(n_heads // 2):
        kn = _dot(c, wk_ref[:, hp * 2 * nope:(hp + 1) * 2 * nope])
        for s in range(2):
            c0 = (2 * hp + s) * HEAD_PAD
            k_ref[:, c0:c0 + nope] = kn[:, s * nope:(s + 1) * nope].astype(BF16)
            k_ref[:, c0 + nope:c0 + HEAD_PAD] = krb


def _flash_kernel(qt_ref, k_ref, vt_ref, o_ref, s0_ref, *, tq, kb, kd, cw, v_head):
    qi = pl.program_id(2)
    ncol = tq // cw
    main = tuple((off, kb, 0) for off in range(0, tq, kb))
    diag = ((0, kb, 0),) + tuple((off, kd, off // cw) for off in range(kb, tq, kd))

    def scores(base, sub, c):
        off, size, _ = sub
        start = pl.multiple_of(base + off, size)
        return _dot(k_ref[pl.ds(start, size), :], qt_ref[:, c * cw:(c + 1) * cw])

    def update(st, sc, base, sub, c, masked):
        off, size, _ = sub
        start = pl.multiple_of(base + off, size)
        if masked and off + size > c * cw:
            kc = (lax.broadcasted_iota(jnp.int32, (size, cw), 0) + off) // CHUNK
            qc = (lax.broadcasted_iota(jnp.int32, (size, cw), 1) + c * cw) // CHUNK
            sc = jnp.where(kc <= qc, sc, -jnp.inf)
        m_prev, acc_prev = st
        m_new = jnp.maximum(m_prev, jnp.max(sc, axis=0, keepdims=True))
        p = jnp.exp2(sc - m_new).astype(BF16)
        return m_new, jnp.exp2(m_prev - m_new) * acc_prev + _dot(vt_ref[:, pl.ds(start, size)], p)

    def trip(base, state, subs, masked, stage_next):
        state = list(state)
        cur = [None] * ncol
        for t, sub in enumerate(subs):
            nxt = [None] * ncol
            for c in range(sub[2], ncol):
                if t + 1 < len(subs) and c >= subs[t + 1][2]:
                    nxt[c] = scores(base, subs[t + 1], c)
                sc = s0_ref[:, c * cw:(c + 1) * cw] if cur[c] is None else cur[c]
                state[c] = update(state[c], sc, base, sub, c, masked)
                if t + 1 == len(subs) and stage_next:
                    s0_ref[:, c * cw:(c + 1) * cw] = scores(base + tq, main[0], c)
            cur = nxt
        return tuple(state)

    for c in range(ncol):
        s0_ref[:, c * cw:(c + 1) * cw] = scores(0, main[0], c)
    init = tuple((jnp.full((1, cw), -jnp.inf, F32), jnp.zeros((vt_ref.shape[0], cw), F32)) for _ in range(ncol))
    state = lax.fori_loop(0, qi, lambda j, st: trip(j * tq, st, main, False, True), init)
    state = trip(qi * tq, state, diag, True, False)
    for c in range(ncol):
        acc = state[c][1]
        o_ref[c * cw:(c + 1) * cw, :] = (acc[:v_head] / acc[v_head:v_head + 1]).T.astype(BF16)


def _sample_attn_kernel(q_ref, wukt_ref, wuv_ref, cc_ref, cr_ref, nc_ref, nr_ref, o_in_ref, o_ref,
                        ql_ref, qr_ref, m_ref, l_ref, acc_ref, *, n_heads, nope, rope, tk, t_new):
    del o_in_ref
    for h in range(n_heads):
        c0 = h * HEAD_PAD
        ql_ref[h * t_new:(h + 1) * t_new, :] = _dot(q_ref[:, c0:c0 + nope], wukt_ref[h]).astype(BF16)
        qr_ref[h * t_new:(h + 1) * t_new, :] = q_ref[:, c0 + nope:c0 + nope + rope]
    ql = ql_ref[...]
    qr = qr_ref[...]
    m_ref[...] = jnp.full(m_ref.shape, -jnp.inf, F32)
    l_ref[...] = jnp.zeros(l_ref.shape, F32)
    acc_ref[...] = jnp.zeros(acc_ref.shape, F32)

    def update(c, r):
        s = _dot_nt(ql, c) + _dot_nt(qr, r)
        m_prev = m_ref[...]
        m_new = jnp.maximum(m_prev, jnp.max(s, axis=-1, keepdims=True))
        alpha = jnp.exp2(m_prev - m_new)
        p = jnp.exp2(s - m_new)
        l_ref[...] = alpha * l_ref[...] + jnp.sum(p, axis=-1, keepdims=True)
        acc_ref[...] = alpha * acc_ref[...] + _dot(p.astype(BF16), c)
        m_ref[...] = m_new

    def body(j, carry):
        start = pl.multiple_of(j * tk, tk)
        update(cc_ref[pl.ds(start, tk), :].astype(BF16), cr_ref[pl.ds(start, tk), :].astype(BF16))
        return carry

    lax.fori_loop(0, cc_ref.shape[0] // tk, body, 0)
    update(nc_ref[...], nr_ref[:, :rope])
    ol = (acc_ref[...] / l_ref[...]).astype(BF16)
    for h in range(n_heads):
        o_ref[:, h * nope:(h + 1) * nope] = _dot(ol[h * t_new:(h + 1) * t_new, :], wuv_ref[h]).astype(BF16)


def _proj_res_kernel(a_ref, w_ref, x_ref, o_ref):
    o_ref[...] = x_ref[...] + _dot(a_ref[...], w_ref[...])


def _ffn_kernel(x_ref, g_ref, w1_ref, w2_ref, o_ref, h_ref):
    @pl.when(pl.program_id(1) == 0)
    def _():
        x = x_ref[...]
        h_ref[...] = _rms(x, g_ref[...]).astype(BF16)
        o_ref[...] = x

    a = jnp.maximum(_dot(h_ref[...], w1_ref[...]), 0.0)
    o_ref[...] += _dot((a * a).astype(BF16), w2_ref[...])


def _conv_in_kernel(x_ref, g_ref, wb_ref, wc_ref, wv_ref, u_ref, gb_ref, h_ref):
    @pl.when(pl.program_id(1) == 0)
    def _():
        h_ref[...] = _rms(x_ref[...], g_ref[...]).astype(BF16)

    h = h_ref[...]
    gb_ref[...] = _dot(h, wb_ref[...]).astype(BF16)
    u_ref[...] = _dot(h, wc_ref[...]) * _dot(h, wv_ref[...])


def _conv_mix(u, p1, p2, first, second, gb, cw, wout, x):
    u1 = jnp.where(first, p1, pltpu.roll(u, 1, axis=0))
    u2 = jnp.where(first, p2, jnp.where(second, p1, pltpu.roll(u, 2, axis=0)))
    conv = cw[0:1] * u2 + cw[1:2] * u1 + cw[2:3] * u
    return x + _dot((gb.astype(F32) * conv).astype(BF16), wout)


def _conv_out_prompt_kernel(u_ref, halo_ref, gb_ref, x_ref, cw_ref, wout_ref, o_ref):
    tm = u_ref.shape[0]
    has_past = pl.program_id(1) > 0
    halo = halo_ref[...]
    p1 = jnp.where(has_past, halo[7:8], 0.0)
    p2 = jnp.where(has_past, halo[6:7], 0.0)
    row = lax.broadcasted_iota(jnp.int32, (tm, 1), 0)
    o_ref[...] = _conv_mix(u_ref[...], p1, p2, row == 0, row == 1, gb_ref[...], cw_ref[...],
                           wout_ref[...], x_ref[...])


def _conv_out_sample_kernel(u_ref, st_ref, gb_ref, x_ref, cw_ref, wout_ref, xp_ref, o_ref, *, t_new):
    del xp_ref
    tm, d = u_ref.shape
    nb = tm // t_new
    st = st_ref[...]
    p1 = jnp.broadcast_to(st[:, 1:2, :], (nb, t_new, d)).reshape(tm, d)
    p2 = jnp.broadcast_to(st[:, 0:1, :], (nb, t_new, d)).reshape(tm, d)
    t = lax.broadcasted_iota(jnp.int32, (tm, 1), 0) % t_new
    o_ref[...] = _conv_mix(u_ref[...], p1, p2, t == 0, t == 1, gb_ref[...], cw_ref[...],
                           wout_ref[...], x_ref[...])


def _final_norm_kernel(x_ref, g_ref, o_ref):
    o_ref[...] = _rms(x_ref[...], g_ref[...])


def _row_spec(tm, n, off=0):
    return pl.BlockSpec((tm, n), lambda i, *_: (i + off, 0))


def _col_spec(n, tm):
    return pl.BlockSpec((n, tm), lambda i, *_: (0, i))


def _mla_layer(x, j, n_prompt, seq, t_new, tabs, cache_c, cache_r, g_mix, w_dq, g_q, w_uq, w_dkv, g_kv,
               w_uk, w_uv, w_o):
    cos, sin, cost, sint = tabs
    t_all, d = x.shape
    q_lora = w_dq.shape[1]
    kv_lora, n_heads, nope = w_uk.shape
    rope = w_dkv.shape[1] - kv_lora
    half = rope // 2
    q_scale = float((nope + rope) ** -0.5 * math.log2(math.e))
    n_sample = t_all - n_prompt

    wr = w_dkv[:, kv_lora:]
    zr = jnp.zeros((d, ROPE_PAD - rope), F32)
    w_down = jnp.concatenate(
        [w_dq, w_dkv[:, :kv_lora], wr, zr, -wr[:, half:], wr[:, :half], zr], axis=1).astype(BF16)
    wq = w_uq.reshape(q_lora, n_heads, nope + rope)
    wq_r = wq[..., nope:]
    w_qa = jnp.concatenate(
        [wq, jnp.zeros((q_lora, n_heads, HEAD_PAD - nope - rope), F32)], axis=-1
    ).reshape(q_lora, n_heads * HEAD_PAD).astype(BF16)
    w_qb = jnp.concatenate(
        [-wq_r[..., half:], wq_r[..., :half], jnp.zeros((q_lora, n_heads, ROPE_PAD - rope), F32)], axis=-1
    ).reshape(q_lora, n_heads * ROPE_PAD).astype(BF16)
    w_k = w_uk.reshape(kv_lora, n_heads * nope).astype(BF16)
    w_vt = w_uv.reshape(kv_lora, n_heads * nope).T.astype(BF16)
    w_ukt = jnp.transpose(w_uk, (1, 2, 0)).astype(BF16)
    w_uvh = jnp.transpose(w_uv, (1, 0, 2)).astype(BF16)
    w_ob = w_o.astype(BF16)

    tm = _tile(t_all, 512)
    n_down = q_lora + kv_lora + 2 * ROPE_PAD
    cq, cqt, ckv, ckvb, ckvt, kr, krb = pl.pallas_call(
        functools.partial(_mla_down_kernel, q_lora=q_lora, kv_lora=kv_lora, rope=rope, q_scale=q_scale),
        grid=(t_all // tm,),
        in_specs=[_row_spec(tm, d), _resident((1, d)), _resident((d, n_down)), _resident((1, q_lora)),
                  _resident((1, kv_lora)), _row_spec(tm, ROPE_PAD), _row_spec(tm, ROPE_PAD)],
        out_specs=[_row_spec(tm, q_lora), _col_spec(q_lora, tm), _row_spec(tm, kv_lora),
                   _row_spec(tm, kv_lora), _col_spec(kv_lora, tm), _row_spec(tm, rope),
                   _row_spec(tm, ROPE_PAD)],
        out_shape=[jax.ShapeDtypeStruct((t_all, q_lora), BF16), jax.ShapeDtypeStruct((q_lora, t_all), BF16),
                   jax.ShapeDtypeStruct((t_all, kv_lora), F32), jax.ShapeDtypeStruct((t_all, kv_lora), BF16),
                   jax.ShapeDtypeStruct((kv_lora, t_all), BF16), jax.ShapeDtypeStruct((t_all, rope), F32),
                   jax.ShapeDtypeStruct((t_all, ROPE_PAD), BF16)],
        compiler_params=_params("parallel"), name="mla_down",
    )(x, g_mix.reshape(1, d), w_down, g_q.reshape(1, q_lora), g_kv.reshape(1, kv_lora), cos, sin)

    tp = _tile(n_prompt, 512)
    qt = pl.pallas_call(
        functools.partial(_mla_qt_kernel, n_heads=n_heads, nope=nope),
        grid=(n_prompt // tp,),
        in_specs=[_col_spec(q_lora, tp), _resident((n_heads * HEAD_PAD, q_lora)),
                  _resident((n_heads * ROPE_PAD, q_lora)), _col_spec(ROPE_PAD, tp), _col_spec(ROPE_PAD, tp)],
        out_specs=_col_spec(n_heads * HEAD_PAD, tp),
        out_shape=jax.ShapeDtypeStruct((n_heads * HEAD_PAD, n_prompt), BF16),
        compiler_params=_params("parallel"), name="mla_qt",
    )(cqt, w_qa.T, w_qb.T, cost, sint)

    ts = _tile(n_sample, 512)
    off = n_prompt // ts
    q_s = pl.pallas_call(
        functools.partial(_mla_q_kernel, n_heads=n_heads, nope=nope),
        grid=(n_sample // ts,),
        in_specs=[_row_spec(ts, q_lora, off), _resident(w_qa.shape), _resident(w_qb.shape),
                  _row_spec(ts, ROPE_PAD, off), _row_spec(ts, ROPE_PAD, off)],
        out_specs=_row_spec(ts, n_heads * HEAD_PAD),
        out_shape=jax.ShapeDtypeStruct((n_sample, n_heads * HEAD_PAD), BF16),
        compiler_params=_params("parallel"), name="mla_q",
    )(cq, w_qa, w_qb, cos, sin)

    k, vt = pl.pallas_call(
        functools.partial(_mla_kv_kernel, n_heads=n_heads, nope=nope),
        grid=(n_prompt // tp,),
        in_specs=[_row_spec(tp, kv_lora), _col_spec(kv_lora, tp), _row_spec(tp, ROPE_PAD),
                  _resident(w_k.shape), _resident(w_vt.shape)],
        out_specs=[_row_spec(tp, n_heads * HEAD_PAD), _col_spec(n_heads * VT_PAD, tp)],
        out_shape=[jax.ShapeDtypeStruct((n_prompt, n_heads * HEAD_PAD), BF16),
                   jax.ShapeDtypeStruct((n_heads * VT_PAD, n_prompt), BF16)],
        compiler_params=_params("parallel"), name="mla_kv",
    )(ckvb, ckvt, krb, w_k, w_vt)

    blk = _tile(seq, 1024)
    kb = _tile(blk, 256)
    nq = seq // blk
    n_batch = n_prompt // seq
    o = pl.pallas_call(
        functools.partial(_flash_kernel, tq=blk, kb=kb, kd=_tile(kb, 256), cw=_tile(blk, 256), v_head=nope),
        grid=(n_batch, n_heads, nq),
        in_specs=[pl.BlockSpec((HEAD_PAD, blk), lambda b, h, i: (h, b * nq + i)),
                  pl.BlockSpec((seq, HEAD_PAD), lambda b, h, i: (b, h)),
                  pl.BlockSpec((VT_PAD, seq), lambda b, h, i: (h, b))],
        out_specs=pl.BlockSpec((blk, nope), lambda b, h, i: (b * nq + i, h)),
        out_shape=jax.ShapeDtypeStruct((t_all, n_heads * nope), BF16),
        scratch_shapes=[pltpu.VMEM((kb, blk), F32)],
        compiler_params=_params("parallel", "parallel", "arbitrary"), name="mla_flash",
    )(qt, k, vt)

    past = cache_c.shape[2]
    n_sb = n_sample // t_new
    row0 = n_prompt // t_new
    tk = _tile(past, 512)
    o = pl.pallas_call(
        functools.partial(_sample_attn_kernel, n_heads=n_heads, nope=nope, rope=rope, tk=tk, t_new=t_new),
        grid=(n_sb,),
        in_specs=[pl.BlockSpec((t_new, n_heads * HEAD_PAD), lambda b: (b, 0)),
                  _resident(w_ukt.shape), _resident(w_uvh.shape),
                  pl.BlockSpec((None, None, past, kv_lora), lambda b: (j, b, 0, 0)),
                  pl.BlockSpec((None, None, past, rope), lambda b: (j, b, 0, 0)),
                  pl.BlockSpec((t_new, kv_lora), lambda b: (row0 + b, 0)),
                  pl.BlockSpec((t_new, ROPE_PAD), lambda b: (row0 + b, 0)),
                  pl.BlockSpec(memory_space=pl.ANY)],
        out_specs=pl.BlockSpec((t_new, n_heads * nope), lambda b: (row0 + b, 0)),
        out_shape=jax.ShapeDtypeStruct((t_all, n_heads * nope), BF16),
        input_output_aliases={7: 0},
        scratch_shapes=[pltpu.VMEM((n_heads * t_new, kv_lora), BF16), pltpu.VMEM((n_heads * t_new, rope), BF16),
                        pltpu.VMEM((n_heads * t_new, 1), F32), pltpu.VMEM((n_heads * t_new, 1), F32),
                        pltpu.VMEM((n_heads * t_new, kv_lora), F32)],
        compiler_params=_params("parallel"), name="mla_sample_attn",
    )(q_s, w_ukt, w_uvh, cache_c, cache_r, ckvb, krb, o)

    x = pl.pallas_call(
        _proj_res_kernel,
        grid=(t_all // tm,),
        in_specs=[_row_spec(tm, n_heads * nope), _resident(w_ob.shape), _row_spec(tm, d)],
        out_specs=_row_spec(tm, d),
        out_shape=jax.ShapeDtypeStruct((t_all, d), F32),
        compiler_params=_params("parallel"), name="mla_out",
    )(o, w_ob, x)
    return x, ckv, kr


def _conv_layer(x, n_prompt, seq, t_new, state, g_mix, w_in, conv_w, w_out):
    assert conv_w.shape[0] == 3 and state.shape[1] == 2
    t_all, d = x.shape
    w_inb = w_in.astype(BF16)
    w_outb = w_out.astype(BF16)
    tm = _tile(t_all, 512)
    tn = _tile(d, 512)
    nn = d // tn
    u, gb = pl.pallas_call(
        _conv_in_kernel,
        grid=(t_all // tm, nn),
        in_specs=[pl.BlockSpec((tm, d), lambda i, n: (i, 0)), _resident((1, d)),
                  pl.BlockSpec((d, tn), lambda i, n: (0, n)),
                  pl.BlockSpec((d, tn), lambda i, n: (0, nn + n)),
                  pl.BlockSpec((d, tn), lambda i, n: (0, 2 * nn + n))],
        out_specs=[pl.BlockSpec((tm, tn), lambda i, n: (i, n)), pl.BlockSpec((tm, tn), lambda i, n: (i, n))],
        out_shape=[jax.ShapeDtypeStruct((t_all, d), F32), jax.ShapeDtypeStruct((t_all, d), BF16)],
        scratch_shapes=[pltpu.VMEM((tm, d), BF16)],
        compiler_params=_params("parallel", "arbitrary"), name="conv_in",
    )(x, g_mix.reshape(1, d), w_inb, w_inb, w_inb)

    tp = _tile(seq, 512)
    ns = seq // tp
    n_batch = n_prompt // seq
    hb = tp // 8
    x_p = pl.pallas_call(
        _conv_out_prompt_kernel,
        grid=(n_batch, ns),
        in_specs=[pl.BlockSpec((tp, d), lambda b, i: (b * ns + i, 0)),
                  pl.BlockSpec((8, d), lambda b, i: (jnp.maximum((b * ns + i) * hb - 1, 0), 0)),
                  pl.BlockSpec((tp, d), lambda b, i: (b * ns + i, 0)),
                  pl.BlockSpec((tp, d), lambda b, i: (b * ns + i, 0)),
                  _resident((conv_w.shape[0], d)), _resident((d, d))],
        out_specs=pl.BlockSpec((tp, d), lambda b, i: (b * ns + i, 0)),
        out_shape=jax.ShapeDtypeStruct((t_all, d), F32),
        compiler_params=_params("parallel", "arbitrary"), name="conv_out_prompt",
    )(u, u, gb, x, conv_w, w_outb)

    n_sample = t_all - n_prompt
    ts = _tile(n_sample, 512)
    off = n_prompt // ts
    nb = ts // t_new
    x_new = pl.pallas_call(
        functools.partial(_conv_out_sample_kernel, t_new=t_new),
        grid=(n_sample // ts,),
        in_specs=[_row_spec(ts, d, off), pl.BlockSpec((nb, state.shape[1], d), lambda i: (i, 0, 0)),
                  _row_spec(ts, d, off), _row_spec(ts, d, off),
                  _resident((conv_w.shape[0], d)), _resident((d, d)), pl.BlockSpec(memory_space=pl.ANY)],
        out_specs=_row_spec(ts, d, off),
        out_shape=jax.ShapeDtypeStruct((t_all, d), F32),
        input_output_aliases={6: 0},
        compiler_params=_params("parallel"), name="conv_out_sample",
    )(u, state, gb, x, conv_w, w_outb, x_p)
    return x_new, u


def _ffn_layer(x, layer, g, w1b, w2b):
    t_all, d = x.shape
    f = w1b.shape[2]
    tm = _tile(t_all, 512)
    tf = _tile(f, 1024)
    return pl.pallas_call(
        _ffn_kernel,
        grid=(t_all // tm, f // tf),
        in_specs=[pl.BlockSpec((tm, d), lambda i, c: (i, 0)), _resident((1, d)),
                  pl.BlockSpec((None, d, tf), lambda i, c: (layer, 0, c)),
                  pl.BlockSpec((None, tf, d), lambda i, c: (layer, c, 0))],
        out_specs=pl.BlockSpec((tm, d), lambda i, c: (i, 0)),
        out_shape=jax.ShapeDtypeStruct((t_all, d), F32),
        scratch_shapes=[pltpu.VMEM((tm, d), BF16)],
        compiler_params=_params("parallel", "arbitrary"), name="ffn",
    )(x, g.reshape(1, d), w1b, w2b)


def _final_norm(x, g, rows, off_rows):
    d = x.shape[1]
    tm = _tile(rows, 512)
    return pl.pallas_call(
        _final_norm_kernel,
        grid=(rows // tm,),
        in_specs=[_row_spec(tm, d, off_rows // tm), _resident((1, d))],
        out_specs=_row_spec(tm, d),
        out_shape=jax.ShapeDtypeStruct((rows, d), F32),
        compiler_params=_params("parallel"), name="final_norm",
    )(x, g.reshape(1, d))


def kernel(x_prompt, x_sample, cache_ckv, cache_krope, state_conv, norm_mix, w_dq, g_q, w_uq, w_dkv, g_kv,
           w_uk, w_uv, w_o, w_in, conv_w, w_out, norm_ffn, w_ff1, w_ff2, norm_final):
    n_batch, seq, d = x_prompt.shape
    s_batch, t_new, _ = x_sample.shape
    past = cache_ckv.shape[2]
    depth = norm_mix.shape[0]
    kv_lora = w_uk.shape[1]
    rope = w_dkv.shape[2] - kv_lora
    half = rope // 2
    n_prompt = n_batch * seq
    n_sample = s_batch * t_new

    x = jnp.concatenate([x_prompt.reshape(n_prompt, d), x_sample.reshape(n_sample, d)], axis=0)

    pos = jnp.concatenate([jnp.tile(jnp.arange(seq, dtype=jnp.int32), n_batch),
                           jnp.tile(past + jnp.arange(t_new, dtype=jnp.int32), s_batch)])
    inv = ROPE_THETA ** (-jnp.arange(0, half, dtype=F32) / half)
    ang = pos.astype(F32)[:, None] * inv[None, :]
    zpad = jnp.zeros((pos.shape[0], ROPE_PAD - rope), F32)
    cos = jnp.concatenate([jnp.cos(ang), jnp.cos(ang), zpad], axis=1)
    sin = jnp.concatenate([jnp.sin(ang), jnp.sin(ang), zpad], axis=1)
    tabs = (cos, sin, cos[:n_prompt].T, sin[:n_prompt].T)

    w1b = w_ff1.astype(BF16)
    w2b = w_ff2.astype(BF16)

    new_c, new_r, new_u = [], [], []
    for i in range(depth):
        j = i // 2
        if i % 2 == 0:
            x, ckv, kr = _mla_layer(x, j, n_prompt, seq, t_new, tabs, cache_ckv, cache_krope,
                                    norm_mix[i], w_dq[j], g_q[j], w_uq[j], w_dkv[j], g_kv[j],
                                    w_uk[j], w_uv[j], w_o[j])
            new_c.append(ckv)
            new_r.append(kr)
        else:
            x, u = _conv_layer(x, n_prompt, seq, t_new, state_conv[j], norm_mix[i], w_in[j], conv_w[j], w_out[j])
            new_u.append(u)
        x = _ffn_layer(x, i, norm_ffn[i], w1b, w2b)

    y_prompt = _final_norm(x, norm_final, n_prompt, 0).reshape(n_batch, seq, d)
    y_sample = _final_norm(x, norm_final, n_sample, n_prompt).reshape(s_batch, t_new, d)

    def split(a, lo, lead, inner):
        return lax.slice_in_dim(a, lo, lo + lead * inner, axis=0).reshape((lead, inner) + a.shape[1:])

    def tails(u, lo, lead, inner, keep):
        return jnp.stack([lax.slice_in_dim(u, lo + (b + 1) * inner - keep, lo + (b + 1) * inner, axis=0)
                          for b in range(lead)])

    keep = conv_w.shape[1] - 1
    ckv_p = jnp.stack([split(c, 0, n_batch, seq) for c in new_c])
    kr_p = jnp.stack([split(r, 0, n_batch, seq) for r in new_r])
    u_p = jnp.stack([tails(u, 0, n_batch, seq, keep) for u in new_u])
    ckv_s = jnp.stack([split(c, n_prompt, s_batch, t_new) for c in new_c])
    kr_s = jnp.stack([split(r, n_prompt, s_batch, t_new) for r in new_r])
    u_s = jnp.stack([tails(u, n_prompt, s_batch, t_new, keep) for u in new_u])
    return (y_prompt, y_sample, ckv_p, kr_p, u_p, ckv_s, kr_s, u_s)
```

```python
import functools
import math

import jax
import jax.numpy as jnp
from jax import lax
from jax.experimental import pallas as pl
from jax.experimental.pallas import tpu as pltpu

CHUNK = 64
ROPE_THETA = 10000.0
EPS = 1e-6
HEAD_PAD = 256
ROPE_PAD = 128
VT_PAD = 144
VMEM_LIMIT = 56 * 1024 * 1024

F32 = jnp.float32
BF16 = jnp.bfloat16


def _params(*sem):
    return pltpu.CompilerParams(dimension_semantics=sem, vmem_limit_bytes=VMEM_LIMIT)


def _tile(total, pref):
    t = min(total, pref)
    assert total % t == 0, (total, pref)
    return t


def _resident(shape):
    nd = len(shape)
    return pl.BlockSpec(shape, lambda *_: (0,) * nd, pipeline_mode=pl.Buffered(1))


def _rms(xf, g):
    return xf * lax.rsqrt(jnp.mean(xf * xf, axis=-1, keepdims=True) + EPS) * g


def _dot(a, b):
    return jnp.dot(a, b, preferred_element_type=F32)


def _dot_nt(a, b):
    return lax.dot_general(a, b, (((1,), (1,)), ((), ())), preferred_element_type=F32)


def _mla_down_kernel(x_ref, g_ref, w_ref, gq_ref, gkv_ref, cos_ref, sin_ref,
                     cq_ref, cqt_ref, ckv_ref, ckvb_ref, ckvt_ref, kr_ref, krb_ref,
                     *, q_lora, kv_lora, rope, q_scale):
    h = _rms(x_ref[...], g_ref[...]).astype(BF16)
    y = _dot(h, w_ref[...])
    cq = _rms(y[:, :q_lora], gq_ref[...]) * q_scale
    cq_ref[...] = cq.astype(BF16)
    cqt_ref[...] = cq.T.astype(BF16)
    ckv = _rms(y[:, q_lora:q_lora + kv_lora], gkv_ref[...])
    ckv_ref[...] = ckv
    ckvb_ref[...] = ckv.astype(BF16)
    ckvt_ref[...] = ckv.T.astype(BF16)
    o = q_lora + kv_lora
    kr = y[:, o:o + ROPE_PAD] * cos_ref[...] + y[:, o + ROPE_PAD:o + 2 * ROPE_PAD] * sin_ref[...]
    kr_ref[...] = kr[:, :rope]
    krb_ref[...] = kr.astype(BF16)


def _mla_q_kernel(cq_ref, wa_ref, wb_ref, cos_ref, sin_ref, q_ref, *, n_heads, nope):
    cq = cq_ref[...]
    cos = cos_ref[...]
    sin = sin_ref[...]
    for hp in range(n_heads // 2):
        a = _dot(cq, wa_ref[:, hp * 2 * HEAD_PAD:(hp + 1) * 2 * HEAD_PAD])
        b = _dot(cq, wb_ref[:, hp * 2 * ROPE_PAD:(hp + 1) * 2 * ROPE_PAD])
        for s in range(2):
            c0 = (2 * hp + s) * HEAD_PAD
            q_ref[:, c0:c0 + nope] = a[:, s * HEAD_PAD:s * HEAD_PAD + nope].astype(BF16)
            ar = a[:, s * HEAD_PAD + nope:(s + 1) * HEAD_PAD]
            br = b[:, s * ROPE_PAD:(s + 1) * ROPE_PAD]
            q_ref[:, c0 + nope:c0 + HEAD_PAD] = (ar * cos + br * sin).astype(BF16)


def _mla_qt_kernel(cqt_ref, wat_ref, wbt_ref, cost_ref, sint_ref, qt_ref, *, n_heads, nope):
    cqt = cqt_ref[...]
    cost = cost_ref[...]
    sint = sint_ref[...]
    for hp in range(n_heads // 2):
        a = _dot(wat_ref[hp * 2 * HEAD_PAD:(hp + 1) * 2 * HEAD_PAD, :], cqt)
        b = _dot(wbt_ref[hp * 2 * ROPE_PAD:(hp + 1) * 2 * ROPE_PAD, :], cqt)
        for s in range(2):
            r0 = (2 * hp + s) * HEAD_PAD
            qt_ref[r0:r0 + nope, :] = a[s * HEAD_PAD:s * HEAD_PAD + nope].astype(BF16)
            ar = a[s * HEAD_PAD + nope:(s + 1) * HEAD_PAD]
            br = b[s * ROPE_PAD:(s + 1) * ROPE_PAD]
            qt_ref[r0 + nope:r0 + HEAD_PAD, :] = (ar * cost + br * sint).astype(BF16)


def _mla_kv_kernel(ckvb_ref, ckvt_ref, krb_ref, wk_ref, wvt_ref, k_ref, vt_ref, *, n_heads, nope):
    c = ckvb_ref[...]
    krb = krb_ref[...]
    tm = c.shape[0]
    vt = _dot(wvt_ref[...], ckvt_ref[...]).astype(BF16)
    ones_row = (lax.broadcasted_iota(jnp.int32, (VT_PAD - nope, tm), 0) == 0).astype(BF16)
    for h in range(n_heads):
        vt_ref[h * VT_PAD:h * VT_PAD + nope, :] = vt[h * nope:(h + 1) * nope]
        vt_ref[h * VT_PAD + nope:(h + 1) * VT_PAD, :] = ones_row
    for hp in range(n_heads // 2):
        kn = _dot(c, wk_ref[:, hp * 2 * nope:(hp + 1) * 2 * nope])
        for s in range(2):
            c0 = (2 * hp + s) * HEAD_PAD
            k_ref[:, c0:c0 + nope] = kn[:, s * nope:(s + 1) * nope].astype(BF16)
            k_ref[:, c0 + nope:c0 + HEAD_PAD] = krb


def _flash_kernel(qt_ref, k_ref, vt_ref, o_ref, s0_ref, *, tq, kb, kd, cw, v_head):
    qi = pl.program_id(2)
    ncol = tq // cw
    main = tuple((off, kb, 0) for off in range(0, tq, kb))
    diag = ((0, kb, 0),) + tuple((off, kd, off // cw) for off in range(kb, tq, kd))

    def scores(base, sub, c):
        off, size, _ = sub
        start = pl.multiple_of(base + off, size)
        return _dot(k_ref[pl.ds(start, size), :], qt_ref[:, c * cw:(c + 1) * cw])

    def update(st, sc, base, sub, c, masked):
        off, size, _ = sub
        start = pl.multiple_of(base + off, size)
        if masked and off + size > c * cw:
            kc = (lax.broadcasted_iota(jnp.int32, (size, cw), 0) + off) // CHUNK
            qc = (lax.broadcasted_iota(jnp.int32, (size, cw), 1) + c * cw) // CHUNK
            sc = jnp.where(kc <= qc, sc, -jnp.inf)
        m_prev, acc_prev = st
        m_new = jnp.maximum(m_prev, jnp.max(sc, axis=0, keepdims=True))
        p = jnp.exp2(sc - m_new).astype(BF16)
        return m_new, jnp.exp2(m_prev - m_new) * acc_prev + _dot(vt_ref[:, pl.ds(start, size)], p)

    def trip(base, state, subs, masked, stage_next):
        state = list(state)
        cur = [None] * ncol
        for t, sub in enumerate(subs):
            nxt = [None] * ncol
            for c in range(sub[2], ncol):
                if t + 1 < len(subs) and c >= subs[t + 1][2]:
                    nxt[c] = scores(base, subs[t + 1], c)
                sc = s0_ref[:, c * cw:(c + 1) * cw] if cur[c] is None else cur[c]
                state[c] = update(state[c], sc, base, sub, c, masked)
                if t + 1 == len(subs) and stage_next:
                    s0_ref[:, c * cw:(c + 1) * cw] = scores(base + tq, main[0], c)
            cur = nxt
        return tuple(state)

    for c in range(ncol):
        s0_ref[:, c * cw:(c + 1) * cw] = scores(0, main[0], c)
    init = tuple((jnp.full((1, cw), -jnp.inf, F32), jnp.zeros((vt_ref.shape[0], cw), F32)) for _ in range(ncol))
    state = lax.fori_loop(0, qi, lambda j, st: trip(j * tq, st, main, False, True), init)
    state = trip(qi * tq, state, diag, True, False)
    for c in range(ncol):
        acc = state[c][1]
        o_ref[c * cw:(c + 1) * cw, :] = (acc[:v_head] / acc[v_head:v_head + 1]).T.astype(BF16)


def _lane_fold(a, op):
    out = a[:, :128]
    for i in range(1, a.shape[1] // 128):
        out = op(out, a[:, i * 128:(i + 1) * 128])
    return out


def _sample_attn_kernel(q_ref, wukt_ref, wuv_ref, cc_ref, crt_ref, nc_ref, nr_ref, o_in_ref, o_ref,
                        ql_ref, qr_ref, cb_ref, s_ref, p_ref, *, n_heads, nope, rope, tk, t_new):
    del o_in_ref
    past = cc_ref.shape[0]
    for h in range(n_heads):
        c0 = h * HEAD_PAD
        ql_ref[h * t_new:(h + 1) * t_new, :] = _dot(q_ref[:, c0:c0 + nope], wukt_ref[h]).astype(BF16)
        qr_ref[h * t_new:(h + 1) * t_new, :] = q_ref[:, c0 + nope:c0 + nope + rope]
    ql = ql_ref[...]
    qr = qr_ref[...]
    nc = nc_ref[...]
    s_new = _dot_nt(ql, nc) + _dot_nt(qr, nr_ref[:, :rope])
    rows = ql.shape[0]
    nchunk = past // tk
    half = nchunk // 2
    ks = [slice(j * tk, (j + 1) * tk) for j in range(nchunk)]

    def score(j, m_lane):
        cb = cc_ref[ks[j], :].astype(BF16)
        cb_ref[ks[j], :] = cb
        s = _dot_nt(ql, cb) + _dot(qr, crt_ref[:, ks[j]].astype(BF16))
        s_ref[:, ks[j]] = s
        return jnp.maximum(m_lane, _lane_fold(s, jnp.maximum))

    def probs(j, m, l_lane):
        p = jnp.exp2(s_ref[:, ks[j]] - m)
        p_ref[:, ks[j]] = p.astype(BF16)
        return l_lane + _lane_fold(p, jnp.add)

    neg = jnp.full((rows, 128), -jnp.inf, F32)
    zero = jnp.zeros((rows, 128), F32)
    m_lane = neg
    for j in range(half):
        m_lane = score(j, m_lane)
    m_a = jnp.max(m_lane, axis=-1, keepdims=True)
    m_lane, l_a = neg, zero
    for i, j in enumerate(range(half, nchunk)):
        m_lane = score(j, m_lane)
        if i < half:
            l_a = probs(i, m_a, l_a)
    m_b = jnp.maximum(jnp.max(m_lane, axis=-1, keepdims=True), jnp.max(s_new, axis=-1, keepdims=True))
    m = jnp.maximum(m_a, m_b)
    acc_a, l_b = jnp.zeros((rows, cb_ref.shape[1]), F32), zero
    for i, j in enumerate(range(half, nchunk)):
        l_b = probs(j, m, l_b)
        if i < half:
            acc_a = acc_a + _dot(p_ref[:, ks[i]], cb_ref[ks[i], :])
    p_new = jnp.exp2(s_new - m)
    acc_b = _dot(p_new.astype(BF16), nc)
    for j in range(half, nchunk):
        acc_b = acc_b + _dot(p_ref[:, ks[j]], cb_ref[ks[j], :])
    alpha = jnp.exp2(m_a - m)
    l = (alpha * jnp.sum(l_a, axis=-1, keepdims=True) + jnp.sum(l_b, axis=-1, keepdims=True)
         + jnp.sum(p_new, axis=-1, keepdims=True))
    ol = ((alpha * acc_a + acc_b) / l).astype(BF16)
    for h in range(n_heads):
        o_ref[:, h * nope:(h + 1) * nope] = _dot(ol[h * t_new:(h + 1) * t_new, :], wuv_ref[h]).astype(BF16)


def _proj_res_kernel(a_ref, w_ref, x_ref, o_ref):
    o_ref[...] = x_ref[...] + _dot(a_ref[...], w_ref[...])


def _ffn_kernel(x_ref, g_ref, w1_ref, w2_ref, o_ref, h_ref):
    @pl.when(pl.program_id(1) == 0)
    def _():
        x = x_ref[...]
        h_ref[...] = _rms(x, g_ref[...]).astype(BF16)
        o_ref[...] = x

    a = jnp.maximum(_dot(h_ref[...], w1_ref[...]), 0.0)
    o_ref[...] += _dot((a * a).astype(BF16), w2_ref[...])


def _conv_in_kernel(x_ref, g_ref, wb_ref, wc_ref, wv_ref, u_ref, gb_ref, h_ref):
    @pl.when(pl.program_id(1) == 0)
    def _():
        h_ref[...] = _rms(x_ref[...], g_ref[...]).astype(BF16)

    h = h_ref[...]
    gb_ref[...] = _dot(h, wb_ref[...]).astype(BF16)
    u_ref[...] = _dot(h, wc_ref[...]) * _dot(h, wv_ref[...])


def _conv_mix(u, p1, p2, first, second, gb, cw, wout, x):
    u1 = jnp.where(first, p1, pltpu.roll(u, 1, axis=0))
    u2 = jnp.where(first, p2, jnp.where(second, p1, pltpu.roll(u, 2, axis=0)))
    conv = cw[0:1] * u2 + cw[1:2] * u1 + cw[2:3] * u
    return x + _dot((gb.astype(F32) * conv).astype(BF16), wout)


def _conv_out_prompt_kernel(u_ref, halo_ref, gb_ref, x_ref, cw_ref, wout_ref, o_ref):
    tm = u_ref.shape[0]
    has_past = pl.program_id(1) > 0
    halo = halo_ref[...]
    p1 = jnp.where(has_past, halo[7:8], 0.0)
    p2 = jnp.where(has_past, halo[6:7], 0.0)
    row = lax.broadcasted_iota(jnp.int32, (tm, 1), 0)
    o_ref[...] = _conv_mix(u_ref[...], p1, p2, row == 0, row == 1, gb_ref[...], cw_ref[...],
                           wout_ref[...], x_ref[...])


def _conv_out_sample_kernel(u_ref, st_ref, gb_ref, x_ref, cw_ref, wout_ref, xp_ref, o_ref, *, t_new):
    del xp_ref
    tm, d = u_ref.shape
    nb = tm // t_new
    st = st_ref[...]
    p1 = jnp.broadcast_to(st[:, 1:2, :], (nb, t_new, d)).reshape(tm, d)
    p2 = jnp.broadcast_to(st[:, 0:1, :], (nb, t_new, d)).reshape(tm, d)
    t = lax.broadcasted_iota(jnp.int32, (tm, 1), 0) % t_new
    o_ref[...] = _conv_mix(u_ref[...], p1, p2, t == 0, t == 1, gb_ref[...], cw_ref[...],
                           wout_ref[...], x_ref[...])


def _final_norm_kernel(x_ref, g_ref, o_ref):
    o_ref[...] = _rms(x_ref[...], g_ref[...])


def _row_spec(tm, n, off=0):
    return pl.BlockSpec((tm, n), lambda i, *_: (i + off, 0))


def _col_spec(n, tm):
    return pl.BlockSpec((n, tm), lambda i, *_: (0, i))


def _mla_layer(x, j, n_prompt, seq, t_new, tabs, cache_c, cache_rt, g_mix, w_dq, g_q, w_uq, w_dkv, g_kv,
               w_uk, w_uv, w_o):
    cos, sin, cost, sint = tabs
    t_all, d = x.shape
    q_lora = w_dq.shape[1]
    kv_lora, n_heads, nope = w_uk.shape
    rope = w_dkv.shape[1] - kv_lora
    half = rope // 2
    q_scale = float((nope + rope) ** -0.5 * math.log2(math.e))
    n_sample = t_all - n_prompt

    wr = w_dkv[:, kv_lora:]
    zr = jnp.zeros((d, ROPE_PAD - rope), F32)
    w_down = jnp.concatenate(
        [w_dq, w_dkv[:, :kv_lora], wr, zr, -wr[:, half:], wr[:, :half], zr], axis=1).astype(BF16)
    wq = w_uq.reshape(q_lora, n_heads, nope + rope)
    wq_r = wq[..., nope:]
    w_qa = jnp.concatenate(
        [wq, jnp.zeros((q_lora, n_heads, HEAD_PAD - nope - rope), F32)], axis=-1
    ).reshape(q_lora, n_heads * HEAD_PAD).astype(BF16)
    w_qb = jnp.concatenate(
        [-wq_r[..., half:], wq_r[..., :half], jnp.zeros((q_lora, n_heads, ROPE_PAD - rope), F32)], axis=-1
    ).reshape(q_lora, n_heads * ROPE_PAD).astype(BF16)
    w_k = w_uk.reshape(kv_lora, n_heads * nope).astype(BF16)
    w_vt = w_uv.reshape(kv_lora, n_heads * nope).T.astype(BF16)
    w_ukt = jnp.transpose(w_uk, (1, 2, 0)).astype(BF16)
    w_uvh = jnp.transpose(w_uv, (1, 0, 2)).astype(BF16)
    w_ob = w_o.astype(BF16)

    tm = _tile(t_all, 512)
    n_down = q_lora + kv_lora + 2 * ROPE_PAD
    cq, cqt, ckv, ckvb, ckvt, kr, krb = pl.pallas_call(
        functools.partial(_mla_down_kernel, q_lora=q_lora, kv_lora=kv_lora, rope=rope, q_scale=q_scale),
        grid=(t_all // tm,),
        in_specs=[_row_spec(tm, d), _resident((1, d)), _resident((d, n_down)), _resident((1, q_lora)),
                  _resident((1, kv_lora)), _row_spec(tm, ROPE_PAD), _row_spec(tm, ROPE_PAD)],
        out_specs=[_row_spec(tm, q_lora), _col_spec(q_lora, tm), _row_spec(tm, kv_lora),
                   _row_spec(tm, kv_lora), _col_spec(kv_lora, tm), _row_spec(tm, rope),
                   _row_spec(tm, ROPE_PAD)],
        out_shape=[jax.ShapeDtypeStruct((t_all, q_lora), BF16), jax.ShapeDtypeStruct((q_lora, t_all), BF16),
                   jax.ShapeDtypeStruct((t_all, kv_lora), F32), jax.ShapeDtypeStruct((t_all, kv_lora), BF16),
                   jax.ShapeDtypeStruct((kv_lora, t_all), BF16), jax.ShapeDtypeStruct((t_all, rope), F32),
                   jax.ShapeDtypeStruct((t_all, ROPE_PAD), BF16)],
        compiler_params=_params("parallel"), name="mla_down",
    )(x, g_mix.reshape(1, d), w_down, g_q.reshape(1, q_lora), g_kv.reshape(1, kv_lora), cos, sin)

    tp = _tile(n_prompt, 512)
    qt = pl.pallas_call(
        functools.partial(_mla_qt_kernel, n_heads=n_heads, nope=nope),
        grid=(n_prompt // tp,),
        in_specs=[_col_spec(q_lora, tp), _resident((n_heads * HEAD_PAD, q_lora)),
                  _resident((n_heads * ROPE_PAD, q_lora)), _col_spec(ROPE_PAD, tp), _col_spec(ROPE_PAD, tp)],
        out_specs=_col_spec(n_heads * HEAD_PAD, tp),
        out_shape=jax.ShapeDtypeStruct((n_heads * HEAD_PAD, n_prompt), BF16),
        compiler_params=_params("parallel"), name="mla_qt",
    )(cqt, w_qa.T, w_qb.T, cost, sint)

    ts = _tile(n_sample, 512)
    off = n_prompt // ts
    q_s = pl.pallas_call(
        functools.partial(_mla_q_kernel, n_heads=n_heads, nope=nope),
        grid=(n_sample // ts,),
        in_specs=[_row_spec(ts, q_lora, off), _resident(w_qa.shape), _resident(w_qb.shape),
                  _row_spec(ts, ROPE_PAD, off), _row_spec(ts, ROPE_PAD, off)],
        out_specs=_row_spec(ts, n_heads * HEAD_PAD),
        out_shape=jax.ShapeDtypeStruct((n_sample, n_heads * HEAD_PAD), BF16),
        compiler_params=_params("parallel"), name="mla_q",
    )(cq, w_qa, w_qb, cos, sin)

    k, vt = pl.pallas_call(
        functools.partial(_mla_kv_kernel, n_heads=n_heads, nope=nope),
        grid=(n_prompt // tp,),
        in_specs=[_row_spec(tp, kv_lora), _col_spec(kv_lora, tp), _row_spec(tp, ROPE_PAD),
                  _resident(w_k.shape), _resident(w_vt.shape)],
        out_specs=[_row_spec(tp, n_heads * HEAD_PAD), _col_spec(n_heads * VT_PAD, tp)],
        out_shape=[jax.ShapeDtypeStruct((n_prompt, n_heads * HEAD_PAD), BF16),
                   jax.ShapeDtypeStruct((n_heads * VT_PAD, n_prompt), BF16)],
        compiler_params=_params("parallel"), name="mla_kv",
    )(ckvb, ckvt, krb, w_k, w_vt)

    blk = _tile(seq, 1024)
    kb = _tile(blk, 256)
    nq = seq // blk
    n_batch = n_prompt // seq
    o = pl.pallas_call(
        functools.partial(_flash_kernel, tq=blk, kb=kb, kd=_tile(kb, 256), cw=_tile(blk, 256), v_head=nope),
        grid=(n_batch, n_heads, nq),
        in_specs=[pl.BlockSpec((HEAD_PAD, blk), lambda b, h, i: (h, b * nq + i)),
                  pl.BlockSpec((seq, HEAD_PAD), lambda b, h, i: (b, h)),
                  pl.BlockSpec((VT_PAD, seq), lambda b, h, i: (h, b))],
        out_specs=pl.BlockSpec((blk, nope), lambda b, h, i: (b * nq + i, h)),
        out_shape=jax.ShapeDtypeStruct((t_all, n_heads * nope), BF16),
        scratch_shapes=[pltpu.VMEM((kb, blk), F32)],
        compiler_params=_params("parallel", "parallel", "arbitrary"), name="mla_flash",
    )(qt, k, vt)

    past = cache_c.shape[2]
    n_sb = n_sample // t_new
    row0 = n_prompt // t_new
    tk = _tile(past, 512)
    o = pl.pallas_call(
        functools.partial(_sample_attn_kernel, n_heads=n_heads, nope=nope, rope=rope, tk=tk, t_new=t_new),
        grid=(n_sb,),
        in_specs=[pl.BlockSpec((t_new, n_heads * HEAD_PAD), lambda b: (b, 0)),
                  _resident(w_ukt.shape), _resident(w_uvh.shape),
                  pl.BlockSpec((None, None, past, kv_lora), lambda b: (j, b, 0, 0)),
                  pl.BlockSpec((None, None, rope, past), lambda b: (j, b, 0, 0)),
                  pl.BlockSpec((t_new, kv_lora), lambda b: (row0 + b, 0)),
                  pl.BlockSpec((t_new, ROPE_PAD), lambda b: (row0 + b, 0)),
                  pl.BlockSpec(memory_space=pl.ANY)],
        out_specs=pl.BlockSpec((t_new, n_heads * nope), lambda b: (row0 + b, 0)),
        out_shape=jax.ShapeDtypeStruct((t_all, n_heads * nope), BF16),
        input_output_aliases={7: 0},
        scratch_shapes=[pltpu.VMEM((n_heads * t_new, kv_lora), BF16), pltpu.VMEM((n_heads * t_new, rope), BF16),
                        pltpu.VMEM((past, kv_lora), BF16), pltpu.VMEM((n_heads * t_new, past), F32),
                        pltpu.VMEM((n_heads * t_new, past), BF16)],
        compiler_params=_params("parallel"), name="mla_sample_attn",
    )(q_s, w_ukt, w_uvh, cache_c, cache_rt, ckvb, krb, o)

    x = pl.pallas_call(
        _proj_res_kernel,
        grid=(t_all // tm,),
        in_specs=[_row_spec(tm, n_heads * nope), _resident(w_ob.shape), _row_spec(tm, d)],
        out_specs=_row_spec(tm, d),
        out_shape=jax.ShapeDtypeStruct((t_all, d), F32),
        compiler_params=_params("parallel"), name="mla_out",
    )(o, w_ob, x)
    return x, ckv, kr


def _conv_layer(x, n_prompt, seq, t_new, state, g_mix, w_in, conv_w, w_out):
    assert conv_w.shape[0] == 3 and state.shape[1] == 2
    t_all, d = x.shape
    w_inb = w_in.astype(BF16)
    w_outb = w_out.astype(BF16)
    tm = _tile(t_all, 512)
    tn = _tile(d, 1024)
    nn = d // tn
    u, gb = pl.pallas_call(
        _conv_in_kernel,
        grid=(t_all // tm, nn),
        in_specs=[pl.BlockSpec((tm, d), lambda i, n: (i, 0)), _resident((1, d)),
                  pl.BlockSpec((d, tn), lambda i, n: (0, n)),
                  pl.BlockSpec((d, tn), lambda i, n: (0, nn + n)),
                  pl.BlockSpec((d, tn), lambda i, n: (0, 2 * nn + n))],
        out_specs=[pl.BlockSpec((tm, tn), lambda i, n: (i, n)), pl.BlockSpec((tm, tn), lambda i, n: (i, n))],
        out_shape=[jax.ShapeDtypeStruct((t_all, d), F32), jax.ShapeDtypeStruct((t_all, d), BF16)],
        scratch_shapes=[pltpu.VMEM((tm, d), BF16)],
        compiler_params=_params("parallel", "arbitrary"), name="conv_in",
    )(x, g_mix.reshape(1, d), w_inb, w_inb, w_inb)

    tp = _tile(seq, 512)
    ns = seq // tp
    n_batch = n_prompt // seq
    hb = tp // 8
    x_p = pl.pallas_call(
        _conv_out_prompt_kernel,
        grid=(n_batch, ns),
        in_specs=[pl.BlockSpec((tp, d), lambda b, i: (b * ns + i, 0)),
                  pl.BlockSpec((8, d), lambda b, i: (jnp.maximum((b * ns + i) * hb - 1, 0), 0)),
                  pl.BlockSpec((tp, d), lambda b, i: (b * ns + i, 0)),
                  pl.BlockSpec((tp, d), lambda b, i: (b * ns + i, 0)),
                  _resident((conv_w.shape[0], d)), _resident((d, d))],
        out_specs=pl.BlockSpec((tp, d), lambda b, i: (b * ns + i, 0)),
        out_shape=jax.ShapeDtypeStruct((t_all, d), F32),
        compiler_params=_params("parallel", "arbitrary"), name="conv_out_prompt",
    )(u, u, gb, x, conv_w, w_outb)

    n_sample = t_all - n_prompt
    ts = _tile(n_sample, 512)
    off = n_prompt // ts
    nb = ts // t_new
    x_new = pl.pallas_call(
        functools.partial(_conv_out_sample_kernel, t_new=t_new),
        grid=(n_sample // ts,),
        in_specs=[_row_spec(ts, d, off), pl.BlockSpec((nb, state.shape[1], d), lambda i: (i, 0, 0)),
                  _row_spec(ts, d, off), _row_spec(ts, d, off),
                  _resident((conv_w.shape[0], d)), _resident((d, d)), pl.BlockSpec(memory_space=pl.ANY)],
        out_specs=_row_spec(ts, d, off),
        out_shape=jax.ShapeDtypeStruct((t_all, d), F32),
        input_output_aliases={6: 0},
        compiler_params=_params("parallel"), name="conv_out_sample",
    )(u, state, gb, x, conv_w, w_outb, x_p)
    return x_new, u


def _ffn_layer(x, layer, g, w1b, w2b):
    t_all, d = x.shape
    f = w1b.shape[2]
    tm = _tile(t_all, 512)
    tf = _tile(f, 1024)
    return pl.pallas_call(
        _ffn_kernel,
        grid=(t_all // tm, f // tf),
        in_specs=[pl.BlockSpec((tm, d), lambda i, c: (i, 0)), _resident((1, d)),
                  pl.BlockSpec((None, d, tf), lambda i, c: (layer, 0, c)),
                  pl.BlockSpec((None, tf, d), lambda i, c: (layer, c, 0))],
        out_specs=pl.BlockSpec((tm, d), lambda i, c: (i, 0)),
        out_shape=jax.ShapeDtypeStruct((t_all, d), F32),
        scratch_shapes=[pltpu.VMEM((tm, d), BF16)],
        compiler_params=_params("parallel", "arbitrary"), name="ffn",
    )(x, g.reshape(1, d), w1b, w2b)


def _final_norm(x, g, rows, off_rows):
    d = x.shape[1]
    tm = _tile(rows, 512)
    return pl.pallas_call(
        _final_norm_kernel,
        grid=(rows // tm,),
        in_specs=[_row_spec(tm, d, off_rows // tm), _resident((1, d))],
        out_specs=_row_spec(tm, d),
        out_shape=jax.ShapeDtypeStruct((rows, d), F32),
        compiler_params=_params("parallel"), name="final_norm",
    )(x, g.reshape(1, d))


def kernel(x_prompt, x_sample, cache_ckv, cache_krope, state_conv, norm_mix, w_dq, g_q, w_uq, w_dkv, g_kv,
           w_uk, w_uv, w_o, w_in, conv_w, w_out, norm_ffn, w_ff1, w_ff2, norm_final):
    n_batch, seq, d = x_prompt.shape
    s_batch, t_new, _ = x_sample.shape
    past = cache_ckv.shape[2]
    depth = norm_mix.shape[0]
    kv_lora = w_uk.shape[1]
    rope = w_dkv.shape[2] - kv_lora
    half = rope // 2
    n_prompt = n_batch * seq
    n_sample = s_batch * t_new

    x = jnp.concatenate([x_prompt.reshape(n_prompt, d), x_sample.reshape(n_sample, d)], axis=0)

    pos = jnp.concatenate([jnp.tile(jnp.arange(seq, dtype=jnp.int32), n_batch),
                           jnp.tile(past + jnp.arange(t_new, dtype=jnp.int32), s_batch)])
    inv = ROPE_THETA ** (-jnp.arange(0, half, dtype=F32) / half)
    ang = pos.astype(F32)[:, None] * inv[None, :]
    zpad = jnp.zeros((pos.shape[0], ROPE_PAD - rope), F32)
    cos = jnp.concatenate([jnp.cos(ang), jnp.cos(ang), zpad], axis=1)
    sin = jnp.concatenate([jnp.sin(ang), jnp.sin(ang), zpad], axis=1)
    tabs = (cos, sin, cos[:n_prompt].T, sin[:n_prompt].T)

    cache_rt = jnp.swapaxes(cache_krope, 2, 3)
    w1b = w_ff1.astype(BF16)
    w2b = w_ff2.astype(BF16)

    new_c, new_r, new_u = [], [], []
    for i in range(depth):
        j = i // 2
        if i % 2 == 0:
            x, ckv, kr = _mla_layer(x, j, n_prompt, seq, t_new, tabs, cache_ckv, cache_rt,
                                    norm_mix[i], w_dq[j], g_q[j], w_uq[j], w_dkv[j], g_kv[j],
                                    w_uk[j], w_uv[j], w_o[j])
            new_c.append(ckv)
            new_r.append(kr)
        else:
            x, u = _conv_layer(x, n_prompt, seq, t_new, state_conv[j], norm_mix[i], w_in[j], conv_w[j], w_out[j])
            new_u.append(u)
        x = _ffn_layer(x, i, norm_ffn[i], w1b, w2b)

    y_prompt = _final_norm(x, norm_final, n_prompt, 0).reshape(n_batch, seq, d)
    y_sample = _final_norm(x, norm_final, n_sample, n_prompt).reshape(s_batch, t_new, d)

    def split(a, lo, lead, inner):
        return lax.slice_in_dim(a, lo, lo + lead * inner, axis=0).reshape((lead, inner) + a.shape[1:])

    def tails(u, lo, lead, inner, keep):
        return jnp.stack([lax.slice_in_dim(u, lo + (b + 1) * inner - keep, lo + (b + 1) * inner, axis=0)
                          for b in range(lead)])

    keep = conv_w.shape[1] - 1
    ckv_p = jnp.stack([split(c, 0, n_batch, seq) for c in new_c])
    kr_p = jnp.stack([split(r, 0, n_batch, seq) for r in new_r])
    u_p = jnp.stack([tails(u, 0, n_batch, seq, keep) for u in new_u])
    ckv_s = jnp.stack([split(c, n_prompt, s_batch, t_new) for c in new_c])
    kr_s = jnp.stack([split(r, n_prompt, s_batch, t_new) for r in new_r])
    u_s = jnp.stack([tails(u, n_prompt, s_batch, t_new, keep) for u in new_u])
    return (y_prompt, y_sample, ckv_p, kr_p, u_p, ckv_s, kr_s, u_s)
```

```python
import functools
import math

import jax
import jax.numpy as jnp
from jax import lax
from jax.experimental import pallas as pl
from jax.experimental.pallas import tpu as pltpu

CHUNK = 64
ROPE_THETA = 10000.0
EPS = 1e-6
HEAD_PAD = 256
ROPE_PAD = 128
VT_PAD = 144
VMEM_LIMIT = 56 * 1024 * 1024

F32 = jnp.float32
BF16 = jnp.bfloat16


def _params(*sem):
    return pltpu.CompilerParams(dimension_semantics=sem, vmem_limit_bytes=VMEM_LIMIT)


def _tile(total, pref):
    t = min(total, pref)
    assert total % t == 0, (total, pref)
    return t


def _resident(shape):
    nd = len(shape)
    return pl.BlockSpec(shape, lambda *_: (0,) * nd, pipeline_mode=pl.Buffered(1))


def _rms(xf, g):
    return xf * lax.rsqrt(jnp.mean(xf * xf, axis=-1, keepdims=True) + EPS) * g


def _dot(a, b):
    return jnp.dot(a, b, preferred_element_type=F32)


def _dot_nt(a, b):
    return lax.dot_general(a, b, (((1,), (1,)), ((), ())), preferred_element_type=F32)


def _mla_down_kernel(x_ref, g_ref, w_ref, gq_ref, gkv_ref, cos_ref, sin_ref,
                     cq_ref, cqt_ref, ckv_ref, ckvb_ref, ckvt_ref, kr_ref, krb_ref,
                     *, q_lora, kv_lora, rope, q_scale):
    h = _rms(x_ref[...], g_ref[...]).astype(BF16)
    y = _dot(h, w_ref[...])
    cq = _rms(y[:, :q_lora], gq_ref[...]) * q_scale
    cq_ref[...] = cq.astype(BF16)
    cqt_ref[...] = cq.T.astype(BF16)
    ckv = _rms(y[:, q_lora:q_lora + kv_lora], gkv_ref[...])
    ckv_ref[...] = ckv
    ckvb_ref[...] = ckv.astype(BF16)
    ckvt_ref[...] = ckv.T.astype(BF16)
    o = q_lora + kv_lora
    kr = y[:, o:o + ROPE_PAD] * cos_ref[...] + y[:, o + ROPE_PAD:o + 2 * ROPE_PAD] * sin_ref[...]
    kr_ref[...] = kr[:, :rope]
    krb_ref[...] = kr.astype(BF16)


def _mla_q_kernel(cq_ref, wa_ref, wb_ref, cos_ref, sin_ref, q_ref, *, n_heads, nope):
    cq = cq_ref[...]
    cos = cos_ref[...]
    sin = sin_ref[...]
    for hp in range(n_heads // 2):
        a = _dot(cq, wa_ref[:, hp * 2 * HEAD_PAD:(hp + 1) * 2 * HEAD_PAD])
        b = _dot(cq, wb_ref[:, hp * 2 * ROPE_PAD:(hp + 1) * 2 * ROPE_PAD])
        for s in range(2):
            c0 = (2 * hp + s) * HEAD_PAD
            q_ref[:, c0:c0 + nope] = a[:, s * HEAD_PAD:s * HEAD_PAD + nope].astype(BF16)
            ar = a[:, s * HEAD_PAD + nope:(s + 1) * HEAD_PAD]
            br = b[:, s * ROPE_PAD:(s + 1) * ROPE_PAD]
            q_ref[:, c0 + nope:c0 + HEAD_PAD] = (ar * cos + br * sin).astype(BF16)


def _mla_qt_kernel(cqt_ref, wat_ref, wbt_ref, cost_ref, sint_ref, qt_ref, *, n_heads, nope):
    cqt = cqt_ref[...]
    cost = cost_ref[...]
    sint = sint_ref[...]
    for hp in range(n_heads // 2):
        a = _dot(wat_ref[hp * 2 * HEAD_PAD:(hp + 1) * 2 * HEAD_PAD, :], cqt)
        b = _dot(wbt_ref[hp * 2 * ROPE_PAD:(hp + 1) * 2 * ROPE_PAD, :], cqt)
        for s in range(2):
            r0 = (2 * hp + s) * HEAD_PAD
            qt_ref[r0:r0 + nope, :] = a[s * HEAD_PAD:s * HEAD_PAD + nope].astype(BF16)
            ar = a[s * HEAD_PAD + nope:(s + 1) * HEAD_PAD]
            br = b[s * ROPE_PAD:(s + 1) * ROPE_PAD]
            qt_ref[r0 + nope:r0 + HEAD_PAD, :] = (ar * cost + br * sint).astype(BF16)


def _mla_kv_kernel(ckvb_ref, ckvt_ref, krb_ref, wk_ref, wvt_ref, k_ref, vt_ref, *, n_heads, nope):
    c = ckvb_ref[...]
    krb = krb_ref[...]
    tm = c.shape[0]
    vt = _dot(wvt_ref[...], ckvt_ref[...]).astype(BF16)
    ones_row = (lax.broadcasted_iota(jnp.int32, (VT_PAD - nope, tm), 0) == 0).astype(BF16)
    for h in range(n_heads):
        vt_ref[h * VT_PAD:h * VT_PAD + nope, :] = vt[h * nope:(h + 1) * nope]
        vt_ref[h * VT_PAD + nope:(h + 1) * VT_PAD, :] = ones_row
    for hp in range(n_heads // 2):
        kn = _dot(c, wk_ref[:, hp * 2 * nope:(hp + 1) * 2 * nope])
        for s in range(2):
            c0 = (2 * hp + s) * HEAD_PAD
            k_ref[:, c0:c0 + nope] = kn[:, s * nope:(s + 1) * nope].astype(BF16)
            k_ref[:, c0 + nope:c0 + HEAD_PAD] = krb


def _flash_kernel(qt_ref, k_ref, vt_ref, o_ref, s0_ref, *, nq, tq, kb, kd, cw, v_head):
    qi = pl.program_id(2)
    ncol = tq // cw
    main = tuple((off, kb, 0) for off in range(0, tq, kb))
    diag = ((0, kb, 0),) + tuple((off, kd, off // cw) for off in range(kb, tq, kd))

    def scores(base, sub, c):
        off, size, _ = sub
        start = pl.multiple_of(base + off, size)
        return _dot(k_ref[pl.ds(start, size), :], qt_ref[:, c * cw:(c + 1) * cw])

    def update(st, sc, base, sub, c, masked):
        off, size, _ = sub
        start = pl.multiple_of(base + off, size)
        if masked and off + size > c * cw:
            kc = (lax.broadcasted_iota(jnp.int32, (size, cw), 0) + off) // CHUNK
            qc = (lax.broadcasted_iota(jnp.int32, (size, cw), 1) + c * cw) // CHUNK
            sc = jnp.where(kc <= qc, sc, -jnp.inf)
        m_prev, acc_prev = st
        m_new = jnp.maximum(m_prev, jnp.max(sc, axis=0, keepdims=True))
        p = jnp.exp2(sc - m_new).astype(BF16)
        return m_new, jnp.exp2(m_prev - m_new) * acc_prev + _dot(vt_ref[:, pl.ds(start, size)], p)

    def trip(base, state, subs, masked, stage_next):
        state = list(state)
        cur = [None] * ncol
        for t, sub in enumerate(subs):
            nxt = [None] * ncol
            for c in range(sub[2], ncol):
                if t + 1 < len(subs) and c >= subs[t + 1][2]:
                    nxt[c] = scores(base, subs[t + 1], c)
                sc = s0_ref[:, c * cw:(c + 1) * cw] if cur[c] is None else cur[c]
                state[c] = update(state[c], sc, base, sub, c, masked)
                if t + 1 == len(subs) and stage_next:
                    s0_ref[:, c * cw:(c + 1) * cw] = scores(base + tq, main[0], c)
            cur = nxt
        return tuple(state)

    for c in range(ncol):
        s0_ref[:, c * cw:(c + 1) * cw] = scores(0, main[0], c)
    init = tuple((jnp.full((1, cw), -jnp.inf, F32), jnp.zeros((vt_ref.shape[0], cw), F32)) for _ in range(ncol))
    state = init
    if nq > 1:
        state = lax.fori_loop(0, qi, lambda j, st: trip(j * tq, st, main, False, True), init)
    state = trip(qi * tq, state, diag, True, False)
    for c in range(ncol):
        acc = state[c][1]
        o_ref[c * cw:(c + 1) * cw, :] = (acc[:v_head] / acc[v_head:v_head + 1]).T.astype(BF16)


def _lane_fold(a, op):
    out = a[:, :128]
    for i in range(1, a.shape[1] // 128):
        out = op(out, a[:, i * 128:(i + 1) * 128])
    return out


def _sample_attn_kernel(q_ref, wukt_ref, wuv_ref, cc_ref, crt_ref, nc_ref, nr_ref, o_in_ref, o_ref,
                        ql_ref, qr_ref, cb_ref, s_ref, p_ref, *, n_heads, nope, rope, tk, t_new):
    del o_in_ref
    past = cc_ref.shape[0]
    for h in range(n_heads):
        c0 = h * HEAD_PAD
        ql_ref[h * t_new:(h + 1) * t_new, :] = _dot(q_ref[:, c0:c0 + nope], wukt_ref[h]).astype(BF16)
        qr_ref[h * t_new:(h + 1) * t_new, :] = q_ref[:, c0 + nope:c0 + nope + rope]
    ql = ql_ref[...]
    qr = qr_ref[...]
    nc = nc_ref[...]
    s_new = _dot_nt(ql, nc) + _dot_nt(qr, nr_ref[:, :rope])
    rows = ql.shape[0]
    nchunk = past // tk
    half = nchunk // 2
    ks = [slice(j * tk, (j + 1) * tk) for j in range(nchunk)]

    def score(j, m_lane):
        cb = cc_ref[ks[j], :].astype(BF16)
        cb_ref[ks[j], :] = cb
        s = _dot_nt(ql, cb) + _dot(qr, crt_ref[:, ks[j]].astype(BF16))
        s_ref[:, ks[j]] = s
        return jnp.maximum(m_lane, _lane_fold(s, jnp.maximum))

    def probs(j, m, l_lane):
        p = jnp.exp2(s_ref[:, ks[j]] - m)
        p_ref[:, ks[j]] = p.astype(BF16)
        return l_lane + _lane_fold(p, jnp.add)

    neg = jnp.full((rows, 128), -jnp.inf, F32)
    zero = jnp.zeros((rows, 128), F32)
    m_lane = neg
    for j in range(half):
        m_lane = score(j, m_lane)
    m_a = jnp.max(m_lane, axis=-1, keepdims=True)
    m_lane, l_a = neg, zero
    for i, j in enumerate(range(half, nchunk)):
        m_lane = score(j, m_lane)
        if i < half:
            l_a = probs(i, m_a, l_a)
    m_b = jnp.maximum(jnp.max(m_lane, axis=-1, keepdims=True), jnp.max(s_new, axis=-1, keepdims=True))
    m = jnp.maximum(m_a, m_b)
    acc_a, l_b = jnp.zeros((rows, cb_ref.shape[1]), F32), zero
    for i, j in enumerate(range(half, nchunk)):
        l_b = probs(j, m, l_b)
        if i < half:
            acc_a = acc_a + _dot(p_ref[:, ks[i]], cb_ref[ks[i], :])
    p_new = jnp.exp2(s_new - m)
    acc_b = _dot(p_new.astype(BF16), nc)
    for j in range(half, nchunk):
        acc_b = acc_b + _dot(p_ref[:, ks[j]], cb_ref[ks[j], :])
    alpha = jnp.exp2(m_a - m)
    l = (alpha * jnp.sum(l_a, axis=-1, keepdims=True) + jnp.sum(l_b, axis=-1, keepdims=True)
         + jnp.sum(p_new, axis=-1, keepdims=True))
    ol = ((alpha * acc_a + acc_b) / l).astype(BF16)
    for h in range(n_heads):
        o_ref[:, h * nope:(h + 1) * nope] = _dot(ol[h * t_new:(h + 1) * t_new, :], wuv_ref[h]).astype(BF16)


def _proj_res_kernel(a_ref, w_ref, x_ref, o_ref):
    o_ref[...] = x_ref[...] + _dot(a_ref[...], w_ref[...])


def _ffn_kernel(x_ref, g_ref, w1_ref, w2_ref, o_ref, h_ref):
    @pl.when(pl.program_id(1) == 0)
    def _():
        x = x_ref[...]
        h_ref[...] = _rms(x, g_ref[...]).astype(BF16)
        o_ref[...] = x

    a = jnp.maximum(_dot(h_ref[...], w1_ref[...]), 0.0)
    o_ref[...] += _dot((a * a).astype(BF16), w2_ref[...])


def _conv_in_kernel(x_ref, g_ref, wb_ref, wc_ref, wv_ref, u_ref, gb_ref, h_ref):
    @pl.when(pl.program_id(1) == 0)
    def _():
        h_ref[...] = _rms(x_ref[...], g_ref[...]).astype(BF16)

    h = h_ref[...]
    gb_ref[...] = _dot(h, wb_ref[...]).astype(BF16)
    u_ref[...] = _dot(h, wc_ref[...]) * _dot(h, wv_ref[...])


def _conv_mix(u, p1, p2, first, second, gb, cw, wout, x):
    u1 = jnp.where(first, p1, pltpu.roll(u, 1, axis=0))
    u2 = jnp.where(first, p2, jnp.where(second, p1, pltpu.roll(u, 2, axis=0)))
    conv = cw[0:1] * u2 + cw[1:2] * u1 + cw[2:3] * u
    return x + _dot((gb.astype(F32) * conv).astype(BF16), wout)


def _conv_out_prompt_kernel(u_ref, halo_ref, gb_ref, x_ref, cw_ref, wout_ref, o_ref):
    tm = u_ref.shape[0]
    has_past = pl.program_id(1) > 0
    halo = halo_ref[...]
    p1 = jnp.where(has_past, halo[7:8], 0.0)
    p2 = jnp.where(has_past, halo[6:7], 0.0)
    row = lax.broadcasted_iota(jnp.int32, (tm, 1), 0)
    o_ref[...] = _conv_mix(u_ref[...], p1, p2, row == 0, row == 1, gb_ref[...], cw_ref[...],
                           wout_ref[...], x_ref[...])


def _conv_out_sample_kernel(u_ref, st_ref, gb_ref, x_ref, cw_ref, wout_ref, xp_ref, o_ref, *, t_new):
    del xp_ref
    tm, d = u_ref.shape
    nb = tm // t_new
    st = st_ref[...]
    p1 = jnp.broadcast_to(st[:, 1:2, :], (nb, t_new, d)).reshape(tm, d)
    p2 = jnp.broadcast_to(st[:, 0:1, :], (nb, t_new, d)).reshape(tm, d)
    t = lax.broadcasted_iota(jnp.int32, (tm, 1), 0) % t_new
    o_ref[...] = _conv_mix(u_ref[...], p1, p2, t == 0, t == 1, gb_ref[...], cw_ref[...],
                           wout_ref[...], x_ref[...])


def _final_norm_kernel(x_ref, g_ref, o_ref):
    o_ref[...] = _rms(x_ref[...], g_ref[...])


def _row_spec(tm, n, off=0):
    return pl.BlockSpec((tm, n), lambda i, *_: (i + off, 0))


def _col_spec(n, tm):
    return pl.BlockSpec((n, tm), lambda i, *_: (0, i))


def _mla_layer(x, j, n_prompt, seq, t_new, tabs, cache_c, cache_rt, g_mix, w_dq, g_q, w_uq, w_dkv, g_kv,
               w_uk, w_uv, w_o):
    cos, sin, cost, sint = tabs
    t_all, d = x.shape
    q_lora = w_dq.shape[1]
    kv_lora, n_heads, nope = w_uk.shape
    rope = w_dkv.shape[1] - kv_lora
    half = rope // 2
    q_scale = float((nope + rope) ** -0.5 * math.log2(math.e))
    n_sample = t_all - n_prompt

    wr = w_dkv[:, kv_lora:]
    zr = jnp.zeros((d, ROPE_PAD - rope), F32)
    w_down = jnp.concatenate(
        [w_dq, w_dkv[:, :kv_lora], wr, zr, -wr[:, half:], wr[:, :half], zr], axis=1).astype(BF16)
    wq = w_uq.reshape(q_lora, n_heads, nope + rope)
    wq_r = wq[..., nope:]
    w_qa = jnp.concatenate(
        [wq, jnp.zeros((q_lora, n_heads, HEAD_PAD - nope - rope), F32)], axis=-1
    ).reshape(q_lora, n_heads * HEAD_PAD).astype(BF16)
    w_qb = jnp.concatenate(
        [-wq_r[..., half:], wq_r[..., :half], jnp.zeros((q_lora, n_heads, ROPE_PAD - rope), F32)], axis=-1
    ).reshape(q_lora, n_heads * ROPE_PAD).astype(BF16)
    w_k = w_uk.reshape(kv_lora, n_heads * nope).astype(BF16)
    w_vt = w_uv.reshape(kv_lora, n_heads * nope).T.astype(BF16)
    w_ukt = jnp.transpose(w_uk, (1, 2, 0)).astype(BF16)
    w_uvh = jnp.transpose(w_uv, (1, 0, 2)).astype(BF16)
    w_ob = w_o.astype(BF16)

    tm = _tile(t_all, 512)
    n_down = q_lora + kv_lora + 2 * ROPE_PAD
    cq, cqt, ckv, ckvb, ckvt, kr, krb = pl.pallas_call(
        functools.partial(_mla_down_kernel, q_lora=q_lora, kv_lora=kv_lora, rope=rope, q_scale=q_scale),
        grid=(t_all // tm,),
        in_specs=[_row_spec(tm, d), _resident((1, d)), _resident((d, n_down)), _resident((1, q_lora)),
                  _resident((1, kv_lora)), _row_spec(tm, ROPE_PAD), _row_spec(tm, ROPE_PAD)],
        out_specs=[_row_spec(tm, q_lora), _col_spec(q_lora, tm), _row_spec(tm, kv_lora),
                   _row_spec(tm, kv_lora), _col_spec(kv_lora, tm), _row_spec(tm, rope),
                   _row_spec(tm, ROPE_PAD)],
        out_shape=[jax.ShapeDtypeStruct((t_all, q_lora), BF16), jax.ShapeDtypeStruct((q_lora, t_all), BF16),
                   jax.ShapeDtypeStruct((t_all, kv_lora), F32), jax.ShapeDtypeStruct((t_all, kv_lora), BF16),
                   jax.ShapeDtypeStruct((kv_lora, t_all), BF16), jax.ShapeDtypeStruct((t_all, rope), F32),
                   jax.ShapeDtypeStruct((t_all, ROPE_PAD), BF16)],
        compiler_params=_params("parallel"), name="mla_down",
    )(x, g_mix.reshape(1, d), w_down, g_q.reshape(1, q_lora), g_kv.reshape(1, kv_lora), cos, sin)

    tp = _tile(n_prompt, 512)
    qt = pl.pallas_call(
        functools.partial(_mla_qt_kernel, n_heads=n_heads, nope=nope),
        grid=(n_prompt // tp,),
        in_specs=[_col_spec(q_lora, tp), _resident((n_heads * HEAD_PAD, q_lora)),
                  _resident((n_heads * ROPE_PAD, q_lora)), _col_spec(ROPE_PAD, tp), _col_spec(ROPE_PAD, tp)],
        out_specs=_col_spec(n_heads * HEAD_PAD, tp),
        out_shape=jax.ShapeDtypeStruct((n_heads * HEAD_PAD, n_prompt), BF16),
        compiler_params=_params("parallel"), name="mla_qt",
    )(cqt, w_qa.T, w_qb.T, cost, sint)

    ts = _tile(n_sample, 512)
    off = n_prompt // ts
    q_s = pl.pallas_call(
        functools.partial(_mla_q_kernel, n_heads=n_heads, nope=nope),
        grid=(n_sample // ts,),
        in_specs=[_row_spec(ts, q_lora, off), _resident(w_qa.shape), _resident(w_qb.shape),
                  _row_spec(ts, ROPE_PAD, off), _row_spec(ts, ROPE_PAD, off)],
        out_specs=_row_spec(ts, n_heads * HEAD_PAD),
        out_shape=jax.ShapeDtypeStruct((n_sample, n_heads * HEAD_PAD), BF16),
        compiler_params=_params("parallel"), name="mla_q",
    )(cq, w_qa, w_qb, cos, sin)

    k, vt = pl.pallas_call(
        functools.partial(_mla_kv_kernel, n_heads=n_heads, nope=nope),
        grid=(n_prompt // tp,),
        in_specs=[_row_spec(tp, kv_lora), _col_spec(kv_lora, tp), _row_spec(tp, ROPE_PAD),
                  _resident(w_k.shape), _resident(w_vt.shape)],
        out_specs=[_row_spec(tp, n_heads * HEAD_PAD), _col_spec(n_heads * VT_PAD, tp)],
        out_shape=[jax.ShapeDtypeStruct((n_prompt, n_heads * HEAD_PAD), BF16),
                   jax.ShapeDtypeStruct((n_heads * VT_PAD, n_prompt), BF16)],
        compiler_params=_params("parallel"), name="mla_kv",
    )(ckvb, ckvt, krb, w_k, w_vt)

    blk = _tile(seq, 4096)
    kb = _tile(blk, 256)
    nq = seq // blk
    n_batch = n_prompt // seq
    o = pl.pallas_call(
        functools.partial(_flash_kernel, nq=nq, tq=blk, kb=kb, kd=_tile(kb, 256), cw=_tile(blk, 256),
                          v_head=nope),
        grid=(n_batch, n_heads, nq),
        in_specs=[pl.BlockSpec((HEAD_PAD, blk), lambda b, h, i: (h, b * nq + i)),
                  pl.BlockSpec((seq, HEAD_PAD), lambda b, h, i: (b, h)),
                  pl.BlockSpec((VT_PAD, seq), lambda b, h, i: (h, b))],
        out_specs=pl.BlockSpec((blk, nope), lambda b, h, i: (b * nq + i, h)),
        out_shape=jax.ShapeDtypeStruct((t_all, n_heads * nope), BF16),
        scratch_shapes=[pltpu.VMEM((kb, blk), F32)],
        compiler_params=_params("parallel", "parallel", "arbitrary"), name="mla_flash",
    )(qt, k, vt)

    past = cache_c.shape[2]
    n_sb = n_sample // t_new
    row0 = n_prompt // t_new
    tk = _tile(past, 512)
    o = pl.pallas_call(
        functools.partial(_sample_attn_kernel, n_heads=n_heads, nope=nope, rope=rope, tk=tk, t_new=t_new),
        grid=(n_sb,),
        in_specs=[pl.BlockSpec((t_new, n_heads * HEAD_PAD), lambda b: (b, 0)),
                  _resident(w_ukt.shape), _resident(w_uvh.shape),
                  pl.BlockSpec((None, None, past, kv_lora), lambda b: (j, b, 0, 0)),
                  pl.BlockSpec((None, None, rope, past), lambda b: (j, b, 0, 0)),
                  pl.BlockSpec((t_new, kv_lora), lambda b: (row0 + b, 0)),
                  pl.BlockSpec((t_new, ROPE_PAD), lambda b: (row0 + b, 0)),
                  pl.BlockSpec(memory_space=pl.ANY)],
        out_specs=pl.BlockSpec((t_new, n_heads * nope), lambda b: (row0 + b, 0)),
        out_shape=jax.ShapeDtypeStruct((t_all, n_heads * nope), BF16),
        input_output_aliases={7: 0},
        scratch_shapes=[pltpu.VMEM((n_heads * t_new, kv_lora), BF16), pltpu.VMEM((n_heads * t_new, rope), BF16),
                        pltpu.VMEM((past, kv_lora), BF16), pltpu.VMEM((n_heads * t_new, past), F32),
                        pltpu.VMEM((n_heads * t_new, past), BF16)],
        compiler_params=_params("parallel"), name="mla_sample_attn",
    )(q_s, w_ukt, w_uvh, cache_c, cache_rt, ckvb, krb, o)

    x = pl.pallas_call(
        _proj_res_kernel,
        grid=(t_all // tm,),
        in_specs=[_row_spec(tm, n_heads * nope), _resident(w_ob.shape), _row_spec(tm, d)],
        out_specs=_row_spec(tm, d),
        out_shape=jax.ShapeDtypeStruct((t_all, d), F32),
        compiler_params=_params("parallel"), name="mla_out",
    )(o, w_ob, x)
    return x, ckv, kr


def _conv_layer(x, n_prompt, seq, t_new, state, g_mix, w_in, conv_w, w_out):
    assert conv_w.shape[0] == 3 and state.shape[1] == 2
    t_all, d = x.shape
    w_inb = w_in.astype(BF16)
    w_outb = w_out.astype(BF16)
    tm = _tile(t_all, 512)
    tn = _tile(d, 1024)
    nn = d // tn
    u, gb = pl.pallas_call(
        _conv_in_kernel,
        grid=(t_all // tm, nn),
        in_specs=[pl.BlockSpec((tm, d), lambda i, n: (i, 0)), _resident((1, d)),
                  pl.BlockSpec((d, tn), lambda i, n: (0, n)),
                  pl.BlockSpec((d, tn), lambda i, n: (0, nn + n)),
                  pl.BlockSpec((d, tn), lambda i, n: (0, 2 * nn + n))],
        out_specs=[pl.BlockSpec((tm, tn), lambda i, n: (i, n)), pl.BlockSpec((tm, tn), lambda i, n: (i, n))],
        out_shape=[jax.ShapeDtypeStruct((t_all, d), F32), jax.ShapeDtypeStruct((t_all, d), BF16)],
        scratch_shapes=[pltpu.VMEM((tm, d), BF16)],
        compiler_params=_params("parallel", "arbitrary"), name="conv_in",
    )(x, g_mix.reshape(1, d), w_inb, w_inb, w_inb)

    tp = _tile(seq, 512)
    ns = seq // tp
    n_batch = n_prompt // seq
    hb = tp // 8
    x_p = pl.pallas_call(
        _conv_out_prompt_kernel,
        grid=(n_batch, ns),
        in_specs=[pl.BlockSpec((tp, d), lambda b, i: (b * ns + i, 0)),
                  pl.BlockSpec((8, d), lambda b, i: (jnp.maximum((b * ns + i) * hb - 1, 0), 0)),
                  pl.BlockSpec((tp, d), lambda b, i: (b * ns + i, 0)),
                  pl.BlockSpec((tp, d), lambda b, i: (b * ns + i, 0)),
                  _resident((conv_w.shape[0], d)), _resident((d, d))],
        out_specs=pl.BlockSpec((tp, d), lambda b, i: (b * ns + i, 0)),
        out_shape=jax.ShapeDtypeStruct((t_all, d), F32),
        compiler_params=_params("parallel", "arbitrary"), name="conv_out_prompt",
    )(u, u, gb, x, conv_w, w_outb)

    n_sample = t_all - n_prompt
    ts = _tile(n_sample, 512)
    off = n_prompt // ts
    nb = ts // t_new
    x_new = pl.pallas_call(
        functools.partial(_conv_out_sample_kernel, t_new=t_new),
        grid=(n_sample // ts,),
        in_specs=[_row_spec(ts, d, off), pl.BlockSpec((nb, state.shape[1], d), lambda i: (i, 0, 0)),
                  _row_spec(ts, d, off), _row_spec(ts, d, off),
                  _resident((conv_w.shape[0], d)), _resident((d, d)), pl.BlockSpec(memory_space=pl.ANY)],
        out_specs=_row_spec(ts, d, off),
        out_shape=jax.ShapeDtypeStruct((t_all, d), F32),
        input_output_aliases={6: 0},
        compiler_params=_params("parallel"), name="conv_out_sample",
    )(u, state, gb, x, conv_w, w_outb, x_p)
    return x_new, u


def _ffn_layer(x, layer, g, w1b, w2b):
    t_all, d = x.shape
    f = w1b.shape[2]
    tm = _tile(t_all, 512)
    tf = _tile(f, 1024)
    return pl.pallas_call(
        _ffn_kernel,
        grid=(t_all // tm, f // tf),
        in_specs=[pl.BlockSpec((tm, d), lambda i, c: (i, 0)), _resident((1, d)),
                  pl.BlockSpec((None, d, tf), lambda i, c: (layer, 0, c)),
                  pl.BlockSpec((None, tf, d), lambda i, c: (layer, c, 0))],
        out_specs=pl.BlockSpec((tm, d), lambda i, c: (i, 0)),
        out_shape=jax.ShapeDtypeStruct((t_all, d), F32),
        scratch_shapes=[pltpu.VMEM((tm, d), BF16)],
        compiler_params=_params("parallel", "arbitrary"), name="ffn",
    )(x, g.reshape(1, d), w1b, w2b)


def _final_norm(x, g, rows, off_rows):
    d = x.shape[1]
    tm = _tile(rows, 512)
    return pl.pallas_call(
        _final_norm_kernel,
        grid=(rows // tm,),
        in_specs=[_row_spec(tm, d, off_rows // tm), _resident((1, d))],
        out_specs=_row_spec(tm, d),
        out_shape=jax.ShapeDtypeStruct((rows, d), F32),
        compiler_params=_params("parallel"), name="final_norm",
    )(x, g.reshape(1, d))


def kernel(x_prompt, x_sample, cache_ckv, cache_krope, state_conv, norm_mix, w_dq, g_q, w_uq, w_dkv, g_kv,
           w_uk, w_uv, w_o, w_in, conv_w, w_out, norm_ffn, w_ff1, w_ff2, norm_final):
    n_batch, seq, d = x_prompt.shape
    s_batch, t_new, _ = x_sample.shape
    past = cache_ckv.shape[2]
    depth = norm_mix.shape[0]
    kv_lora = w_uk.shape[1]
    rope = w_dkv.shape[2] - kv_lora
    half = rope // 2
    n_prompt = n_batch * seq
    n_sample = s_batch * t_new

    x = jnp.concatenate([x_prompt.reshape(n_prompt, d), x_sample.reshape(n_sample, d)], axis=0)

    pos = jnp.concatenate([jnp.tile(jnp.arange(seq, dtype=jnp.int32), n_batch),
                           jnp.tile(past + jnp.arange(t_new, dtype=jnp.int32), s_batch)])
    inv = ROPE_THETA ** (-jnp.arange(0, half, dtype=F32) / half)
    ang = pos.astype(F32)[:, None] * inv[None, :]
    zpad = jnp.zeros((pos.shape[0], ROPE_PAD - rope), F32)
    cos = jnp.concatenate([jnp.cos(ang), jnp.cos(ang), zpad], axis=1)
    sin = jnp.concatenate([jnp.sin(ang), jnp.sin(ang), zpad], axis=1)
    tabs = (cos, sin, cos[:n_prompt].T, sin[:n_prompt].T)

    cache_rt = jnp.swapaxes(cache_krope, 2, 3)
    w1b = w_ff1.astype(BF16)
    w2b = w_ff2.astype(BF16)

    new_c, new_r, new_u = [], [], []
    for i in range(depth):
        j = i // 2
        if i % 2 == 0:
            x, ckv, kr = _mla_layer(x, j, n_prompt, seq, t_new, tabs, cache_ckv, cache_rt,
                                    norm_mix[i], w_dq[j], g_q[j], w_uq[j], w_dkv[j], g_kv[j],
                                    w_uk[j], w_uv[j], w_o[j])
            new_c.append(ckv)
            new_r.append(kr)
        else:
            x, u = _conv_layer(x, n_prompt, seq, t_new, state_conv[j], norm_mix[i], w_in[j], conv_w[j], w_out[j])
            new_u.append(u)
        x = _ffn_layer(x, i, norm_ffn[i], w1b, w2b)

    y_prompt = _final_norm(x, norm_final, n_prompt, 0).reshape(n_batch, seq, d)
    y_sample = _final_norm(x, norm_final, n_sample, n_prompt).reshape(s_batch, t_new, d)

    def split(a, lo, lead, inner):
        return lax.slice_in_dim(a, lo, lo + lead * inner, axis=0).reshape((lead, inner) + a.shape[1:])

    def tails(u, lo, lead, inner, keep):
        return jnp.stack([lax.slice_in_dim(u, lo + (b + 1) * inner - keep, lo + (b + 1) * inner, axis=0)
                          for b in range(lead)])

    keep = conv_w.shape[1] - 1
    ckv_p = jnp.stack([split(c, 0, n_batch, seq) for c in new_c])
    kr_p = jnp.stack([split(r, 0, n_batch, seq) for r in new_r])
    u_p = jnp.stack([tails(u, 0, n_batch, seq, keep) for u in new_u])
    ckv_s = jnp.stack([split(c, n_prompt, s_batch, t_new) for c in new_c])
    kr_s = jnp.stack([split(r, n_prompt, s_batch, t_new) for r in new_r])
    u_s = jnp.stack([tails(u, n_prompt, s_batch, t_new, keep) for u in new_u])
    return (y_prompt, y_sample, ckv_p, kr_p, u_p, ckv_s, kr_s, u_s)
```

```python
import functools
import math

import jax
import jax.numpy as jnp
from jax import lax
from jax.experimental import pallas as pl
from jax.experimental.pallas import tpu as pltpu

CHUNK = 64
ROPE_THETA = 10000.0
EPS = 1e-6
HEAD_PAD = 256
ROPE_PAD = 128
VT_PAD = 144
VMEM_LIMIT = 56 * 1024 * 1024

F32 = jnp.float32
BF16 = jnp.bfloat16


def _params(*sem):
    return pltpu.CompilerParams(dimension_semantics=sem, vmem_limit_bytes=VMEM_LIMIT)


def _tile(total, pref):
    t = min(total, pref)
    assert total % t == 0, (total, pref)
    return t


def _resident(shape):
    nd = len(shape)
    return pl.BlockSpec(shape, lambda *_: (0,) * nd, pipeline_mode=pl.Buffered(1))


def _rms(xf, g):
    return xf * lax.rsqrt(jnp.mean(xf * xf, axis=-1, keepdims=True) + EPS) * g


def _dot(a, b):
    return jnp.dot(a, b, preferred_element_type=F32)


def _dot_nt(a, b):
    return lax.dot_general(a, b, (((1,), (1,)), ((), ())), preferred_element_type=F32)


def _load_rows(x_refs, np_tiles):
    if len(x_refs) == 1:
        return x_refs[0][...]
    return jnp.where(pl.program_id(0) < np_tiles, x_refs[0][...], x_refs[1][...])


def _store_rows(p_ref, s_ref, val, np_tiles):
    is_prompt = pl.program_id(0) < np_tiles

    @pl.when(is_prompt)
    def _():
        p_ref[...] = val

    @pl.when(jnp.logical_not(is_prompt))
    def _():
        s_ref[...] = val


def _mla_down_kernel(*refs, n_x, n_alias, np_tiles, q_lora, kv_lora, rope, q_scale):
    x_refs, refs = refs[:n_x], refs[n_x:]
    g_ref, w_ref, gq_ref, gkv_ref, cos_ref, sin_ref = refs[:6]
    cq_ref, cqt_ref, ckvb_ref, ckvt_ref, krb_ref, ckvp_ref, ckvs_ref, krp_ref, krs_ref = refs[6 + n_alias:]
    h = _rms(_load_rows(x_refs, np_tiles), g_ref[...]).astype(BF16)
    y = _dot(h, w_ref[...])
    cq = _rms(y[:, :q_lora], gq_ref[...]) * q_scale
    cq_ref[...] = cq.astype(BF16)
    cqt_ref[...] = cq.T.astype(BF16)
    ckv = _rms(y[:, q_lora:q_lora + kv_lora], gkv_ref[...])
    _store_rows(ckvp_ref, ckvs_ref, ckv, np_tiles)
    ckvb_ref[...] = ckv.astype(BF16)
    ckvt_ref[...] = ckv.T.astype(BF16)
    o = q_lora + kv_lora
    kr = y[:, o:o + ROPE_PAD] * cos_ref[...] + y[:, o + ROPE_PAD:o + 2 * ROPE_PAD] * sin_ref[...]
    _store_rows(krp_ref, krs_ref, kr[:, :rope], np_tiles)
    krb_ref[...] = kr.astype(BF16)


def _mla_q_kernel(cq_ref, wa_ref, wb_ref, cos_ref, sin_ref, q_ref, *, n_heads, nope):
    cq = cq_ref[...]
    cos = cos_ref[...]
    sin = sin_ref[...]
    for hp in range(n_heads // 2):
        a = _dot(cq, wa_ref[:, hp * 2 * HEAD_PAD:(hp + 1) * 2 * HEAD_PAD])
        b = _dot(cq, wb_ref[:, hp * 2 * ROPE_PAD:(hp + 1) * 2 * ROPE_PAD])
        for s in range(2):
            c0 = (2 * hp + s) * HEAD_PAD
            q_ref[:, c0:c0 + nope] = a[:, s * HEAD_PAD:s * HEAD_PAD + nope].astype(BF16)
            ar = a[:, s * HEAD_PAD + nope:(s + 1) * HEAD_PAD]
            br = b[:, s * ROPE_PAD:(s + 1) * ROPE_PAD]
            q_ref[:, c0 + nope:c0 + HEAD_PAD] = (ar * cos + br * sin).astype(BF16)


def _mla_qt_kernel(cqt_ref, wat_ref, wbt_ref, cost_ref, sint_ref, qt_ref, *, n_heads, nope):
    cqt = cqt_ref[...]
    cost = cost_ref[...]
    sint = sint_ref[...]
    for hp in range(n_heads // 2):
        a = _dot(wat_ref[hp * 2 * HEAD_PAD:(hp + 1) * 2 * HEAD_PAD, :], cqt)
        b = _dot(wbt_ref[hp * 2 * ROPE_PAD:(hp + 1) * 2 * ROPE_PAD, :], cqt)
        for s in range(2):
            r0 = (2 * hp + s) * HEAD_PAD
            qt_ref[r0:r0 + nope, :] = a[s * HEAD_PAD:s * HEAD_PAD + nope].astype(BF16)
            ar = a[s * HEAD_PAD + nope:(s + 1) * HEAD_PAD]
            br = b[s * ROPE_PAD:(s + 1) * ROPE_PAD]
            qt_ref[r0 + nope:r0 + HEAD_PAD, :] = (ar * cost + br * sint).astype(BF16)


def _mla_kv_kernel(ckvb_ref, ckvt_ref, krb_ref, wk_ref, wvt_ref, k_ref, vt_ref, *, n_heads, nope):
    c = ckvb_ref[...]
    krb = krb_ref[...]
    tm = c.shape[0]
    vt = _dot(wvt_ref[...], ckvt_ref[...]).astype(BF16)
    ones_row = (lax.broadcasted_iota(jnp.int32, (VT_PAD - nope, tm), 0) == 0).astype(BF16)
    for h in range(n_heads):
        vt_ref[h * VT_PAD:h * VT_PAD + nope, :] = vt[h * nope:(h + 1) * nope]
        vt_ref[h * VT_PAD + nope:(h + 1) * VT_PAD, :] = ones_row
    for hp in range(n_heads // 2):
        kn = _dot(c, wk_ref[:, hp * 2 * nope:(hp + 1) * 2 * nope])
        for s in range(2):
            c0 = (2 * hp + s) * HEAD_PAD
            k_ref[:, c0:c0 + nope] = kn[:, s * nope:(s + 1) * nope].astype(BF16)
            k_ref[:, c0 + nope:c0 + HEAD_PAD] = krb


def _flash_kernel(qt_ref, k_ref, vt_ref, o_ref, s0_ref, *, nq, tq, kb, kd, cw, v_head):
    qi = pl.program_id(2)
    ncol = tq // cw
    main = tuple((off, kb, 0) for off in range(0, tq, kb))
    diag = ((0, kb, 0),) + tuple((off, kd, off // cw) for off in range(kb, tq, kd))

    def scores(base, sub, c):
        off, size, _ = sub
        start = pl.multiple_of(base + off, size)
        return _dot(k_ref[pl.ds(start, size), :], qt_ref[:, c * cw:(c + 1) * cw])

    def update(st, sc, base, sub, c, masked):
        off, size, _ = sub
        start = pl.multiple_of(base + off, size)
        if masked and off + size > c * cw:
            kc = (lax.broadcasted_iota(jnp.int32, (size, cw), 0) + off) // CHUNK
            qc = (lax.broadcasted_iota(jnp.int32, (size, cw), 1) + c * cw) // CHUNK
            sc = jnp.where(kc <= qc, sc, -jnp.inf)
        m_prev, acc_prev = st
        m_new = jnp.maximum(m_prev, jnp.max(sc, axis=0, keepdims=True))
        p = jnp.exp2(sc - m_new).astype(BF16)
        return m_new, jnp.exp2(m_prev - m_new) * acc_prev + _dot(vt_ref[:, pl.ds(start, size)], p)

    def trip(base, state, subs, masked, stage_next):
        state = list(state)
        cur = [None] * ncol
        for t, sub in enumerate(subs):
            nxt = [None] * ncol
            for c in range(sub[2], ncol):
                if t + 1 < len(subs) and c >= subs[t + 1][2]:
                    nxt[c] = scores(base, subs[t + 1], c)
                sc = s0_ref[:, c * cw:(c + 1) * cw] if cur[c] is None else cur[c]
                state[c] = update(state[c], sc, base, sub, c, masked)
                if t + 1 == len(subs) and stage_next:
                    s0_ref[:, c * cw:(c + 1) * cw] = scores(base + tq, main[0], c)
            cur = nxt
        return tuple(state)

    for c in range(ncol):
        s0_ref[:, c * cw:(c + 1) * cw] = scores(0, main[0], c)
    init = tuple((jnp.full((1, cw), -jnp.inf, F32), jnp.zeros((vt_ref.shape[0], cw), F32)) for _ in range(ncol))
    state = init
    if nq > 1:
        state = lax.fori_loop(0, qi, lambda j, st: trip(j * tq, st, main, False, True), init)
    state = trip(qi * tq, state, diag, True, False)
    for c in range(ncol):
        acc = state[c][1]
        o_ref[c * cw:(c + 1) * cw, :] = (acc[:v_head] / acc[v_head:v_head + 1]).T.astype(BF16)


def _lane_fold(a, op):
    out = a[:, :128]
    for i in range(1, a.shape[1] // 128):
        out = op(out, a[:, i * 128:(i + 1) * 128])
    return out


def _sample_attn_kernel(q_ref, wukt_ref, wuv_ref, cc_ref, crt_ref, nc_ref, nr_ref, o_in_ref, o_ref,
                        ql_ref, qr_ref, cb_ref, s_ref, p_ref, *, n_heads, nope, rope, tk, t_new):
    del o_in_ref
    past = cc_ref.shape[0]
    for h in range(n_heads):
        c0 = h * HEAD_PAD
        ql_ref[h * t_new:(h + 1) * t_new, :] = _dot(q_ref[:, c0:c0 + nope], wukt_ref[h]).astype(BF16)
        qr_ref[h * t_new:(h + 1) * t_new, :] = q_ref[:, c0 + nope:c0 + nope + rope]
    ql = ql_ref[...]
    qr = qr_ref[...]
    nc = nc_ref[...]
    s_new = _dot_nt(ql, nc) + _dot_nt(qr, nr_ref[:, :rope])
    rows = ql.shape[0]
    nchunk = past // tk
    half = nchunk // 2
    ks = [slice(j * tk, (j + 1) * tk) for j in range(nchunk)]

    def score(j, m_lane):
        cb = cc_ref[ks[j], :].astype(BF16)
        cb_ref[ks[j], :] = cb
        s = _dot_nt(ql, cb) + _dot(qr, crt_ref[:, ks[j]].astype(BF16))
        s_ref[:, ks[j]] = s
        return jnp.maximum(m_lane, _lane_fold(s, jnp.maximum))

    def probs(j, m, l_lane):
        p = jnp.exp2(s_ref[:, ks[j]] - m)
        p_ref[:, ks[j]] = p.astype(BF16)
        return l_lane + _lane_fold(p, jnp.add)

    neg = jnp.full((rows, 128), -jnp.inf, F32)
    zero = jnp.zeros((rows, 128), F32)
    m_lane = neg
    for j in range(half):
        m_lane = score(j, m_lane)
    m_a = jnp.max(m_lane, axis=-1, keepdims=True)
    m_lane, l_a = neg, zero
    for i, j in enumerate(range(half, nchunk)):
        m_lane = score(j, m_lane)
        if i < half:
            l_a = probs(i, m_a, l_a)
    m_b = jnp.maximum(jnp.max(m_lane, axis=-1, keepdims=True), jnp.max(s_new, axis=-1, keepdims=True))
    m = jnp.maximum(m_a, m_b)
    acc_a, l_b = jnp.zeros((rows, cb_ref.shape[1]), F32), zero
    for i, j in enumerate(range(half, nchunk)):
        l_b = probs(j, m, l_b)
        if i < half:
            acc_a = acc_a + _dot(p_ref[:, ks[i]], cb_ref[ks[i], :])
    p_new = jnp.exp2(s_new - m)
    acc_b = _dot(p_new.astype(BF16), nc)
    for j in range(half, nchunk):
        acc_b = acc_b + _dot(p_ref[:, ks[j]], cb_ref[ks[j], :])
    alpha = jnp.exp2(m_a - m)
    l = (alpha * jnp.sum(l_a, axis=-1, keepdims=True) + jnp.sum(l_b, axis=-1, keepdims=True)
         + jnp.sum(p_new, axis=-1, keepdims=True))
    ol = ((alpha * acc_a + acc_b) / l).astype(BF16)
    for h in range(n_heads):
        o_ref[:, h * nope:(h + 1) * nope] = _dot(ol[h * t_new:(h + 1) * t_new, :], wuv_ref[h]).astype(BF16)


def _proj_res_kernel(a_ref, w_ref, *refs, np_tiles):
    *x_refs, o_ref = refs
    o_ref[...] = _load_rows(x_refs, np_tiles) + _dot(a_ref[...], w_ref[...])


def _ffn_kernel(x_ref, g_ref, w1_ref, w2_ref, o_ref, h_ref):
    @pl.when(pl.program_id(1) == 0)
    def _():
        x = x_ref[...]
        h_ref[...] = _rms(x, g_ref[...]).astype(BF16)
        o_ref[...] = x

    a = jnp.maximum(_dot(h_ref[...], w1_ref[...]), 0.0)
    o_ref[...] += _dot((a * a).astype(BF16), w2_ref[...])


def _ffn_final_kernel(x_ref, g_ref, w1_ref, w2_ref, gf_ref, yp_ref, ys_ref, h_ref, acc_ref, *, np_tiles):
    @pl.when(pl.program_id(1) == 0)
    def _():
        x = x_ref[...]
        h_ref[...] = _rms(x, g_ref[...]).astype(BF16)
        acc_ref[...] = x

    a = jnp.maximum(_dot(h_ref[...], w1_ref[...]), 0.0)
    acc_ref[...] += _dot((a * a).astype(BF16), w2_ref[...])

    @pl.when(pl.program_id(1) == pl.num_programs(1) - 1)
    def _():
        _store_rows(yp_ref, ys_ref, _rms(acc_ref[...], gf_ref[...]), np_tiles)


def _conv_in_kernel(x_ref, g_ref, wb_ref, wc_ref, wv_ref, u_ref, gb_ref, h_ref):
    @pl.when(pl.program_id(1) == 0)
    def _():
        h_ref[...] = _rms(x_ref[...], g_ref[...]).astype(BF16)

    h = h_ref[...]
    gb_ref[...] = _dot(h, wb_ref[...]).astype(BF16)
    u_ref[...] = _dot(h, wc_ref[...]) * _dot(h, wv_ref[...])


def _conv_mix(u, p1, p2, first, second, gb, cw, wout, x):
    u1 = jnp.where(first, p1, pltpu.roll(u, 1, axis=0))
    u2 = jnp.where(first, p2, jnp.where(second, p1, pltpu.roll(u, 2, axis=0)))
    conv = cw[0:1] * u2 + cw[1:2] * u1 + cw[2:3] * u
    return x + _dot((gb.astype(F32) * conv).astype(BF16), wout)


def _conv_out_prompt_kernel(u_ref, halo_ref, gb_ref, x_ref, cw_ref, wout_ref, o_ref):
    tm = u_ref.shape[0]
    has_past = pl.program_id(1) > 0
    halo = halo_ref[...]
    p1 = jnp.where(has_past, halo[7:8], 0.0)
    p2 = jnp.where(has_past, halo[6:7], 0.0)
    row = lax.broadcasted_iota(jnp.int32, (tm, 1), 0)
    o_ref[...] = _conv_mix(u_ref[...], p1, p2, row == 0, row == 1, gb_ref[...], cw_ref[...],
                           wout_ref[...], x_ref[...])


def _conv_out_sample_kernel(u_ref, st_ref, gb_ref, x_ref, cw_ref, wout_ref, xp_ref, o_ref, *, t_new):
    del xp_ref
    tm, d = u_ref.shape
    nb = tm // t_new
    st = st_ref[...]
    p1 = jnp.broadcast_to(st[:, 1:2, :], (nb, t_new, d)).reshape(tm, d)
    p2 = jnp.broadcast_to(st[:, 0:1, :], (nb, t_new, d)).reshape(tm, d)
    t = lax.broadcasted_iota(jnp.int32, (tm, 1), 0) % t_new
    o_ref[...] = _conv_mix(u_ref[...], p1, p2, t == 0, t == 1, gb_ref[...], cw_ref[...],
                           wout_ref[...], x_ref[...])


def _row_spec(tm, n, off=0):
    return pl.BlockSpec((tm, n), lambda i, *_: (i + off, 0))


def _col_spec(n, tm):
    return pl.BlockSpec((n, tm), lambda i, *_: (0, i))


def _split_row_specs(tm, n, np_tiles, layer=None):
    def p_map(i, *_):
        return (jnp.minimum(i, np_tiles - 1), 0)

    def s_map(i, *_):
        return (jnp.maximum(i - np_tiles, 0), 0)

    if layer is None:
        return [pl.BlockSpec((tm, n), p_map), pl.BlockSpec((tm, n), s_map)]
    return [pl.BlockSpec((None, tm, n), lambda i, *_: (layer,) + p_map(i)),
            pl.BlockSpec((None, tm, n), lambda i, *_: (layer,) + s_map(i))]


def _mla_layer(xs, j, prev, n_prompt, seq, t_new, tabs, cache_c, cache_rt, g_mix, w_dq, g_q, w_uq, w_dkv, g_kv,
               w_uk, w_uv, w_ob):
    cos, sin, cost, sint = tabs
    t_all, d = sum(a.shape[0] for a in xs), xs[0].shape[1]
    q_lora = w_dq.shape[1]
    kv_lora, n_heads, nope = w_uk.shape
    rope = w_dkv.shape[1] - kv_lora
    half = rope // 2
    q_scale = float((nope + rope) ** -0.5 * math.log2(math.e))
    n_sample = t_all - n_prompt

    wr = w_dkv[:, kv_lora:]
    zr = jnp.zeros((d, ROPE_PAD - rope), F32)
    w_down = jnp.concatenate(
        [w_dq, w_dkv[:, :kv_lora], wr, zr, -wr[:, half:], wr[:, :half], zr], axis=1).astype(BF16)
    wq = w_uq.reshape(q_lora, n_heads, nope + rope)
    wq_r = wq[..., nope:]
    w_qa = jnp.concatenate(
        [wq, jnp.zeros((q_lora, n_heads, HEAD_PAD - nope - rope), F32)], axis=-1
    ).reshape(q_lora, n_heads * HEAD_PAD).astype(BF16)
    w_qb = jnp.concatenate(
        [-wq_r[..., half:], wq_r[..., :half], jnp.zeros((q_lora, n_heads, ROPE_PAD - rope), F32)], axis=-1
    ).reshape(q_lora, n_heads * ROPE_PAD).astype(BF16)
    w_k = w_uk.reshape(kv_lora, n_heads * nope).astype(BF16)
    w_vt = w_uv.reshape(kv_lora, n_heads * nope).T.astype(BF16)
    w_ukt = jnp.transpose(w_uk, (1, 2, 0)).astype(BF16)
    w_uvh = jnp.transpose(w_uv, (1, 0, 2)).astype(BF16)

    tm = _tile(n_sample, 512)
    np_tiles = n_prompt // tm
    n_layers = cache_c.shape[0]
    n_down = q_lora + kv_lora + 2 * ROPE_PAD
    x_specs = _split_row_specs(tm, d, np_tiles) if len(xs) == 2 else [_row_spec(tm, d)]
    new_shapes = [(n_layers, n_prompt, kv_lora), (n_layers, n_sample, kv_lora),
                  (n_layers, n_prompt, rope), (n_layers, n_sample, rope)]
    n_alias = 0 if prev is None else len(prev)
    n_in = len(xs) + 6
    cq, cqt, ckvb, ckvt, krb, *new = pl.pallas_call(
        functools.partial(_mla_down_kernel, n_x=len(xs), n_alias=n_alias, np_tiles=np_tiles,
                          q_lora=q_lora, kv_lora=kv_lora, rope=rope, q_scale=q_scale),
        grid=(t_all // tm,),
        in_specs=x_specs + [_resident((1, d)), _resident((d, n_down)), _resident((1, q_lora)),
                            _resident((1, kv_lora)), _row_spec(tm, ROPE_PAD), _row_spec(tm, ROPE_PAD)]
        + [pl.BlockSpec(memory_space=pl.ANY)] * n_alias,
        out_specs=[_row_spec(tm, q_lora), _col_spec(q_lora, tm), _row_spec(tm, kv_lora),
                   _col_spec(kv_lora, tm), _row_spec(tm, ROPE_PAD)]
        + _split_row_specs(tm, kv_lora, np_tiles, j) + _split_row_specs(tm, rope, np_tiles, j),
        out_shape=[jax.ShapeDtypeStruct((t_all, q_lora), BF16), jax.ShapeDtypeStruct((q_lora, t_all), BF16),
                   jax.ShapeDtypeStruct((t_all, kv_lora), BF16), jax.ShapeDtypeStruct((kv_lora, t_all), BF16),
                   jax.ShapeDtypeStruct((t_all, ROPE_PAD), BF16)]
        + [jax.ShapeDtypeStruct(s, F32) for s in new_shapes],
        input_output_aliases={n_in + a: 5 + a for a in range(n_alias)},
        compiler_params=_params("arbitrary"), name="mla_down",
    )(*xs, g_mix.reshape(1, d), w_down, g_q.reshape(1, q_lora), g_kv.reshape(1, kv_lora), cos, sin,
      *(prev or ()))

    tp = _tile(n_prompt, 512)
    qt = pl.pallas_call(
        functools.partial(_mla_qt_kernel, n_heads=n_heads, nope=nope),
        grid=(n_prompt // tp,),
        in_specs=[_col_spec(q_lora, tp), _resident((n_heads * HEAD_PAD, q_lora)),
                  _resident((n_heads * ROPE_PAD, q_lora)), _col_spec(ROPE_PAD, tp), _col_spec(ROPE_PAD, tp)],
        out_specs=_col_spec(n_heads * HEAD_PAD, tp),
        out_shape=jax.ShapeDtypeStruct((n_heads * HEAD_PAD, n_prompt), BF16),
        compiler_params=_params("parallel"), name="mla_qt",
    )(cqt, w_qa.T, w_qb.T, cost, sint)

    ts = _tile(n_sample, 512)
    off = n_prompt // ts
    q_s = pl.pallas_call(
        functools.partial(_mla_q_kernel, n_heads=n_heads, nope=nope),
        grid=(n_sample // ts,),
        in_specs=[_row_spec(ts, q_lora, off), _resident(w_qa.shape), _resident(w_qb.shape),
                  _row_spec(ts, ROPE_PAD, off), _row_spec(ts, ROPE_PAD, off)],
        out_specs=_row_spec(ts, n_heads * HEAD_PAD),
        out_shape=jax.ShapeDtypeStruct((n_sample, n_heads * HEAD_PAD), BF16),
        compiler_params=_params("parallel"), name="mla_q",
    )(cq, w_qa, w_qb, cos, sin)

    k, vt = pl.pallas_call(
        functools.partial(_mla_kv_kernel, n_heads=n_heads, nope=nope),
        grid=(n_prompt // tp,),
        in_specs=[_row_spec(tp, kv_lora), _col_spec(kv_lora, tp), _row_spec(tp, ROPE_PAD),
                  _resident(w_k.shape), _resident(w_vt.shape)],
        out_specs=[_row_spec(tp, n_heads * HEAD_PAD), _col_spec(n_heads * VT_PAD, tp)],
        out_shape=[jax.ShapeDtypeStruct((n_prompt, n_heads * HEAD_PAD), BF16),
                   jax.ShapeDtypeStruct((n_heads * VT_PAD, n_prompt), BF16)],
        compiler_params=_params("parallel"), name="mla_kv",
    )(ckvb, ckvt, krb, w_k, w_vt)

    blk = _tile(seq, 4096)
    kb = _tile(blk, 256)
    nq = seq // blk
    n_batch = n_prompt // seq
    o = pl.pallas_call(
        functools.partial(_flash_kernel, nq=nq, tq=blk, kb=kb, kd=_tile(kb, 256), cw=_tile(blk, 256),
                          v_head=nope),
        grid=(n_batch, n_heads, nq),
        in_specs=[pl.BlockSpec((HEAD_PAD, blk), lambda b, h, i: (h, b * nq + i)),
                  pl.BlockSpec((seq, HEAD_PAD), lambda b, h, i: (b, h)),
                  pl.BlockSpec((VT_PAD, seq), lambda b, h, i: (h, b))],
        out_specs=pl.BlockSpec((blk, nope), lambda b, h, i: (b * nq + i, h)),
        out_shape=jax.ShapeDtypeStruct((t_all, n_heads * nope), BF16),
        scratch_shapes=[pltpu.VMEM((kb, blk), F32)],
        compiler_params=_params("parallel", "parallel", "arbitrary"), name="mla_flash",
    )(qt, k, vt)

    past = cache_c.shape[2]
    n_sb = n_sample // t_new
    row0 = n_prompt // t_new
    tk = _tile(past, 512)
    o = pl.pallas_call(
        functools.partial(_sample_attn_kernel, n_heads=n_heads, nope=nope, rope=rope, tk=tk, t_new=t_new),
        grid=(n_sb,),
        in_specs=[pl.BlockSpec((t_new, n_heads * HEAD_PAD), lambda b: (b, 0)),
                  _resident(w_ukt.shape), _resident(w_uvh.shape),
                  pl.BlockSpec((None, None, past, kv_lora), lambda b: (j, b, 0, 0)),
                  pl.BlockSpec((None, None, rope, past), lambda b: (j, b, 0, 0)),
                  pl.BlockSpec((t_new, kv_lora), lambda b: (row0 + b, 0)),
                  pl.BlockSpec((t_new, ROPE_PAD), lambda b: (row0 + b, 0)),
                  pl.BlockSpec(memory_space=pl.ANY)],
        out_specs=pl.BlockSpec((t_new, n_heads * nope), lambda b: (row0 + b, 0)),
        out_shape=jax.ShapeDtypeStruct((t_all, n_heads * nope), BF16),
        input_output_aliases={7: 0},
        scratch_shapes=[pltpu.VMEM((n_heads * t_new, kv_lora), BF16), pltpu.VMEM((n_heads * t_new, rope), BF16),
                        pltpu.VMEM((past, kv_lora), BF16), pltpu.VMEM((n_heads * t_new, past), F32),
                        pltpu.VMEM((n_heads * t_new, past), BF16)],
        compiler_params=_params("parallel"), name="mla_sample_attn",
    )(q_s, w_ukt, w_uvh, cache_c, cache_rt, ckvb, krb, o)

    n_o = n_heads * nope
    x = pl.pallas_call(
        functools.partial(_proj_res_kernel, np_tiles=np_tiles),
        grid=(t_all // tm,),
        in_specs=[_row_spec(tm, n_o), pl.BlockSpec((None, n_o, d), lambda i: (j, 0, 0),
                                                   pipeline_mode=pl.Buffered(1))] + x_specs,
        out_specs=_row_spec(tm, d),
        out_shape=jax.ShapeDtypeStruct((t_all, d), F32),
        compiler_params=_params("parallel"), name="mla_out",
    )(o, w_ob, *xs)
    return x, new


def _conv_layer(x, j, n_prompt, seq, t_new, state, g_mix, w_inb, conv_w, w_outb):
    assert conv_w.shape[0] == 3 and state.shape[1] == 2
    t_all, d = x.shape
    tm = _tile(t_all, 512)
    tn = _tile(d, 1024)
    nn = d // tn
    w_out_spec = pl.BlockSpec((None, d, d), lambda *_: (j, 0, 0), pipeline_mode=pl.Buffered(1))
    u, gb = pl.pallas_call(
        _conv_in_kernel,
        grid=(t_all // tm, nn),
        in_specs=[pl.BlockSpec((tm, d), lambda i, n: (i, 0)), _resident((1, d)),
                  pl.BlockSpec((None, d, tn), lambda i, n: (j, 0, n)),
                  pl.BlockSpec((None, d, tn), lambda i, n: (j, 0, nn + n)),
                  pl.BlockSpec((None, d, tn), lambda i, n: (j, 0, 2 * nn + n))],
        out_specs=[pl.BlockSpec((tm, tn), lambda i, n: (i, n)), pl.BlockSpec((tm, tn), lambda i, n: (i, n))],
        out_shape=[jax.ShapeDtypeStruct((t_all, d), F32), jax.ShapeDtypeStruct((t_all, d), BF16)],
        scratch_shapes=[pltpu.VMEM((tm, d), BF16)],
        compiler_params=_params("parallel", "arbitrary"), name="conv_in",
    )(x, g_mix.reshape(1, d), w_inb, w_inb, w_inb)

    tp = _tile(seq, 512)
    ns = seq // tp
    n_batch = n_prompt // seq
    hb = tp // 8
    x_p = pl.pallas_call(
        _conv_out_prompt_kernel,
        grid=(n_batch, ns),
        in_specs=[pl.BlockSpec((tp, d), lambda b, i: (b * ns + i, 0)),
                  pl.BlockSpec((8, d), lambda b, i: (jnp.maximum((b * ns + i) * hb - 1, 0), 0)),
                  pl.BlockSpec((tp, d), lambda b, i: (b * ns + i, 0)),
                  pl.BlockSpec((tp, d), lambda b, i: (b * ns + i, 0)),
                  _resident((conv_w.shape[0], d)), w_out_spec],
        out_specs=pl.BlockSpec((tp, d), lambda b, i: (b * ns + i, 0)),
        out_shape=jax.ShapeDtypeStruct((t_all, d), F32),
        compiler_params=_params("parallel", "arbitrary"), name="conv_out_prompt",
    )(u, u, gb, x, conv_w, w_outb)

    n_sample = t_all - n_prompt
    ts = _tile(n_sample, 512)
    off = n_prompt // ts
    nb = ts // t_new
    x_new = pl.pallas_call(
        functools.partial(_conv_out_sample_kernel, t_new=t_new),
        grid=(n_sample // ts,),
        in_specs=[_row_spec(ts, d, off), pl.BlockSpec((nb, state.shape[1], d), lambda i: (i, 0, 0)),
                  _row_spec(ts, d, off), _row_spec(ts, d, off),
                  _resident((conv_w.shape[0], d)), w_out_spec, pl.BlockSpec(memory_space=pl.ANY)],
        out_specs=_row_spec(ts, d, off),
        out_shape=jax.ShapeDtypeStruct((t_all, d), F32),
        input_output_aliases={6: 0},
        compiler_params=_params("parallel"), name="conv_out_sample",
    )(u, state, gb, x, conv_w, w_outb, x_p)
    return x_new, u


def _ffn_layer(x, layer, g, w1b, w2b):
    t_all, d = x.shape
    f = w1b.shape[2]
    tm = _tile(t_all, 512)
    tf = _tile(f, 1024)
    return pl.pallas_call(
        _ffn_kernel,
        grid=(t_all // tm, f // tf),
        in_specs=[pl.BlockSpec((tm, d), lambda i, c: (i, 0)), _resident((1, d)),
                  pl.BlockSpec((None, d, tf), lambda i, c: (layer, 0, c)),
                  pl.BlockSpec((None, tf, d), lambda i, c: (layer, c, 0))],
        out_specs=pl.BlockSpec((tm, d), lambda i, c: (i, 0)),
        out_shape=jax.ShapeDtypeStruct((t_all, d), F32),
        scratch_shapes=[pltpu.VMEM((tm, d), BF16)],
        compiler_params=_params("parallel", "arbitrary"), name="ffn",
    )(x, g.reshape(1, d), w1b, w2b)


def _ffn_final_layer(x, layer, g, w1b, w2b, g_final, n_prompt):
    t_all, d = x.shape
    f = w1b.shape[2]
    tm = _tile(t_all - n_prompt, 512)
    tf = _tile(f, 1024)
    np_tiles = n_prompt // tm
    return pl.pallas_call(
        functools.partial(_ffn_final_kernel, np_tiles=np_tiles),
        grid=(t_all // tm, f // tf),
        in_specs=[pl.BlockSpec((tm, d), lambda i, c: (i, 0)), _resident((1, d)),
                  pl.BlockSpec((None, d, tf), lambda i, c: (layer, 0, c)),
                  pl.BlockSpec((None, tf, d), lambda i, c: (layer, c, 0)), _resident((1, d))],
        out_specs=_split_row_specs(tm, d, np_tiles),
        out_shape=[jax.ShapeDtypeStruct((n_prompt, d), F32), jax.ShapeDtypeStruct((t_all - n_prompt, d), F32)],
        scratch_shapes=[pltpu.VMEM((tm, d), BF16), pltpu.VMEM((tm, d), F32)],
        compiler_params=_params("arbitrary", "arbitrary"), name="ffn_final",
    )(x, g.reshape(1, d), w1b, w2b, g_final.reshape(1, d))


def kernel(x_prompt, x_sample, cache_ckv, cache_krope, state_conv, norm_mix, w_dq, g_q, w_uq, w_dkv, g_kv,
           w_uk, w_uv, w_o, w_in, conv_w, w_out, norm_ffn, w_ff1, w_ff2, norm_final):
    n_batch, seq, d = x_prompt.shape
    s_batch, t_new, _ = x_sample.shape
    past = cache_ckv.shape[2]
    depth = norm_mix.shape[0]
    kv_lora = w_uk.shape[1]
    rope = w_dkv.shape[2] - kv_lora
    half = rope // 2
    n_prompt = n_batch * seq
    n_sample = s_batch * t_new

    pos = jnp.concatenate([jnp.tile(jnp.arange(seq, dtype=jnp.int32), n_batch),
                           jnp.tile(past + jnp.arange(t_new, dtype=jnp.int32), s_batch)])
    inv = ROPE_THETA ** (-jnp.arange(0, half, dtype=F32) / half)
    ang = pos.astype(F32)[:, None] * inv[None, :]
    zpad = jnp.zeros((pos.shape[0], ROPE_PAD - rope), F32)
    cos = jnp.concatenate([jnp.cos(ang), jnp.cos(ang), zpad], axis=1)
    sin = jnp.concatenate([jnp.sin(ang), jnp.sin(ang), zpad], axis=1)
    tabs = (cos, sin, cos[:n_prompt].T, sin[:n_prompt].T)

    cache_rt = jnp.swapaxes(cache_krope, 2, 3)
    w1b = w_ff1.astype(BF16)
    w2b = w_ff2.astype(BF16)
    w_ob = w_o.astype(BF16)
    w_inb = w_in.astype(BF16)
    w_outb = w_out.astype(BF16)

    xs = (x_prompt.reshape(n_prompt, d), x_sample.reshape(n_sample, d))
    latents, new_u = None, []
    for i in range(depth):
        j = i // 2
        if i % 2 == 0:
            x, latents = _mla_layer(xs, j, latents, n_prompt, seq, t_new, tabs, cache_ckv, cache_rt,
                                    norm_mix[i], w_dq[j], g_q[j], w_uq[j], w_dkv[j], g_kv[j],
                                    w_uk[j], w_uv[j], w_ob)
        else:
            x, u = _conv_layer(xs[0], j, n_prompt, seq, t_new, state_conv[j], norm_mix[i], w_inb, conv_w[j],
                               w_outb)
            new_u.append(u)
        if i + 1 < depth:
            xs = (_ffn_layer(x, i, norm_ffn[i], w1b, w2b),)
        else:
            y_prompt, y_sample = _ffn_final_layer(x, i, norm_ffn[i], w1b, w2b, norm_final, n_prompt)

    def tails(u, lo, lead, inner, keep):
        return jnp.stack([lax.slice_in_dim(u, lo + (b + 1) * inner - keep, lo + (b + 1) * inner, axis=0)
                          for b in range(lead)])

    keep = conv_w.shape[1] - 1
    ckv_p, ckv_s, kr_p, kr_s = latents
    n_a = ckv_p.shape[0]
    u_p = jnp.stack([tails(u, 0, n_batch, seq, keep) for u in new_u])
    u_s = jnp.stack([tails(u, n_prompt, s_batch, t_new, keep) for u in new_u])
    return (y_prompt.reshape(n_batch, seq, d), y_sample.reshape(s_batch, t_new, d),
            ckv_p.reshape(n_a, n_batch, seq, -1), kr_p.reshape(n_a, n_batch, seq, -1), u_p,
            ckv_s.reshape(n_a, s_batch, t_new, -1), kr_s.reshape(n_a, s_batch, t_new, -1), u_s)
```

```python
import functools
import math

import jax
import jax.numpy as jnp
from jax import lax
from jax.experimental import pallas as pl
from jax.experimental.pallas import tpu as pltpu

CHUNK = 64
ROPE_THETA = 10000.0
EPS = 1e-6
HEAD_PAD = 256
ROPE_PAD = 128
VT_PAD = 144
VMEM_LIMIT = 56 * 1024 * 1024

F32 = jnp.float32
BF16 = jnp.bfloat16


def _params(*sem):
    return pltpu.CompilerParams(dimension_semantics=sem, vmem_limit_bytes=VMEM_LIMIT)


def _tile(total, pref):
    t = min(total, pref)
    assert total % t == 0, (total, pref)
    return t


def _resident(shape):
    nd = len(shape)
    return pl.BlockSpec(shape, lambda *_: (0,) * nd, pipeline_mode=pl.Buffered(1))


def _rms(xf, g):
    return xf * lax.rsqrt(jnp.mean(xf * xf, axis=-1, keepdims=True) + EPS) * g


def _dot(a, b):
    return jnp.dot(a, b, preferred_element_type=F32)


def _dot_nt(a, b):
    return lax.dot_general(a, b, (((1,), (1,)), ((), ())), preferred_element_type=F32)


def _load_rows(x_refs, np_tiles):
    if len(x_refs) == 1:
        return x_refs[0][...]
    return jnp.where(pl.program_id(0) < np_tiles, x_refs[0][...], x_refs[1][...])


def _store_rows(p_ref, s_ref, val, np_tiles):
    is_prompt = pl.program_id(0) < np_tiles

    @pl.when(is_prompt)
    def _():
        p_ref[...] = val

    @pl.when(jnp.logical_not(is_prompt))
    def _():
        s_ref[...] = val


def _mla_down_kernel(*refs, n_x, n_alias, np_tiles, q_lora, kv_lora, rope, q_scale):
    x_refs, refs = refs[:n_x], refs[n_x:]
    g_ref, w_ref, gq_ref, gkv_ref, cos_ref, sin_ref = refs[:6]
    cq_ref, cqt_ref, ckvb_ref, ckvt_ref, krb_ref, ckvp_ref, ckvs_ref, krp_ref, krs_ref = refs[6 + n_alias:]
    h = _rms(_load_rows(x_refs, np_tiles), g_ref[...]).astype(BF16)
    y = _dot(h, w_ref[...])
    cq = _rms(y[:, :q_lora], gq_ref[...]) * q_scale
    cq_ref[...] = cq.astype(BF16)
    cqt_ref[...] = cq.T.astype(BF16)
    ckv = _rms(y[:, q_lora:q_lora + kv_lora], gkv_ref[...])
    _store_rows(ckvp_ref, ckvs_ref, ckv, np_tiles)
    ckvb_ref[...] = ckv.astype(BF16)
    ckvt_ref[...] = ckv.T.astype(BF16)
    o = q_lora + kv_lora
    kr = y[:, o:o + ROPE_PAD] * cos_ref[...] + y[:, o + ROPE_PAD:o + 2 * ROPE_PAD] * sin_ref[...]
    _store_rows(krp_ref, krs_ref, kr[:, :rope], np_tiles)
    krb_ref[...] = kr.astype(BF16)


def _mla_q_kernel(cq_ref, wa_ref, wb_ref, cos_ref, sin_ref, q_ref, *, n_heads, nope):
    cq = cq_ref[...]
    cos = cos_ref[...]
    sin = sin_ref[...]
    for hp in range(n_heads // 2):
        a = _dot(cq, wa_ref[:, hp * 2 * HEAD_PAD:(hp + 1) * 2 * HEAD_PAD])
        b = _dot(cq, wb_ref[:, hp * 2 * ROPE_PAD:(hp + 1) * 2 * ROPE_PAD])
        for s in range(2):
            c0 = (2 * hp + s) * HEAD_PAD
            q_ref[:, c0:c0 + nope] = a[:, s * HEAD_PAD:s * HEAD_PAD + nope].astype(BF16)
            ar = a[:, s * HEAD_PAD + nope:(s + 1) * HEAD_PAD]
            br = b[:, s * ROPE_PAD:(s + 1) * ROPE_PAD]
            q_ref[:, c0 + nope:c0 + HEAD_PAD] = (ar * cos + br * sin).astype(BF16)


def _mla_qt_kernel(cqt_ref, wat_ref, wbt_ref, cost_ref, sint_ref, qt_ref, *, n_heads, nope):
    cqt = cqt_ref[...]
    cost = cost_ref[...]
    sint = sint_ref[...]
    rope = cost.shape[0]
    hd = nope + rope
    zeros = jnp.zeros((HEAD_PAD - hd, cqt.shape[1]), BF16)
    for hp in range(n_heads // 2):
        a = _dot(wat_ref[hp * 2 * hd:(hp + 1) * 2 * hd, :], cqt)
        b = _dot(wbt_ref[hp * 2 * rope:(hp + 1) * 2 * rope, :], cqt)
        for s in range(2):
            r0 = (2 * hp + s) * HEAD_PAD
            qt_ref[r0:r0 + nope, :] = a[s * hd:s * hd + nope].astype(BF16)
            ar = a[s * hd + nope:(s + 1) * hd]
            br = b[s * rope:(s + 1) * rope]
            qt_ref[r0 + nope:r0 + hd, :] = (ar * cost + br * sint).astype(BF16)
            qt_ref[r0 + hd:r0 + HEAD_PAD, :] = zeros


def _mla_kv_kernel(ckvb_ref, ckvt_ref, krb_ref, wk_ref, wvt_ref, k_ref, vt_ref, *, n_heads, nope):
    c = ckvb_ref[...]
    krb = krb_ref[...]
    tm = c.shape[0]
    vt = _dot(wvt_ref[...], ckvt_ref[...]).astype(BF16)
    ones_row = (lax.broadcasted_iota(jnp.int32, (VT_PAD - nope, tm), 0) == 0).astype(BF16)
    for h in range(n_heads):
        vt_ref[h * VT_PAD:h * VT_PAD + nope, :] = vt[h * nope:(h + 1) * nope]
        vt_ref[h * VT_PAD + nope:(h + 1) * VT_PAD, :] = ones_row
    for hp in range(n_heads // 2):
        kn = _dot(c, wk_ref[:, hp * 2 * nope:(hp + 1) * 2 * nope])
        for s in range(2):
            c0 = (2 * hp + s) * HEAD_PAD
            k_ref[:, c0:c0 + nope] = kn[:, s * nope:(s + 1) * nope].astype(BF16)
            k_ref[:, c0 + nope:c0 + HEAD_PAD] = krb


def _flash_kernel(qt_ref, k_ref, vt_ref, o_ref, s0_ref, *, nq, tq, kb, kd, cw, v_head):
    qi = pl.program_id(2)
    ncol = tq // cw
    main = tuple((off, kb, 0) for off in range(0, tq, kb))
    diag = ((0, kb, 0),) + tuple((off, kd, off // cw) for off in range(kb, tq, kd))

    def scores(base, sub, c):
        off, size, _ = sub
        start = pl.multiple_of(base + off, size)
        return _dot(k_ref[pl.ds(start, size), :], qt_ref[:, c * cw:(c + 1) * cw])

    def update(st, sc, base, sub, c, masked):
        off, size, _ = sub
        start = pl.multiple_of(base + off, size)
        if masked and off + size > c * cw:
            kc = (lax.broadcasted_iota(jnp.int32, (size, cw), 0) + off) // CHUNK
            qc = (lax.broadcasted_iota(jnp.int32, (size, cw), 1) + c * cw) // CHUNK
            sc = jnp.where(kc <= qc, sc, -jnp.inf)
        m_prev, acc_prev = st
        m_new = jnp.maximum(m_prev, jnp.max(sc, axis=0, keepdims=True))
        p = jnp.exp2(sc - m_new).astype(BF16)
        return m_new, jnp.exp2(m_prev - m_new) * acc_prev + _dot(vt_ref[:, pl.ds(start, size)], p)

    def trip(base, state, subs, masked, stage_next):
        state = list(state)
        cur = [None] * ncol
        for t, sub in enumerate(subs):
            nxt = [None] * ncol
            for c in range(sub[2], ncol):
                if t + 1 < len(subs) and c >= subs[t + 1][2]:
                    nxt[c] = scores(base, subs[t + 1], c)
                sc = s0_ref[:, c * cw:(c + 1) * cw] if cur[c] is None else cur[c]
                state[c] = update(state[c], sc, base, sub, c, masked)
                if t + 1 == len(subs) and stage_next:
                    s0_ref[:, c * cw:(c + 1) * cw] = scores(base + tq, main[0], c)
            cur = nxt
        return tuple(state)

    for c in range(ncol):
        s0_ref[:, c * cw:(c + 1) * cw] = scores(0, main[0], c)
    init = tuple((jnp.full((1, cw), -jnp.inf, F32), jnp.zeros((vt_ref.shape[0], cw), F32)) for _ in range(ncol))
    state = init
    if nq > 1:
        state = lax.fori_loop(0, qi, lambda j, st: trip(j * tq, st, main, False, True), init)
    state = trip(qi * tq, state, diag, True, False)
    for c in range(ncol):
        acc = state[c][1]
        o_ref[c * cw:(c + 1) * cw, :] = (acc[:v_head] / acc[v_head:v_head + 1]).T.astype(BF16)


def _lane_fold(a, op):
    out = a[:, :128]
    for i in range(1, a.shape[1] // 128):
        out = op(out, a[:, i * 128:(i + 1) * 128])
    return out


def _sample_attn_kernel(q_ref, wukt_ref, wuv_ref, cc_ref, crt_ref, nc_ref, nr_ref, o_in_ref, o_ref,
                        ql_ref, qr_ref, cb_ref, s_ref, p_ref, *, n_heads, nope, rope, tk, t_new):
    del o_in_ref
    past = cc_ref.shape[0]
    for h in range(n_heads):
        c0 = h * HEAD_PAD
        ql_ref[h * t_new:(h + 1) * t_new, :] = _dot(q_ref[:, c0:c0 + nope], wukt_ref[h]).astype(BF16)
        qr_ref[h * t_new:(h + 1) * t_new, :] = q_ref[:, c0 + nope:c0 + nope + rope]
    ql = ql_ref[...]
    qr = qr_ref[...]
    nc = nc_ref[...]
    s_new = _dot_nt(ql, nc) + _dot_nt(qr, nr_ref[:, :rope])
    rows = ql.shape[0]
    nchunk = past // tk
    half = nchunk // 2
    ks = [slice(j * tk, (j + 1) * tk) for j in range(nchunk)]

    def score(j, m_lane):
        cb = cc_ref[ks[j], :].astype(BF16)
        cb_ref[ks[j], :] = cb
        s = _dot_nt(ql, cb) + _dot(qr, crt_ref[:, ks[j]].astype(BF16))
        s_ref[:, ks[j]] = s
        return jnp.maximum(m_lane, _lane_fold(s, jnp.maximum))

    def probs(j, m, l_lane):
        p = jnp.exp2(s_ref[:, ks[j]] - m)
        p_ref[:, ks[j]] = p.astype(BF16)
        return l_lane + _lane_fold(p, jnp.add)

    neg = jnp.full((rows, 128), -jnp.inf, F32)
    zero = jnp.zeros((rows, 128), F32)
    m_lane = neg
    for j in range(half):
        m_lane = score(j, m_lane)
    m_a = jnp.max(m_lane, axis=-1, keepdims=True)
    m_lane, l_a = neg, zero
    for i, j in enumerate(range(half, nchunk)):
        m_lane = score(j, m_lane)
        if i < half:
            l_a = probs(i, m_a, l_a)
    m_b = jnp.maximum(jnp.max(m_lane, axis=-1, keepdims=True), jnp.max(s_new, axis=-1, keepdims=True))
    m = jnp.maximum(m_a, m_b)
    acc_a, l_b = jnp.zeros((rows, cb_ref.shape[1]), F32), zero
    for i, j in enumerate(range(half, nchunk)):
        l_b = probs(j, m, l_b)
        if i < half:
            acc_a = acc_a + _dot(p_ref[:, ks[i]], cb_ref[ks[i], :])
    p_new = jnp.exp2(s_new - m)
    acc_b = _dot(p_new.astype(BF16), nc)
    for j in range(half, nchunk):
        acc_b = acc_b + _dot(p_ref[:, ks[j]], cb_ref[ks[j], :])
    alpha = jnp.exp2(m_a - m)
    l = (alpha * jnp.sum(l_a, axis=-1, keepdims=True) + jnp.sum(l_b, axis=-1, keepdims=True)
         + jnp.sum(p_new, axis=-1, keepdims=True))
    ol = ((alpha * acc_a + acc_b) / l).astype(BF16)
    for h in range(n_heads):
        o_ref[:, h * nope:(h + 1) * nope] = _dot(ol[h * t_new:(h + 1) * t_new, :], wuv_ref[h]).astype(BF16)


def _store_stream(x_new, gf_ref, o_ref, h_ref):
    o_ref[...] = x_new
    h_ref[...] = _rms(x_new, gf_ref[...]).astype(BF16)


def _proj_res_kernel(a_ref, w_ref, gf_ref, *refs, np_tiles):
    *x_refs, o_ref, h_ref = refs
    _store_stream(_load_rows(x_refs, np_tiles) + _dot(a_ref[...], w_ref[...]), gf_ref, o_ref, h_ref)


def _ffn_accumulate(x_ref, h_ref, w1_ref, w2_ref, acc_ref):
    @pl.when(pl.program_id(1) == 0)
    def _():
        acc_ref[...] = x_ref[...]

    a = jnp.maximum(_dot(h_ref[...], w1_ref[...]), 0.0)
    acc_ref[...] += _dot((a * a).astype(BF16), w2_ref[...])


def _ffn_kernel(x_ref, h_ref, w1_ref, w2_ref, o_ref):
    _ffn_accumulate(x_ref, h_ref, w1_ref, w2_ref, o_ref)


def _ffn_final_kernel(x_ref, h_ref, w1_ref, w2_ref, gf_ref, yp_ref, ys_ref, acc_ref, *, np_tiles):
    _ffn_accumulate(x_ref, h_ref, w1_ref, w2_ref, acc_ref)

    @pl.when(pl.program_id(1) == pl.num_programs(1) - 1)
    def _():
        _store_rows(yp_ref, ys_ref, _rms(acc_ref[...], gf_ref[...]), np_tiles)


def _conv_in_kernel(x_ref, g_ref, wb_ref, wc_ref, wv_ref, u_ref, gb_ref, h_ref):
    @pl.when(pl.program_id(1) == 0)
    def _():
        h_ref[...] = _rms(x_ref[...], g_ref[...]).astype(BF16)

    h = h_ref[...]
    gb_ref[...] = _dot(h, wb_ref[...]).astype(BF16)
    u_ref[...] = _dot(h, wc_ref[...]) * _dot(h, wv_ref[...])


def _conv_mix(u, p1, p2, first, second, gb, cw, wout, x):
    u1 = jnp.where(first, p1, pltpu.roll(u, 1, axis=0))
    u2 = jnp.where(first, p2, jnp.where(second, p1, pltpu.roll(u, 2, axis=0)))
    conv = cw[0:1] * u2 + cw[1:2] * u1 + cw[2:3] * u
    return x + _dot((gb.astype(F32) * conv).astype(BF16), wout)


def _conv_out_prompt_kernel(u_ref, halo_ref, gb_ref, x_ref, cw_ref, wout_ref, gf_ref, o_ref, h_ref):
    tm = u_ref.shape[0]
    has_past = pl.program_id(1) > 0
    halo = halo_ref[...]
    p1 = jnp.where(has_past, halo[7:8], 0.0)
    p2 = jnp.where(has_past, halo[6:7], 0.0)
    row = lax.broadcasted_iota(jnp.int32, (tm, 1), 0)
    x_new = _conv_mix(u_ref[...], p1, p2, row == 0, row == 1, gb_ref[...], cw_ref[...],
                      wout_ref[...], x_ref[...])
    _store_stream(x_new, gf_ref, o_ref, h_ref)


def _conv_out_sample_kernel(u_ref, st_ref, gb_ref, x_ref, cw_ref, wout_ref, gf_ref, xp_ref, hp_ref,
                            o_ref, h_ref, *, t_new):
    del xp_ref, hp_ref
    tm, d = u_ref.shape
    nb = tm // t_new
    st = st_ref[...]
    p1 = jnp.broadcast_to(st[:, 1:2, :], (nb, t_new, d)).reshape(tm, d)
    p2 = jnp.broadcast_to(st[:, 0:1, :], (nb, t_new, d)).reshape(tm, d)
    t = lax.broadcasted_iota(jnp.int32, (tm, 1), 0) % t_new
    x_new = _conv_mix(u_ref[...], p1, p2, t == 0, t == 1, gb_ref[...], cw_ref[...],
                      wout_ref[...], x_ref[...])
    _store_stream(x_new, gf_ref, o_ref, h_ref)


def _row_spec(tm, n, off=0):
    return pl.BlockSpec((tm, n), lambda i, *_: (i + off, 0))


def _col_spec(n, tm):
    return pl.BlockSpec((n, tm), lambda i, *_: (0, i))


def _split_row_specs(tm, n, np_tiles, layer=None):
    def p_map(i, *_):
        return (jnp.minimum(i, np_tiles - 1), 0)

    def s_map(i, *_):
        return (jnp.maximum(i - np_tiles, 0), 0)

    if layer is None:
        return [pl.BlockSpec((tm, n), p_map), pl.BlockSpec((tm, n), s_map)]
    return [pl.BlockSpec((None, tm, n), lambda i, *_: (layer,) + p_map(i)),
            pl.BlockSpec((None, tm, n), lambda i, *_: (layer,) + s_map(i))]


def _mla_layer(xs, j, prev, n_prompt, seq, t_new, tabs, cache_c, cache_rt, g_mix, w_dq, g_q, w_uq, w_dkv, g_kv,
               w_uk, w_uv, w_ob, g_ffn):
    cos, sin, cost, sint = tabs
    t_all, d = sum(a.shape[0] for a in xs), xs[0].shape[1]
    q_lora = w_dq.shape[1]
    kv_lora, n_heads, nope = w_uk.shape
    rope = w_dkv.shape[1] - kv_lora
    half = rope // 2
    q_scale = float((nope + rope) ** -0.5 * math.log2(math.e))
    n_sample = t_all - n_prompt

    wr = w_dkv[:, kv_lora:]
    zr = jnp.zeros((d, ROPE_PAD - rope), F32)
    w_down = jnp.concatenate(
        [w_dq, w_dkv[:, :kv_lora], wr, zr, -wr[:, half:], wr[:, :half], zr], axis=1).astype(BF16)
    wq = w_uq.reshape(q_lora, n_heads, nope + rope)
    wq_r = wq[..., nope:]
    w_qa = jnp.concatenate(
        [wq, jnp.zeros((q_lora, n_heads, HEAD_PAD - nope - rope), F32)], axis=-1
    ).reshape(q_lora, n_heads * HEAD_PAD).astype(BF16)
    w_qb = jnp.concatenate(
        [-wq_r[..., half:], wq_r[..., :half], jnp.zeros((q_lora, n_heads, ROPE_PAD - rope), F32)], axis=-1
    ).reshape(q_lora, n_heads * ROPE_PAD).astype(BF16)
    w_k = w_uk.reshape(kv_lora, n_heads * nope).astype(BF16)
    w_vt = w_uv.reshape(kv_lora, n_heads * nope).T.astype(BF16)
    w_ukt = jnp.transpose(w_uk, (1, 2, 0)).astype(BF16)
    w_uvh = jnp.transpose(w_uv, (1, 0, 2)).astype(BF16)

    tm = _tile(n_sample, 512)
    np_tiles = n_prompt // tm
    n_layers = cache_c.shape[0]
    n_down = q_lora + kv_lora + 2 * ROPE_PAD
    x_specs = _split_row_specs(tm, d, np_tiles) if len(xs) == 2 else [_row_spec(tm, d)]
    new_shapes = [(n_layers, n_prompt, kv_lora), (n_layers, n_sample, kv_lora),
                  (n_layers, n_prompt, rope), (n_layers, n_sample, rope)]
    n_alias = 0 if prev is None else len(prev)
    n_in = len(xs) + 6
    cq, cqt, ckvb, ckvt, krb, *new = pl.pallas_call(
        functools.partial(_mla_down_kernel, n_x=len(xs), n_alias=n_alias, np_tiles=np_tiles,
                          q_lora=q_lora, kv_lora=kv_lora, rope=rope, q_scale=q_scale),
        grid=(t_all // tm,),
        in_specs=x_specs + [_resident((1, d)), _resident((d, n_down)), _resident((1, q_lora)),
                            _resident((1, kv_lora)), _row_spec(tm, ROPE_PAD), _row_spec(tm, ROPE_PAD)]
        + [pl.BlockSpec(memory_space=pl.ANY)] * n_alias,
        out_specs=[_row_spec(tm, q_lora), _col_spec(q_lora, tm), _row_spec(tm, kv_lora),
                   _col_spec(kv_lora, tm), _row_spec(tm, ROPE_PAD)]
        + _split_row_specs(tm, kv_lora, np_tiles, j) + _split_row_specs(tm, rope, np_tiles, j),
        out_shape=[jax.ShapeDtypeStruct((t_all, q_lora), BF16), jax.ShapeDtypeStruct((q_lora, t_all), BF16),
                   jax.ShapeDtypeStruct((t_all, kv_lora), BF16), jax.ShapeDtypeStruct((kv_lora, t_all), BF16),
                   jax.ShapeDtypeStruct((t_all, ROPE_PAD), BF16)]
        + [jax.ShapeDtypeStruct(s, F32) for s in new_shapes],
        input_output_aliases={n_in + a: 5 + a for a in range(n_alias)},
        compiler_params=_params("arbitrary"), name="mla_down",
    )(*xs, g_mix.reshape(1, d), w_down, g_q.reshape(1, q_lora), g_kv.reshape(1, kv_lora), cos, sin,
      *(prev or ()))

    tp = _tile(n_prompt, 512)
    w_qat = w_uq.T.astype(BF16)
    w_qbt = jnp.concatenate([-wq_r[..., half:], wq_r[..., :half]], axis=-1
                            ).reshape(q_lora, n_heads * rope).T.astype(BF16)
    qt = pl.pallas_call(
        functools.partial(_mla_qt_kernel, n_heads=n_heads, nope=nope),
        grid=(n_prompt // tp,),
        in_specs=[_col_spec(q_lora, tp), _resident(w_qat.shape), _resident(w_qbt.shape),
                  _col_spec(rope, tp), _col_spec(rope, tp)],
        out_specs=_col_spec(n_heads * HEAD_PAD, tp),
        out_shape=jax.ShapeDtypeStruct((n_heads * HEAD_PAD, n_prompt), BF16),
        compiler_params=_params("parallel"), name="mla_qt",
    )(cqt, w_qat, w_qbt, cost, sint)

    ts = _tile(n_sample, 512)
    off = n_prompt // ts
    q_s = pl.pallas_call(
        functools.partial(_mla_q_kernel, n_heads=n_heads, nope=nope),
        grid=(n_sample // ts,),
        in_specs=[_row_spec(ts, q_lora, off), _resident(w_qa.shape), _resident(w_qb.shape),
                  _row_spec(ts, ROPE_PAD, off), _row_spec(ts, ROPE_PAD, off)],
        out_specs=_row_spec(ts, n_heads * HEAD_PAD),
        out_shape=jax.ShapeDtypeStruct((n_sample, n_heads * HEAD_PAD), BF16),
        compiler_params=_params("parallel"), name="mla_q",
    )(cq, w_qa, w_qb, cos, sin)

    k, vt = pl.pallas_call(
        functools.partial(_mla_kv_kernel, n_heads=n_heads, nope=nope),
        grid=(n_prompt // tp,),
        in_specs=[_row_spec(tp, kv_lora), _col_spec(kv_lora, tp), _row_spec(tp, ROPE_PAD),
                  _resident(w_k.shape), _resident(w_vt.shape)],
        out_specs=[_row_spec(tp, n_heads * HEAD_PAD), _col_spec(n_heads * VT_PAD, tp)],
        out_shape=[jax.ShapeDtypeStruct((n_prompt, n_heads * HEAD_PAD), BF16),
                   jax.ShapeDtypeStruct((n_heads * VT_PAD, n_prompt), BF16)],
        compiler_params=_params("parallel"), name="mla_kv",
    )(ckvb, ckvt, krb, w_k, w_vt)

    blk = _tile(seq, 4096)
    kb = _tile(blk, 256)
    nq = seq // blk
    n_batch = n_prompt // seq
    o = pl.pallas_call(
        functools.partial(_flash_kernel, nq=nq, tq=blk, kb=kb, kd=_tile(kb, 256), cw=_tile(blk, 256),
                          v_head=nope),
        grid=(n_batch, n_heads, nq),
        in_specs=[pl.BlockSpec((HEAD_PAD, blk), lambda b, h, i: (h, b * nq + i)),
                  pl.BlockSpec((seq, HEAD_PAD), lambda b, h, i: (b, h)),
                  pl.BlockSpec((VT_PAD, seq), lambda b, h, i: (h, b))],
        out_specs=pl.BlockSpec((blk, nope), lambda b, h, i: (b * nq + i, h)),
        out_shape=jax.ShapeDtypeStruct((t_all, n_heads * nope), BF16),
        scratch_shapes=[pltpu.VMEM((kb, blk), F32)],
        compiler_params=_params("parallel", "parallel", "arbitrary"), name="mla_flash",
    )(qt, k, vt)

    past = cache_c.shape[2]
    n_sb = n_sample // t_new
    row0 = n_prompt // t_new
    tk = _tile(past, 512)
    o = pl.pallas_call(
        functools.partial(_sample_attn_kernel, n_heads=n_heads, nope=nope, rope=rope, tk=tk, t_new=t_new),
        grid=(n_sb,),
        in_specs=[pl.BlockSpec((t_new, n_heads * HEAD_PAD), lambda b: (b, 0)),
                  _resident(w_ukt.shape), _resident(w_uvh.shape),
                  pl.BlockSpec((None, None, past, kv_lora), lambda b: (j, b, 0, 0)),
                  pl.BlockSpec((None, None, rope, past), lambda b: (j, b, 0, 0)),
                  pl.BlockSpec((t_new, kv_lora), lambda b: (row0 + b, 0)),
                  pl.BlockSpec((t_new, ROPE_PAD), lambda b: (row0 + b, 0)),
                  pl.BlockSpec(memory_space=pl.ANY)],
        out_specs=pl.BlockSpec((t_new, n_heads * nope), lambda b: (row0 + b, 0)),
        out_shape=jax.ShapeDtypeStruct((t_all, n_heads * nope), BF16),
        input_output_aliases={7: 0},
        scratch_shapes=[pltpu.VMEM((n_heads * t_new, kv_lora), BF16), pltpu.VMEM((n_heads * t_new, rope), BF16),
                        pltpu.VMEM((past, kv_lora), BF16), pltpu.VMEM((n_heads * t_new, past), F32),
                        pltpu.VMEM((n_heads * t_new, past), BF16)],
        compiler_params=_params("parallel"), name="mla_sample_attn",
    )(q_s, w_ukt, w_uvh, cache_c, cache_rt, ckvb, krb, o)

    n_o = n_heads * nope
    x, h = pl.pallas_call(
        functools.partial(_proj_res_kernel, np_tiles=np_tiles),
        grid=(t_all // tm,),
        in_specs=[_row_spec(tm, n_o), pl.BlockSpec((None, n_o, d), lambda i: (j, 0, 0),
                                                   pipeline_mode=pl.Buffered(1)), _resident((1, d))] + x_specs,
        out_specs=[_row_spec(tm, d), _row_spec(tm, d)],
        out_shape=[jax.ShapeDtypeStruct((t_all, d), F32), jax.ShapeDtypeStruct((t_all, d), BF16)],
        compiler_params=_params("parallel"), name="mla_out",
    )(o, w_ob, g_ffn.reshape(1, d), *xs)
    return x, h, new


def _conv_layer(x, j, n_prompt, seq, t_new, state, g_mix, w_inb, conv_w, w_outb, g_ffn):
    assert conv_w.shape[0] == 3 and state.shape[1] == 2
    t_all, d = x.shape
    tm = _tile(t_all, 512)
    tn = _tile(d, 1024)
    nn = d // tn
    w_out_spec = pl.BlockSpec((None, d, d), lambda *_: (j, 0, 0), pipeline_mode=pl.Buffered(1))
    u, gb = pl.pallas_call(
        _conv_in_kernel,
        grid=(t_all // tm, nn),
        in_specs=[pl.BlockSpec((tm, d), lambda i, n: (i, 0)), _resident((1, d)),
                  pl.BlockSpec((None, d, tn), lambda i, n: (j, 0, n)),
                  pl.BlockSpec((None, d, tn), lambda i, n: (j, 0, nn + n)),
                  pl.BlockSpec((None, d, tn), lambda i, n: (j, 0, 2 * nn + n))],
        out_specs=[pl.BlockSpec((tm, tn), lambda i, n: (i, n)), pl.BlockSpec((tm, tn), lambda i, n: (i, n))],
        out_shape=[jax.ShapeDtypeStruct((t_all, d), F32), jax.ShapeDtypeStruct((t_all, d), BF16)],
        scratch_shapes=[pltpu.VMEM((tm, d), BF16)],
        compiler_params=_params("parallel", "arbitrary"), name="conv_in",
    )(x, g_mix.reshape(1, d), w_inb, w_inb, w_inb)

    tp = _tile(seq, 512)
    ns = seq // tp
    n_batch = n_prompt // seq
    hb = tp // 8
    tile_spec = pl.BlockSpec((tp, d), lambda b, i: (b * ns + i, 0))
    stream_shapes = [jax.ShapeDtypeStruct((t_all, d), F32), jax.ShapeDtypeStruct((t_all, d), BF16)]
    gf = g_ffn.reshape(1, d)
    x_p, h_p = pl.pallas_call(
        _conv_out_prompt_kernel,
        grid=(n_batch, ns),
        in_specs=[tile_spec,
                  pl.BlockSpec((8, d), lambda b, i: (jnp.maximum((b * ns + i) * hb - 1, 0), 0)),
                  tile_spec, tile_spec, _resident((conv_w.shape[0], d)), w_out_spec, _resident((1, d))],
        out_specs=[tile_spec, tile_spec],
        out_shape=stream_shapes,
        compiler_params=_params("parallel", "arbitrary"), name="conv_out_prompt",
    )(u, u, gb, x, conv_w, w_outb, gf)

    n_sample = t_all - n_prompt
    ts = _tile(n_sample, 512)
    off = n_prompt // ts
    nb = ts // t_new
    any_spec = pl.BlockSpec(memory_space=pl.ANY)
    x_new, h_new = pl.pallas_call(
        functools.partial(_conv_out_sample_kernel, t_new=t_new),
        grid=(n_sample // ts,),
        in_specs=[_row_spec(ts, d, off), pl.BlockSpec((nb, state.shape[1], d), lambda i: (i, 0, 0)),
                  _row_spec(ts, d, off), _row_spec(ts, d, off),
                  _resident((conv_w.shape[0], d)), w_out_spec, _resident((1, d)), any_spec, any_spec],
        out_specs=[_row_spec(ts, d, off), _row_spec(ts, d, off)],
        out_shape=stream_shapes,
        input_output_aliases={7: 0, 8: 1},
        compiler_params=_params("parallel"), name="conv_out_sample",
    )(u, state, gb, x, conv_w, w_outb, gf, x_p, h_p)
    return x_new, h_new, u


def _ffn_specs(t_all, d, f, layer):
    tm = _tile(t_all, 512)
    tf = _tile(f, 1024)
    row = pl.BlockSpec((tm, d), lambda i, c: (i, 0))
    specs = [row, row, pl.BlockSpec((None, d, tf), lambda i, c: (layer, 0, c)),
             pl.BlockSpec((None, tf, d), lambda i, c: (layer, c, 0))]
    return tm, (t_all // tm, f // tf), specs, row


def _ffn_layer(x, h, layer, w1b, w2b):
    t_all, d = x.shape
    _, grid, specs, row = _ffn_specs(t_all, d, w1b.shape[2], layer)
    return pl.pallas_call(
        _ffn_kernel, grid=grid, in_specs=specs, out_specs=row,
        out_shape=jax.ShapeDtypeStruct((t_all, d), F32),
        compiler_params=_params("parallel", "arbitrary"), name="ffn",
    )(x, h, w1b, w2b)


def _ffn_final_layer(x, h, layer, w1b, w2b, g_final, n_prompt):
    t_all, d = x.shape
    tm, grid, specs, _ = _ffn_specs(t_all, d, w1b.shape[2], layer)
    np_tiles = n_prompt // tm
    return pl.pallas_call(
        functools.partial(_ffn_final_kernel, np_tiles=np_tiles),
        grid=grid, in_specs=specs + [_resident((1, d))],
        out_specs=_split_row_specs(tm, d, np_tiles),
        out_shape=[jax.ShapeDtypeStruct((n_prompt, d), F32), jax.ShapeDtypeStruct((t_all - n_prompt, d), F32)],
        scratch_shapes=[pltpu.VMEM((tm, d), F32)],
        compiler_params=_params("arbitrary", "arbitrary"), name="ffn_final",
    )(x, h, w1b, w2b, g_final.reshape(1, d))


def kernel(x_prompt, x_sample, cache_ckv, cache_krope, state_conv, norm_mix, w_dq, g_q, w_uq, w_dkv, g_kv,
           w_uk, w_uv, w_o, w_in, conv_w, w_out, norm_ffn, w_ff1, w_ff2, norm_final):
    n_batch, seq, d = x_prompt.shape
    s_batch, t_new, _ = x_sample.shape
    past = cache_ckv.shape[2]
    depth = norm_mix.shape[0]
    kv_lora = w_uk.shape[1]
    rope = w_dkv.shape[2] - kv_lora
    half = rope // 2
    n_prompt = n_batch * seq
    n_sample = s_batch * t_new

    pos = jnp.concatenate([jnp.tile(jnp.arange(seq, dtype=jnp.int32), n_batch),
                           jnp.tile(past + jnp.arange(t_new, dtype=jnp.int32), s_batch)])
    inv = ROPE_THETA ** (-jnp.arange(0, half, dtype=F32) / half)
    ang = pos.astype(F32)[:, None] * inv[None, :]
    zpad = jnp.zeros((pos.shape[0], ROPE_PAD - rope), F32)
    cos = jnp.concatenate([jnp.cos(ang), jnp.cos(ang), zpad], axis=1)
    sin = jnp.concatenate([jnp.sin(ang), jnp.sin(ang), zpad], axis=1)
    tabs = (cos, sin, cos[:n_prompt, :rope].T, sin[:n_prompt, :rope].T)

    cache_rt = jnp.swapaxes(cache_krope, 2, 3)
    w1b = w_ff1.astype(BF16)
    w2b = w_ff2.astype(BF16)
    w_ob = w_o.astype(BF16)
    w_inb = w_in.astype(BF16)
    w_outb = w_out.astype(BF16)

    xs = (x_prompt.reshape(n_prompt, d), x_sample.reshape(n_sample, d))
    latents, new_u = None, []
    for i in range(depth):
        j = i // 2
        if i % 2 == 0:
            x, h, latents = _mla_layer(xs, j, latents, n_prompt, seq, t_new, tabs, cache_ckv, cache_rt,
                                       norm_mix[i], w_dq[j], g_q[j], w_uq[j], w_dkv[j], g_kv[j],
                                       w_uk[j], w_uv[j], w_ob, norm_ffn[i])
        else:
            x, h, u = _conv_layer(xs[0], j, n_prompt, seq, t_new, state_conv[j], norm_mix[i], w_inb, conv_w[j],
                                  w_outb, norm_ffn[i])
            new_u.append(u)
        if i + 1 < depth:
            xs = (_ffn_layer(x, h, i, w1b, w2b),)
        else:
            y_prompt, y_sample = _ffn_final_layer(x, h, i, w1b, w2b, norm_final, n_prompt)

    def tails(u, lo, lead, inner, keep):
        return jnp.stack([lax.slice_in_dim(u, lo + (b + 1) * inner - keep, lo + (b + 1) * inner, axis=0)
                          for b in range(lead)])

    keep = conv_w.shape[1] - 1
    ckv_p, ckv_s, kr_p, kr_s = latents
    n_a = ckv_p.shape[0]
    u_p = jnp.stack([tails(u, 0, n_batch, seq, keep) for u in new_u])
    u_s = jnp.stack([tails(u, n_prompt, s_batch, t_new, keep) for u in new_u])
    return (y_prompt.reshape(n_batch, seq, d), y_sample.reshape(s_batch, t_new, d),
            ckv_p.reshape(n_a, n_batch, seq, -1), kr_p.reshape(n_a, n_batch, seq, -1), u_p,
            ckv_s.reshape(n_a, s_batch, t_new, -1), kr_s.reshape(n_a, s_batch, t_new, -1), u_s)
```

```python
import functools
import math

import jax
import jax.numpy as jnp
from jax import lax
from jax.experimental import pallas as pl
from jax.experimental.pallas import tpu as pltpu

CHUNK = 64
ROPE_THETA = 10000.0
EPS = 1e-6
HEAD_PAD = 256
ROPE_PAD = 128
VT_PAD = 144
VMEM_LIMIT = 56 * 1024 * 1024

F32 = jnp.float32
BF16 = jnp.bfloat16


def _params(*sem):
    return pltpu.CompilerParams(dimension_semantics=sem, vmem_limit_bytes=VMEM_LIMIT)


def _tile(total, pref):
    t = min(total, pref)
    assert total % t == 0, (total, pref)
    return t


def _resident(shape):
    nd = len(shape)
    return pl.BlockSpec(shape, lambda *_: (0,) * nd, pipeline_mode=pl.Buffered(1))


def _rms(xf, g):
    return xf * lax.rsqrt(jnp.mean(xf * xf, axis=-1, keepdims=True) + EPS) * g


def _dot(a, b):
    return jnp.dot(a, b, preferred_element_type=F32)


def _dot_nt(a, b):
    return lax.dot_general(a, b, (((1,), (1,)), ((), ())), preferred_element_type=F32)


def _load_rows(x_refs, np_tiles):
    if len(x_refs) == 1:
        return x_refs[0][...]
    return jnp.where(pl.program_id(0) < np_tiles, x_refs[0][...], x_refs[1][...])


def _store_rows(p_ref, s_ref, val, np_tiles, prev=None):
    is_prompt = pl.program_id(0) < np_tiles

    def put(o_ref, prev_ref):
        if prev is None:
            o_ref[...] = val
        else:
            if prev_ref is not None:
                o_ref[:-1] = prev_ref[...]
            o_ref[o_ref.shape[0] - 1] = val

    @pl.when(is_prompt)
    def _():
        put(p_ref, prev and prev[0])

    @pl.when(jnp.logical_not(is_prompt))
    def _():
        put(s_ref, prev and prev[1])


def _mla_down_kernel(*refs, n_x, n_prev, np_tiles, q_lora, kv_lora, rope, q_scale):
    x_refs, refs = refs[:n_x], refs[n_x:]
    g_ref, w_ref, gq_ref, gkv_ref, cos_ref, sin_ref = refs[:6]
    prev = refs[6:6 + n_prev] if n_prev else (None,) * 4
    cq_ref, cqt_ref, ckvb_ref, ckvt_ref, krb_ref, ckvp_ref, ckvs_ref, krp_ref, krs_ref = refs[6 + n_prev:]
    h = _rms(_load_rows(x_refs, np_tiles), g_ref[...]).astype(BF16)
    y = _dot(h, w_ref[...])
    cq = _rms(y[:, :q_lora], gq_ref[...]) * q_scale
    cq_ref[...] = cq.astype(BF16)
    cqt_ref[...] = cq.T.astype(BF16)
    ckv = _rms(y[:, q_lora:q_lora + kv_lora], gkv_ref[...])
    _store_rows(ckvp_ref, ckvs_ref, ckv, np_tiles, prev[0:2])
    ckvb_ref[...] = ckv.astype(BF16)
    ckvt_ref[...] = ckv.T.astype(BF16)
    o = q_lora + kv_lora
    kr = y[:, o:o + ROPE_PAD] * cos_ref[...] + y[:, o + ROPE_PAD:o + 2 * ROPE_PAD] * sin_ref[...]
    _store_rows(krp_ref, krs_ref, kr[:, :rope], np_tiles, prev[2:4])
    krb_ref[...] = kr.astype(BF16)


def _mla_q_kernel(cq_ref, wa_ref, wb_ref, cos_ref, sin_ref, q_ref, *, n_heads, nope):
    cq = cq_ref[...]
    cos = cos_ref[...]
    sin = sin_ref[...]
    for hp in range(n_heads // 2):
        a = _dot(cq, wa_ref[:, hp * 2 * HEAD_PAD:(hp + 1) * 2 * HEAD_PAD])
        b = _dot(cq, wb_ref[:, hp * 2 * ROPE_PAD:(hp + 1) * 2 * ROPE_PAD])
        for s in range(2):
            c0 = (2 * hp + s) * HEAD_PAD
            q_ref[:, c0:c0 + nope] = a[:, s * HEAD_PAD:s * HEAD_PAD + nope].astype(BF16)
            ar = a[:, s * HEAD_PAD + nope:(s + 1) * HEAD_PAD]
            br = b[:, s * ROPE_PAD:(s + 1) * ROPE_PAD]
            q_ref[:, c0 + nope:c0 + HEAD_PAD] = (ar * cos + br * sin).astype(BF16)


def _mla_qt_kernel(cqt_ref, wat_ref, wbt_ref, cost_ref, sint_ref, qt_ref, *, n_heads, nope):
    cqt = cqt_ref[...]
    cost = cost_ref[...]
    sint = sint_ref[...]
    rope = cost.shape[0]
    hd = nope + rope
    zeros = jnp.zeros((HEAD_PAD - hd, cqt.shape[1]), BF16)
    for hp in range(n_heads // 2):
        a = _dot(wat_ref[hp * 2 * hd:(hp + 1) * 2 * hd, :], cqt)
        b = _dot(wbt_ref[hp * 2 * rope:(hp + 1) * 2 * rope, :], cqt)
        for s in range(2):
            r0 = (2 * hp + s) * HEAD_PAD
            qt_ref[r0:r0 + nope, :] = a[s * hd:s * hd + nope].astype(BF16)
            ar = a[s * hd + nope:(s + 1) * hd]
            br = b[s * rope:(s + 1) * rope]
            qt_ref[r0 + nope:r0 + hd, :] = (ar * cost + br * sint).astype(BF16)
            qt_ref[r0 + hd:r0 + HEAD_PAD, :] = zeros


def _mla_kv_kernel(ckvb_ref, ckvt_ref, krb_ref, wk_ref, wvt_ref, k_ref, vt_ref, *, n_heads, nope):
    c = ckvb_ref[...]
    krb = krb_ref[...]
    tm = c.shape[0]
    vt = _dot(wvt_ref[...], ckvt_ref[...]).astype(BF16)
    ones_row = (lax.broadcasted_iota(jnp.int32, (VT_PAD - nope, tm), 0) == 0).astype(BF16)
    for h in range(n_heads):
        vt_ref[h * VT_PAD:h * VT_PAD + nope, :] = vt[h * nope:(h + 1) * nope]
        vt_ref[h * VT_PAD + nope:(h + 1) * VT_PAD, :] = ones_row
    for hp in range(n_heads // 2):
        kn = _dot(c, wk_ref[:, hp * 2 * nope:(hp + 1) * 2 * nope])
        for s in range(2):
            c0 = (2 * hp + s) * HEAD_PAD
            k_ref[:, c0:c0 + nope] = kn[:, s * nope:(s + 1) * nope].astype(BF16)
            k_ref[:, c0 + nope:c0 + HEAD_PAD] = krb


def _flash_kernel(qt_ref, k_ref, vt_ref, o_ref, s0_ref, *, nq, tq, kb, kd, cw, v_head):
    qi = pl.program_id(2)
    ncol = tq // cw
    main = tuple((off, kb, 0) for off in range(0, tq, kb))
    diag = ((0, kb, 0),) + tuple((off, kd, off // cw) for off in range(kb, tq, kd))

    def scores(base, sub, c):
        off, size, _ = sub
        start = pl.multiple_of(base + off, size)
        return _dot(k_ref[pl.ds(start, size), :], qt_ref[:, c * cw:(c + 1) * cw])

    def update(st, sc, base, sub, c, masked):
        off, size, _ = sub
        start = pl.multiple_of(base + off, size)
        if masked and off + size > c * cw:
            kc = (lax.broadcasted_iota(jnp.int32, (size, cw), 0) + off) // CHUNK
            qc = (lax.broadcasted_iota(jnp.int32, (size, cw), 1) + c * cw) // CHUNK
            sc = jnp.where(kc <= qc, sc, -jnp.inf)
        m_prev, acc_prev = st
        m_new = jnp.maximum(m_prev, jnp.max(sc, axis=0, keepdims=True))
        p = jnp.exp2(sc - m_new).astype(BF16)
        return m_new, jnp.exp2(m_prev - m_new) * acc_prev + _dot(vt_ref[:, pl.ds(start, size)], p)

    def trip(base, state, subs, masked, stage_next):
        state = list(state)
        cur = [None] * ncol
        for t, sub in enumerate(subs):
            nxt = [None] * ncol
            for c in range(sub[2], ncol):
                if t + 1 < len(subs) and c >= subs[t + 1][2]:
                    nxt[c] = scores(base, subs[t + 1], c)
                sc = s0_ref[:, c * cw:(c + 1) * cw] if cur[c] is None else cur[c]
                state[c] = update(state[c], sc, base, sub, c, masked)
                if t + 1 == len(subs) and stage_next:
                    s0_ref[:, c * cw:(c + 1) * cw] = scores(base + tq, main[0], c)
            cur = nxt
        return tuple(state)

    for c in range(ncol):
        s0_ref[:, c * cw:(c + 1) * cw] = scores(0, main[0], c)
    init = tuple((jnp.full((1, cw), -jnp.inf, F32), jnp.zeros((vt_ref.shape[0], cw), F32)) for _ in range(ncol))
    state = init
    if nq > 1:
        state = lax.fori_loop(0, qi, lambda j, st: trip(j * tq, st, main, False, True), init)
    state = trip(qi * tq, state, diag, True, False)
    for c in range(ncol):
        acc = state[c][1]
        o_ref[c * cw:(c + 1) * cw, :] = (acc[:v_head] / acc[v_head:v_head + 1]).T.astype(BF16)


def _lane_fold(a, op):
    out = a[:, :128]
    for i in range(1, a.shape[1] // 128):
        out = op(out, a[:, i * 128:(i + 1) * 128])
    return out


def _sample_attn_kernel(q_ref, wukt_ref, wuv_ref, cc_ref, crt_ref, nc_ref, nr_ref, o_ref,
                        ql_ref, qr_ref, cb_ref, s_ref, p_ref, *, n_heads, nope, rope, tk, t_new):
    past = cc_ref.shape[0]
    for h in range(n_heads):
        c0 = h * HEAD_PAD
        ql_ref[h * t_new:(h + 1) * t_new, :] = _dot(q_ref[:, c0:c0 + nope], wukt_ref[h]).astype(BF16)
        qr_ref[h * t_new:(h + 1) * t_new, :] = q_ref[:, c0 + nope:c0 + nope + rope]
    ql = ql_ref[...]
    qr = qr_ref[...]
    nc = nc_ref[...]
    s_new = _dot_nt(ql, nc) + _dot_nt(qr, nr_ref[:, :rope])
    rows = ql.shape[0]
    nchunk = past // tk
    half = nchunk // 2
    ks = [slice(j * tk, (j + 1) * tk) for j in range(nchunk)]

    def score(j, m_lane):
        cb = cc_ref[ks[j], :].astype(BF16)
        cb_ref[ks[j], :] = cb
        s = _dot_nt(ql, cb) + _dot(qr, crt_ref[:, ks[j]].astype(BF16))
        s_ref[:, ks[j]] = s
        return jnp.maximum(m_lane, _lane_fold(s, jnp.maximum))

    def probs(j, m, l_lane):
        p = jnp.exp2(s_ref[:, ks[j]] - m)
        p_ref[:, ks[j]] = p.astype(BF16)
        return l_lane + _lane_fold(p, jnp.add)

    neg = jnp.full((rows, 128), -jnp.inf, F32)
    zero = jnp.zeros((rows, 128), F32)
    m_lane = neg
    for j in range(half):
        m_lane = score(j, m_lane)
    m_a = jnp.max(m_lane, axis=-1, keepdims=True)
    m_lane, l_a = neg, zero
    for i, j in enumerate(range(half, nchunk)):
        m_lane = score(j, m_lane)
        if i < half:
            l_a = probs(i, m_a, l_a)
    m_b = jnp.maximum(jnp.max(m_lane, axis=-1, keepdims=True), jnp.max(s_new, axis=-1, keepdims=True))
    m = jnp.maximum(m_a, m_b)
    acc_a, l_b = jnp.zeros((rows, cb_ref.shape[1]), F32), zero
    for i, j in enumerate(range(half, nchunk)):
        l_b = probs(j, m, l_b)
        if i < half:
            acc_a = acc_a + _dot(p_ref[:, ks[i]], cb_ref[ks[i], :])
    p_new = jnp.exp2(s_new - m)
    acc_b = _dot(p_new.astype(BF16), nc)
    for j in range(half, nchunk):
        acc_b = acc_b + _dot(p_ref[:, ks[j]], cb_ref[ks[j], :])
    alpha = jnp.exp2(m_a - m)
    l = (alpha * jnp.sum(l_a, axis=-1, keepdims=True) + jnp.sum(l_b, axis=-1, keepdims=True)
         + jnp.sum(p_new, axis=-1, keepdims=True))
    ol = ((alpha * acc_a + acc_b) / l).astype(BF16)
    for h in range(n_heads):
        o_ref[:, h * nope:(h + 1) * nope] = _dot(ol[h * t_new:(h + 1) * t_new, :], wuv_ref[h]).astype(BF16)


def _proj_res_kernel(w_ref, *refs, n_a, n_x, np_tiles):
    a_refs, x_refs, o_ref = refs[:n_a], refs[n_a:n_a + n_x], refs[n_a + n_x]
    o_ref[...] = _load_rows(x_refs, np_tiles) + _dot(_load_rows(a_refs, np_tiles), w_ref[...])


def _ffn_accumulate(x_refs, g_ref, w1_ref, w2_ref, acc_ref, h_ref, np_tiles):
    @pl.when(pl.program_id(1) == 0)
    def _():
        x = _load_rows(x_refs, np_tiles)
        h_ref[...] = _rms(x, g_ref[...]).astype(BF16)
        acc_ref[...] = x

    a = jnp.maximum(_dot(h_ref[...], w1_ref[...]), 0.0)
    acc_ref[...] += _dot((a * a).astype(BF16), w2_ref[...])


def _ffn_kernel(*refs, n_x, np_tiles):
    x_refs, (g_ref, w1_ref, w2_ref, o_ref, h_ref) = refs[:n_x], refs[n_x:]
    _ffn_accumulate(x_refs, g_ref, w1_ref, w2_ref, o_ref, h_ref, np_tiles)


def _ffn_final_kernel(*refs, n_x, np_tiles):
    x_refs, (g_ref, w1_ref, w2_ref, gf_ref, yp_ref, ys_ref, h_ref, acc_ref) = refs[:n_x], refs[n_x:]
    _ffn_accumulate(x_refs, g_ref, w1_ref, w2_ref, acc_ref, h_ref, np_tiles)

    @pl.when(pl.program_id(1) == pl.num_programs(1) - 1)
    def _():
        _store_rows(yp_ref, ys_ref, _rms(acc_ref[...], gf_ref[...]), np_tiles)


def _conv_in_kernel(x_ref, g_ref, wb_ref, wc_ref, wv_ref, u_ref, gb_ref, h_ref):
    @pl.when(pl.program_id(1) == 0)
    def _():
        h_ref[...] = _rms(x_ref[...], g_ref[...]).astype(BF16)

    h = h_ref[...]
    gb_ref[...] = _dot(h, wb_ref[...]).astype(BF16)
    u_ref[...] = _dot(h, wc_ref[...]) * _dot(h, wv_ref[...])


def _conv_mix(u, p1, p2, first, second, gb, cw, wout, x):
    u1 = jnp.where(first, p1, pltpu.roll(u, 1, axis=0))
    u2 = jnp.where(first, p2, jnp.where(second, p1, pltpu.roll(u, 2, axis=0)))
    conv = cw[0:1] * u2 + cw[1:2] * u1 + cw[2:3] * u
    return x + _dot((gb.astype(F32) * conv).astype(BF16), wout)


def _conv_out_prompt_kernel(u_ref, halo_ref, gb_ref, x_ref, cw_ref, wout_ref, o_ref):
    tm = u_ref.shape[0]
    has_past = pl.program_id(1) > 0
    halo = halo_ref[...]
    p1 = jnp.where(has_past, halo[7:8], 0.0)
    p2 = jnp.where(has_past, halo[6:7], 0.0)
    row = lax.broadcasted_iota(jnp.int32, (tm, 1), 0)
    o_ref[...] = _conv_mix(u_ref[...], p1, p2, row == 0, row == 1, gb_ref[...], cw_ref[...],
                           wout_ref[...], x_ref[...])


def _conv_out_sample_kernel(u_ref, st_ref, gb_ref, x_ref, cw_ref, wout_ref, o_ref, *, t_new):
    tm, d = u_ref.shape
    nb = tm // t_new
    st = st_ref[...]
    p1 = jnp.broadcast_to(st[:, 1:2, :], (nb, t_new, d)).reshape(tm, d)
    p2 = jnp.broadcast_to(st[:, 0:1, :], (nb, t_new, d)).reshape(tm, d)
    t = lax.broadcasted_iota(jnp.int32, (tm, 1), 0) % t_new
    o_ref[...] = _conv_mix(u_ref[...], p1, p2, t == 0, t == 1, gb_ref[...], cw_ref[...],
                           wout_ref[...], x_ref[...])


def _row_spec(tm, n, off=0):
    return pl.BlockSpec((tm, n), lambda i, *_: (i + off, 0))


def _col_spec(n, tm):
    return pl.BlockSpec((n, tm), lambda i, *_: (0, i))


def _split_row_specs(tm, n, np_tiles, layers=None):
    def p_map(i, *_):
        return (jnp.minimum(i, np_tiles - 1), 0)

    def s_map(i, *_):
        return (jnp.maximum(i - np_tiles, 0), 0)

    if layers is None:
        return [pl.BlockSpec((tm, n), p_map), pl.BlockSpec((tm, n), s_map)]
    return [pl.BlockSpec((layers, tm, n), lambda i, *_: (0,) + p_map(i)),
            pl.BlockSpec((layers, tm, n), lambda i, *_: (0,) + s_map(i))]


def _mla_layer(xs, j, prev, n_prompt, seq, t_new, tabs, cache_c, cache_rt, g_mix, w_dq, g_q, w_uq, w_dkv, g_kv,
               w_uk, w_uv, w_ob):
    cos, sin, cost, sint = tabs
    t_all, d = sum(a.shape[0] for a in xs), xs[0].shape[1]
    q_lora = w_dq.shape[1]
    kv_lora, n_heads, nope = w_uk.shape
    rope = w_dkv.shape[1] - kv_lora
    half = rope // 2
    q_scale = float((nope + rope) ** -0.5 * math.log2(math.e))
    n_sample = t_all - n_prompt

    wr = w_dkv[:, kv_lora:]
    zr = jnp.zeros((d, ROPE_PAD - rope), F32)
    w_down = jnp.concatenate(
        [w_dq, w_dkv[:, :kv_lora], wr, zr, -wr[:, half:], wr[:, :half], zr], axis=1).astype(BF16)
    wq = w_uq.reshape(q_lora, n_heads, nope + rope)
    wq_r = wq[..., nope:]
    w_qa = jnp.concatenate(
        [wq, jnp.zeros((q_lora, n_heads, HEAD_PAD - nope - rope), F32)], axis=-1
    ).reshape(q_lora, n_heads * HEAD_PAD).astype(BF16)
    w_qb = jnp.concatenate(
        [-wq_r[..., half:], wq_r[..., :half], jnp.zeros((q_lora, n_heads, ROPE_PAD - rope), F32)], axis=-1
    ).reshape(q_lora, n_heads * ROPE_PAD).astype(BF16)
    w_k = w_uk.reshape(kv_lora, n_heads * nope).astype(BF16)
    w_vt = w_uv.reshape(kv_lora, n_heads * nope).T.astype(BF16)
    w_ukt = jnp.transpose(w_uk, (1, 2, 0)).astype(BF16)
    w_uvh = jnp.transpose(w_uv, (1, 0, 2)).astype(BF16)

    tm = _tile(n_sample, 512)
    np_tiles = n_prompt // tm
    n_down = q_lora + kv_lora + 2 * ROPE_PAD
    x_specs = _split_row_specs(tm, d, np_tiles) if len(xs) == 2 else [_row_spec(tm, d)]
    widths = (kv_lora, kv_lora, rope, rope)
    rows = (n_prompt, n_sample, n_prompt, n_sample)
    prev_specs = []
    if prev is not None:
        prev_specs = _split_row_specs(tm, kv_lora, np_tiles, j) + _split_row_specs(tm, rope, np_tiles, j)
    cq, cqt, ckvb, ckvt, krb, *new = pl.pallas_call(
        functools.partial(_mla_down_kernel, n_x=len(xs), n_prev=len(prev_specs), np_tiles=np_tiles,
                          q_lora=q_lora, kv_lora=kv_lora, rope=rope, q_scale=q_scale),
        grid=(t_all // tm,),
        in_specs=x_specs + [_resident((1, d)), _resident((d, n_down)), _resident((1, q_lora)),
                            _resident((1, kv_lora)), _row_spec(tm, ROPE_PAD), _row_spec(tm, ROPE_PAD)]
        + prev_specs,
        out_specs=[_row_spec(tm, q_lora), _col_spec(q_lora, tm), _row_spec(tm, kv_lora),
                   _col_spec(kv_lora, tm), _row_spec(tm, ROPE_PAD)]
        + _split_row_specs(tm, kv_lora, np_tiles, j + 1) + _split_row_specs(tm, rope, np_tiles, j + 1),
        out_shape=[jax.ShapeDtypeStruct((t_all, q_lora), BF16), jax.ShapeDtypeStruct((q_lora, t_all), BF16),
                   jax.ShapeDtypeStruct((t_all, kv_lora), BF16), jax.ShapeDtypeStruct((kv_lora, t_all), BF16),
                   jax.ShapeDtypeStruct((t_all, ROPE_PAD), BF16)]
        + [jax.ShapeDtypeStruct((j + 1, r, w), F32) for r, w in zip(rows, widths)],
        compiler_params=_params("arbitrary"), name="mla_down",
    )(*xs, g_mix.reshape(1, d), w_down, g_q.reshape(1, q_lora), g_kv.reshape(1, kv_lora), cos, sin,
      *(prev or ()))

    tp = _tile(n_prompt, 512)
    w_qat = w_uq.T.astype(BF16)
    w_qbt = jnp.concatenate([-wq_r[..., half:], wq_r[..., :half]], axis=-1
                            ).reshape(q_lora, n_heads * rope).T.astype(BF16)
    qt = pl.pallas_call(
        functools.partial(_mla_qt_kernel, n_heads=n_heads, nope=nope),
        grid=(n_prompt // tp,),
        in_specs=[_col_spec(q_lora, tp), _resident(w_qat.shape), _resident(w_qbt.shape),
                  _col_spec(rope, tp), _col_spec(rope, tp)],
        out_specs=_col_spec(n_heads * HEAD_PAD, tp),
        out_shape=jax.ShapeDtypeStruct((n_heads * HEAD_PAD, n_prompt), BF16),
        compiler_params=_params("parallel"), name="mla_qt",
    )(cqt, w_qat, w_qbt, cost, sint)

    ts = _tile(n_sample, 512)
    off = n_prompt // ts
    q_s = pl.pallas_call(
        functools.partial(_mla_q_kernel, n_heads=n_heads, nope=nope),
        grid=(n_sample // ts,),
        in_specs=[_row_spec(ts, q_lora, off), _resident(w_qa.shape), _resident(w_qb.shape),
                  _row_spec(ts, ROPE_PAD, off), _row_spec(ts, ROPE_PAD, off)],
        out_specs=_row_spec(ts, n_heads * HEAD_PAD),
        out_shape=jax.ShapeDtypeStruct((n_sample, n_heads * HEAD_PAD), BF16),
        compiler_params=_params("parallel"), name="mla_q",
    )(cq, w_qa, w_qb, cos, sin)

    k, vt = pl.pallas_call(
        functools.partial(_mla_kv_kernel, n_heads=n_heads, nope=nope),
        grid=(n_prompt // tp,),
        in_specs=[_row_spec(tp, kv_lora), _col_spec(kv_lora, tp), _row_spec(tp, ROPE_PAD),
                  _resident(w_k.shape), _resident(w_vt.shape)],
        out_specs=[_row_spec(tp, n_heads * HEAD_PAD), _col_spec(n_heads * VT_PAD, tp)],
        out_shape=[jax.ShapeDtypeStruct((n_prompt, n_heads * HEAD_PAD), BF16),
                   jax.ShapeDtypeStruct((n_heads * VT_PAD, n_prompt), BF16)],
        compiler_params=_params("parallel"), name="mla_kv",
    )(ckvb, ckvt, krb, w_k, w_vt)

    blk = _tile(seq, 4096)
    kb = _tile(blk, 256)
    nq = seq // blk
    n_batch = n_prompt // seq
    n_o = n_heads * nope
    o_p = pl.pallas_call(
        functools.partial(_flash_kernel, nq=nq, tq=blk, kb=kb, kd=_tile(kb, 256), cw=_tile(blk, 256),
                          v_head=nope),
        grid=(n_batch, n_heads, nq),
        in_specs=[pl.BlockSpec((HEAD_PAD, blk), lambda b, h, i: (h, b * nq + i)),
                  pl.BlockSpec((seq, HEAD_PAD), lambda b, h, i: (b, h)),
                  pl.BlockSpec((VT_PAD, seq), lambda b, h, i: (h, b))],
        out_specs=pl.BlockSpec((blk, nope), lambda b, h, i: (b * nq + i, h)),
        out_shape=jax.ShapeDtypeStruct((n_prompt, n_o), BF16),
        scratch_shapes=[pltpu.VMEM((kb, blk), F32)],
        compiler_params=_params("parallel", "parallel", "arbitrary"), name="mla_flash",
    )(qt, k, vt)

    past = cache_c.shape[2]
    n_sb = n_sample // t_new
    row0 = n_prompt // t_new
    tk = _tile(past, 512)
    o_s = pl.pallas_call(
        functools.partial(_sample_attn_kernel, n_heads=n_heads, nope=nope, rope=rope, tk=tk, t_new=t_new),
        grid=(n_sb,),
        in_specs=[pl.BlockSpec((t_new, n_heads * HEAD_PAD), lambda b: (b, 0)),
                  _resident(w_ukt.shape), _resident(w_uvh.shape),
                  pl.BlockSpec((None, None, past, kv_lora), lambda b: (j, b, 0, 0)),
                  pl.BlockSpec((None, None, rope, past), lambda b: (j, b, 0, 0)),
                  pl.BlockSpec((t_new, kv_lora), lambda b: (row0 + b, 0)),
                  pl.BlockSpec((t_new, ROPE_PAD), lambda b: (row0 + b, 0))],
        out_specs=pl.BlockSpec((t_new, n_o), lambda b: (b, 0)),
        out_shape=jax.ShapeDtypeStruct((n_sample, n_o), BF16),
        scratch_shapes=[pltpu.VMEM((n_heads * t_new, kv_lora), BF16), pltpu.VMEM((n_heads * t_new, rope), BF16),
                        pltpu.VMEM((past, kv_lora), BF16), pltpu.VMEM((n_heads * t_new, past), F32),
                        pltpu.VMEM((n_heads * t_new, past), BF16)],
        compiler_params=_params("parallel"), name="mla_sample_attn",
    )(q_s, w_ukt, w_uvh, cache_c, cache_rt, ckvb, krb)

    x = pl.pallas_call(
        functools.partial(_proj_res_kernel, n_a=2, n_x=len(xs), np_tiles=np_tiles),
        grid=(t_all // tm,),
        in_specs=[pl.BlockSpec((None, n_o, d), lambda i: (j, 0, 0), pipeline_mode=pl.Buffered(1))]
        + _split_row_specs(tm, n_o, np_tiles) + x_specs,
        out_specs=_row_spec(tm, d),
        out_shape=jax.ShapeDtypeStruct((t_all, d), F32),
        compiler_params=_params("arbitrary"), name="mla_out",
    )(w_ob, o_p, o_s, *xs)
    return (x,), new


def _conv_layer(x, j, n_prompt, seq, t_new, state, g_mix, w_inb, conv_w, w_outb):
    assert conv_w.shape[0] == 3 and state.shape[1] == 2
    t_all, d = x.shape
    tm = _tile(t_all, 512)
    tn = _tile(d, 1024)
    nn = d // tn
    w_out_spec = pl.BlockSpec((None, d, d), lambda *_: (j, 0, 0), pipeline_mode=pl.Buffered(1))
    u, gb = pl.pallas_call(
        _conv_in_kernel,
        grid=(t_all // tm, nn),
        in_specs=[pl.BlockSpec((tm, d), lambda i, n: (i, 0)), _resident((1, d)),
                  pl.BlockSpec((None, d, tn), lambda i, n: (j, 0, n)),
                  pl.BlockSpec((None, d, tn), lambda i, n: (j, 0, nn + n)),
                  pl.BlockSpec((None, d, tn), lambda i, n: (j, 0, 2 * nn + n))],
        out_specs=[pl.BlockSpec((tm, tn), lambda i, n: (i, n)), pl.BlockSpec((tm, tn), lambda i, n: (i, n))],
        out_shape=[jax.ShapeDtypeStruct((t_all, d), F32), jax.ShapeDtypeStruct((t_all, d), BF16)],
        scratch_shapes=[pltpu.VMEM((tm, d), BF16)],
        compiler_params=_params("parallel", "arbitrary"), name="conv_in",
    )(x, g_mix.reshape(1, d), w_inb, w_inb, w_inb)

    tp = _tile(seq, 512)
    ns = seq // tp
    n_batch = n_prompt // seq
    hb = tp // 8
    tile_spec = pl.BlockSpec((tp, d), lambda b, i: (b * ns + i, 0))
    x_p = pl.pallas_call(
        _conv_out_prompt_kernel,
        grid=(n_batch, ns),
        in_specs=[tile_spec,
                  pl.BlockSpec((8, d), lambda b, i: (jnp.maximum((b * ns + i) * hb - 1, 0), 0)),
                  tile_spec, tile_spec, _resident((conv_w.shape[0], d)), w_out_spec],
        out_specs=tile_spec,
        out_shape=jax.ShapeDtypeStruct((n_prompt, d), F32),
        compiler_params=_params("parallel", "arbitrary"), name="conv_out_prompt",
    )(u, u, gb, x, conv_w, w_outb)

    n_sample = t_all - n_prompt
    ts = _tile(n_sample, 512)
    off = n_prompt // ts
    nb = ts // t_new
    x_s = pl.pallas_call(
        functools.partial(_conv_out_sample_kernel, t_new=t_new),
        grid=(n_sample // ts,),
        in_specs=[_row_spec(ts, d, off), pl.BlockSpec((nb, state.shape[1], d), lambda i: (i, 0, 0)),
                  _row_spec(ts, d, off), _row_spec(ts, d, off),
                  _resident((conv_w.shape[0], d)), w_out_spec],
        out_specs=_row_spec(ts, d),
        out_shape=jax.ShapeDtypeStruct((n_sample, d), F32),
        compiler_params=_params("parallel"), name="conv_out_sample",
    )(u, state, gb, x, conv_w, w_outb)
    return (x_p, x_s), u


def _ffn_layer(xs, layer, g, w1b, w2b, n_prompt, g_final=None):
    t_all, d = sum(a.shape[0] for a in xs), xs[0].shape[1]
    f = w1b.shape[2]
    tm = _tile(t_all - n_prompt, 512)
    tf = _tile(f, 512 if (g_final is not None and len(xs) == 2) else 1024)
    np_tiles = n_prompt // tm
    x_specs = _split_row_specs(tm, d, np_tiles) if len(xs) == 2 else [_row_spec(tm, d)]
    in_specs = x_specs + [_resident((1, d)), pl.BlockSpec((None, d, tf), lambda i, c: (layer, 0, c)),
                          pl.BlockSpec((None, tf, d), lambda i, c: (layer, c, 0))]
    grid = (t_all // tm, f // tf)
    h_scratch = pltpu.VMEM((tm, d), BF16)
    if g_final is None:
        return pl.pallas_call(
            functools.partial(_ffn_kernel, n_x=len(xs), np_tiles=np_tiles),
            grid=grid, in_specs=in_specs, out_specs=_row_spec(tm, d),
            out_shape=jax.ShapeDtypeStruct((t_all, d), F32), scratch_shapes=[h_scratch],
            compiler_params=_params("arbitrary", "arbitrary"), name="ffn",
        )(*xs, g.reshape(1, d), w1b, w2b)
    return pl.pallas_call(
        functools.partial(_ffn_final_kernel, n_x=len(xs), np_tiles=np_tiles),
        grid=grid, in_specs=in_specs + [_resident((1, d))],
        out_specs=_split_row_specs(tm, d, np_tiles),
        out_shape=[jax.ShapeDtypeStruct((n_prompt, d), F32), jax.ShapeDtypeStruct((t_all - n_prompt, d), F32)],
        scratch_shapes=[h_scratch, pltpu.VMEM((tm, d), F32)],
        compiler_params=_params("arbitrary", "arbitrary"), name="ffn_final",
    )(*xs, g.reshape(1, d), w1b, w2b, g_final.reshape(1, d))


def kernel(x_prompt, x_sample, cache_ckv, cache_krope, state_conv, norm_mix, w_dq, g_q, w_uq, w_dkv, g_kv,
           w_uk, w_uv, w_o, w_in, conv_w, w_out, norm_ffn, w_ff1, w_ff2, norm_final):
    n_batch, seq, d = x_prompt.shape
    s_batch, t_new, _ = x_sample.shape
    past = cache_ckv.shape[2]
    depth = norm_mix.shape[0]
    kv_lora = w_uk.shape[1]
    rope = w_dkv.shape[2] - kv_lora
    half = rope // 2
    n_prompt = n_batch * seq
    n_sample = s_batch * t_new

    pos = jnp.concatenate([jnp.tile(jnp.arange(seq, dtype=jnp.int32), n_batch),
                           jnp.tile(past + jnp.arange(t_new, dtype=jnp.int32), s_batch)])
    inv = ROPE_THETA ** (-jnp.arange(0, half, dtype=F32) / half)
    ang = pos.astype(F32)[:, None] * inv[None, :]
    zpad = jnp.zeros((pos.shape[0], ROPE_PAD - rope), F32)
    cos = jnp.concatenate([jnp.cos(ang), jnp.cos(ang), zpad], axis=1)
    sin = jnp.concatenate([jnp.sin(ang), jnp.sin(ang), zpad], axis=1)
    tabs = (cos, sin, cos[:n_prompt, :rope].T, sin[:n_prompt, :rope].T)

    cache_rt = jnp.swapaxes(cache_krope, 2, 3)
    w1b = w_ff1.astype(BF16)
    w2b = w_ff2.astype(BF16)
    w_ob = w_o.astype(BF16)
    w_inb = w_in.astype(BF16)
    w_outb = w_out.astype(BF16)

    xs = (x_prompt.reshape(n_prompt, d), x_sample.reshape(n_sample, d))
    latents, new_u = None, []
    for i in range(depth):
        j = i // 2
        if i % 2 == 0:
            xs, latents = _mla_layer(xs, j, latents, n_prompt, seq, t_new, tabs, cache_ckv, cache_rt,
                                     norm_mix[i], w_dq[j], g_q[j], w_uq[j], w_dkv[j], g_kv[j],
                                     w_uk[j], w_uv[j], w_ob)
        else:
            xs, u = _conv_layer(xs[0], j, n_prompt, seq, t_new, state_conv[j], norm_mix[i], w_inb, conv_w[j],
                                w_outb)
            new_u.append(u)
        if i + 1 < depth:
            xs = (_ffn_layer(xs, i, norm_ffn[i], w1b, w2b, n_prompt),)
        else:
            y_prompt, y_sample = _ffn_layer(xs, i, norm_ffn[i], w1b, w2b, n_prompt, norm_final)

    def tails(u, lo, lead, inner, keep):
        return jnp.stack([lax.slice_in_dim(u, lo + (b + 1) * inner - keep, lo + (b + 1) * inner, axis=0)
                          for b in range(lead)])

    keep = conv_w.shape[1] - 1
    ckv_p, ckv_s, kr_p, kr_s = latents
    n_a = ckv_p.shape[0]
    u_p = jnp.stack([tails(u, 0, n_batch, seq, keep) for u in new_u])
    u_s = jnp.stack([tails(u, n_prompt, s_batch, t_new, keep) for u in new_u])
    return (y_prompt.reshape(n_batch, seq, d), y_sample.reshape(s_batch, t_new, d),
            ckv_p.reshape(n_a, n_batch, seq, -1), kr_p.reshape(n_a, n_batch, seq, -1), u_p,
            ckv_s.reshape(n_a, s_batch, t_new, -1), kr_s.reshape(n_a, s_batch, t_new, -1), u_s)
```

```python
import functools
import math

import jax
import jax.numpy as jnp
from jax import lax
from jax.experimental import pallas as pl
from jax.experimental.pallas import tpu as pltpu

CHUNK = 64
ROPE_THETA = 10000.0
EPS = 1e-6
HEAD_PAD = 256
ROPE_PAD = 128
VT_PAD = 144
VMEM_LIMIT = 56 * 1024 * 1024

F32 = jnp.float32
BF16 = jnp.bfloat16


def _params(*sem):
    return pltpu.CompilerParams(dimension_semantics=sem, vmem_limit_bytes=VMEM_LIMIT)


def _tile(total, pref):
    t = min(total, pref)
    assert total % t == 0, (total, pref)
    return t


def _resident(shape):
    nd = len(shape)
    return pl.BlockSpec(shape, lambda *_: (0,) * nd, pipeline_mode=pl.Buffered(1))


def _rms(xf, g):
    return xf * lax.rsqrt(jnp.mean(xf * xf, axis=-1, keepdims=True) + EPS) * g


def _dot(a, b):
    return jnp.dot(a, b, preferred_element_type=F32)


def _dot_nt(a, b):
    return lax.dot_general(a, b, (((1,), (1,)), ((), ())), preferred_element_type=F32)


def _load_rows(x_refs, np_tiles):
    if len(x_refs) == 1:
        return x_refs[0][...]
    return jnp.where(pl.program_id(0) < np_tiles, x_refs[0][...], x_refs[1][...])


def _store_rows(p_ref, s_ref, val, np_tiles, prev=None):
    is_prompt = pl.program_id(0) < np_tiles

    def put(o_ref, prev_ref):
        if prev is None:
            o_ref[...] = val
        else:
            if prev_ref is not None:
                o_ref[:-1] = prev_ref[...]
            o_ref[o_ref.shape[0] - 1] = val

    @pl.when(is_prompt)
    def _():
        put(p_ref, prev and prev[0])

    @pl.when(jnp.logical_not(is_prompt))
    def _():
        put(s_ref, prev and prev[1])


def _mla_down_kernel(*refs, n_x, n_prev, np_tiles, q_lora, kv_lora, rope, q_scale):
    x_refs, refs = refs[:n_x], refs[n_x:]
    g_ref, w_ref, gq_ref, gkv_ref, cos_ref, sin_ref = refs[:6]
    prev = refs[6:6 + n_prev] if n_prev else (None,) * 4
    cq_ref, cqt_ref, ckvb_ref, ckvt_ref, krb_ref, ckvp_ref, ckvs_ref, krp_ref, krs_ref = refs[6 + n_prev:]
    h = _rms(_load_rows(x_refs, np_tiles), g_ref[...]).astype(BF16)
    y = _dot(h, w_ref[...])
    cq = _rms(y[:, :q_lora], gq_ref[...]) * q_scale
    cq_ref[...] = cq.astype(BF16)
    cqt_ref[...] = cq.T.astype(BF16)
    ckv = _rms(y[:, q_lora:q_lora + kv_lora], gkv_ref[...])
    _store_rows(ckvp_ref, ckvs_ref, ckv, np_tiles, prev[0:2])
    ckvb_ref[...] = ckv.astype(BF16)
    ckvt_ref[...] = ckv.T.astype(BF16)
    o = q_lora + kv_lora
    kr = y[:, o:o + ROPE_PAD] * cos_ref[...] + y[:, o + ROPE_PAD:o + 2 * ROPE_PAD] * sin_ref[...]
    _store_rows(krp_ref, krs_ref, kr[:, :rope], np_tiles, prev[2:4])
    krb_ref[...] = kr.astype(BF16)


def _mla_q_kernel(cq_ref, wa_ref, wb_ref, cos_ref, sin_ref, q_ref, *, n_heads, nope):
    cq = cq_ref[...]
    cos = cos_ref[...]
    sin = sin_ref[...]
    for hp in range(n_heads // 2):
        a = _dot(cq, wa_ref[:, hp * 2 * HEAD_PAD:(hp + 1) * 2 * HEAD_PAD])
        b = _dot(cq, wb_ref[:, hp * 2 * ROPE_PAD:(hp + 1) * 2 * ROPE_PAD])
        for s in range(2):
            c0 = (2 * hp + s) * HEAD_PAD
            q_ref[:, c0:c0 + nope] = a[:, s * HEAD_PAD:s * HEAD_PAD + nope].astype(BF16)
            ar = a[:, s * HEAD_PAD + nope:(s + 1) * HEAD_PAD]
            br = b[:, s * ROPE_PAD:(s + 1) * ROPE_PAD]
            q_ref[:, c0 + nope:c0 + HEAD_PAD] = (ar * cos + br * sin).astype(BF16)


def _mla_qt_kernel(cqt_ref, wat_ref, wbt_ref, cost_ref, sint_ref, qt_ref, *, n_heads, nope):
    cqt = cqt_ref[...]
    cost = cost_ref[...]
    sint = sint_ref[...]
    rope = cost.shape[0]
    hd = nope + rope
    zeros = jnp.zeros((HEAD_PAD - hd, cqt.shape[1]), BF16)
    for hp in range(n_heads // 2):
        a = _dot(wat_ref[hp * 2 * hd:(hp + 1) * 2 * hd, :], cqt)
        b = _dot(wbt_ref[hp * 2 * rope:(hp + 1) * 2 * rope, :], cqt)
        for s in range(2):
            r0 = (2 * hp + s) * HEAD_PAD
            qt_ref[r0:r0 + nope, :] = a[s * hd:s * hd + nope].astype(BF16)
            ar = a[s * hd + nope:(s + 1) * hd]
            br = b[s * rope:(s + 1) * rope]
            qt_ref[r0 + nope:r0 + hd, :] = (ar * cost + br * sint).astype(BF16)
            qt_ref[r0 + hd:r0 + HEAD_PAD, :] = zeros


def _mla_kv_kernel(ckvb_ref, ckvt_ref, krb_ref, wk_ref, wvt_ref, k_ref, vt_ref, *, n_heads, nope):
    c = ckvb_ref[...]
    krb = krb_ref[...]
    tm = c.shape[0]
    vt = _dot(wvt_ref[...], ckvt_ref[...]).astype(BF16)
    ones_row = (lax.broadcasted_iota(jnp.int32, (VT_PAD - nope, tm), 0) == 0).astype(BF16)
    for h in range(n_heads):
        vt_ref[h * VT_PAD:h * VT_PAD + nope, :] = vt[h * nope:(h + 1) * nope]
        vt_ref[h * VT_PAD + nope:(h + 1) * VT_PAD, :] = ones_row
    for hp in range(n_heads // 2):
        kn = _dot(c, wk_ref[:, hp * 2 * nope:(hp + 1) * 2 * nope])
        for s in range(2):
            c0 = (2 * hp + s) * HEAD_PAD
            k_ref[:, c0:c0 + nope] = kn[:, s * nope:(s + 1) * nope].astype(BF16)
            k_ref[:, c0 + nope:c0 + HEAD_PAD] = krb


def _flash_kernel(qt_ref, k_ref, vt_ref, o_ref, s0_ref, *, nq, tq, kb, kd, cw, v_head):
    qi = pl.program_id(2)
    ncol = tq // cw
    main = tuple((off, kb, 0) for off in range(0, tq, kb))
    diag = ((0, kb, 0),) + tuple((off, kd, off // cw) for off in range(kb, tq, kd))

    def scores(base, sub, c):
        off, size, _ = sub
        start = pl.multiple_of(base + off, size)
        return _dot(k_ref[pl.ds(start, size), :], qt_ref[:, c * cw:(c + 1) * cw])

    def update(st, sc, base, sub, c, masked):
        off, size, _ = sub
        start = pl.multiple_of(base + off, size)
        if masked and off + size > c * cw:
            kc = (lax.broadcasted_iota(jnp.int32, (size, cw), 0) + off) // CHUNK
            qc = (lax.broadcasted_iota(jnp.int32, (size, cw), 1) + c * cw) // CHUNK
            sc = jnp.where(kc <= qc, sc, -jnp.inf)
        m_prev, acc_prev = st
        m_new = jnp.maximum(m_prev, jnp.max(sc, axis=0, keepdims=True))
        p = jnp.exp2(sc - m_new).astype(BF16)
        return m_new, jnp.exp2(m_prev - m_new) * acc_prev + _dot(vt_ref[:, pl.ds(start, size)], p)

    def trip(base, state, subs, masked, stage_next):
        state = list(state)
        cur = [None] * ncol
        for t, sub in enumerate(subs):
            nxt = [None] * ncol
            for c in range(sub[2], ncol):
                if t + 1 < len(subs) and c >= subs[t + 1][2]:
                    nxt[c] = scores(base, subs[t + 1], c)
                sc = s0_ref[:, c * cw:(c + 1) * cw] if cur[c] is None else cur[c]
                state[c] = update(state[c], sc, base, sub, c, masked)
                if t + 1 == len(subs) and stage_next:
                    s0_ref[:, c * cw:(c + 1) * cw] = scores(base + tq, main[0], c)
            cur = nxt
        return tuple(state)

    for c in range(ncol):
        s0_ref[:, c * cw:(c + 1) * cw] = scores(0, main[0], c)
    init = tuple((jnp.full((1, cw), -jnp.inf, F32), jnp.zeros((vt_ref.shape[0], cw), F32)) for _ in range(ncol))
    state = init
    if nq > 1:
        state = lax.fori_loop(0, qi, lambda j, st: trip(j * tq, st, main, False, True), init)
    state = trip(qi * tq, state, diag, True, False)
    for c in range(ncol):
        acc = state[c][1]
        o_ref[c * cw:(c + 1) * cw, :] = (acc[:v_head] / acc[v_head:v_head + 1]).T.astype(BF16)


def _lane_fold(a, op):
    out = a[:, :128]
    for i in range(1, a.shape[1] // 128):
        out = op(out, a[:, i * 128:(i + 1) * 128])
    return out


def _sample_attn_kernel(q_ref, wukt_ref, wuv_ref, cc_ref, crt_ref, nc_ref, nr_ref, o_ref,
                        ql_ref, qr_ref, cb_ref, s_ref, p_ref, *, n_heads, nope, rope, tk, t_new):
    past = cc_ref.shape[0]
    for h in range(n_heads):
        c0 = h * HEAD_PAD
        ql_ref[h * t_new:(h + 1) * t_new, :] = _dot(q_ref[:, c0:c0 + nope], wukt_ref[h]).astype(BF16)
        qr_ref[h * t_new:(h + 1) * t_new, :] = q_ref[:, c0 + nope:c0 + nope + rope]
    ql = ql_ref[...]
    qr = qr_ref[...]
    nc = nc_ref[...]
    s_new = _dot_nt(ql, nc) + _dot_nt(qr, nr_ref[:, :rope])
    rows = ql.shape[0]
    nchunk = past // tk
    half = nchunk // 2
    ks = [slice(j * tk, (j + 1) * tk) for j in range(nchunk)]

    def score(j, m_lane):
        cb = cc_ref[ks[j], :].astype(BF16)
        cb_ref[ks[j], :] = cb
        s = _dot_nt(ql, cb) + _dot(qr, crt_ref[:, ks[j]].astype(BF16))
        s_ref[:, ks[j]] = s
        return jnp.maximum(m_lane, _lane_fold(s, jnp.maximum))

    def probs(j, m, l_lane):
        p = jnp.exp2(s_ref[:, ks[j]] - m)
        p_ref[:, ks[j]] = p.astype(BF16)
        return l_lane + _lane_fold(p, jnp.add)

    neg = jnp.full((rows, 128), -jnp.inf, F32)
    zero = jnp.zeros((rows, 128), F32)
    m_lane = neg
    for j in range(half):
        m_lane = score(j, m_lane)
    m_a = jnp.max(m_lane, axis=-1, keepdims=True)
    m_lane, l_a = neg, zero
    for i, j in enumerate(range(half, nchunk)):
        m_lane = score(j, m_lane)
        if i < half:
            l_a = probs(i, m_a, l_a)
    m_b = jnp.maximum(jnp.max(m_lane, axis=-1, keepdims=True), jnp.max(s_new, axis=-1, keepdims=True))
    m = jnp.maximum(m_a, m_b)
    acc_a, l_b = jnp.zeros((rows, cb_ref.shape[1]), F32), zero
    for i, j in enumerate(range(half, nchunk)):
        l_b = probs(j, m, l_b)
        if i < half:
            acc_a = acc_a + _dot(p_ref[:, ks[i]], cb_ref[ks[i], :])
    p_new = jnp.exp2(s_new - m)
    acc_b = _dot(p_new.astype(BF16), nc)
    for j in range(half, nchunk):
        acc_b = acc_b + _dot(p_ref[:, ks[j]], cb_ref[ks[j], :])
    alpha = jnp.exp2(m_a - m)
    l = (alpha * jnp.sum(l_a, axis=-1, keepdims=True) + jnp.sum(l_b, axis=-1, keepdims=True)
         + jnp.sum(p_new, axis=-1, keepdims=True))
    ol = ((alpha * acc_a + acc_b) / l).astype(BF16)
    for h in range(n_heads):
        o_ref[:, h * nope:(h + 1) * nope] = _dot(ol[h * t_new:(h + 1) * t_new, :], wuv_ref[h]).astype(BF16)


def _proj_res_kernel(w_ref, *refs, n_a, n_x, np_tiles):
    a_refs, x_refs, o_ref = refs[:n_a], refs[n_a:n_a + n_x], refs[n_a + n_x]
    o_ref[...] = _load_rows(x_refs, np_tiles) + _dot(_load_rows(a_refs, np_tiles), w_ref[...])


def _ffn_accumulate(x_refs, g_ref, w1_ref, w2_ref, acc_ref, h_ref, np_tiles):
    @pl.when(pl.program_id(1) == 0)
    def _():
        x = _load_rows(x_refs, np_tiles)
        h_ref[...] = _rms(x, g_ref[...]).astype(BF16)
        acc_ref[...] = x

    a = jnp.maximum(_dot(h_ref[...], w1_ref[...]), 0.0)
    acc_ref[...] += _dot((a * a).astype(BF16), w2_ref[...])


def _ffn_kernel(*refs, n_x, np_tiles):
    x_refs, (g_ref, w1_ref, w2_ref, o_ref, h_ref) = refs[:n_x], refs[n_x:]
    _ffn_accumulate(x_refs, g_ref, w1_ref, w2_ref, o_ref, h_ref, np_tiles)


def _ffn_final_kernel(*refs, n_x, np_tiles):
    x_refs, (g_ref, w1_ref, w2_ref, gf_ref, yp_ref, ys_ref, h_ref, acc_ref) = refs[:n_x], refs[n_x:]
    _ffn_accumulate(x_refs, g_ref, w1_ref, w2_ref, acc_ref, h_ref, np_tiles)

    @pl.when(pl.program_id(1) == pl.num_programs(1) - 1)
    def _():
        _store_rows(yp_ref, ys_ref, _rms(acc_ref[...], gf_ref[...]), np_tiles)


def _conv_in_kernel(x_ref, g_ref, wb_ref, wc_ref, wv_ref, u_ref, gb_ref, h_ref):
    @pl.when(pl.program_id(1) == 0)
    def _():
        h_ref[...] = _rms(x_ref[...], g_ref[...]).astype(BF16)

    h = h_ref[...]
    gb_ref[...] = _dot(h, wb_ref[...]).astype(BF16)
    u_ref[...] = _dot(h, wc_ref[...]) * _dot(h, wv_ref[...])


def _conv_mix(u, p1, p2, first, second, gb, cw, wout, x):
    u1 = jnp.where(first, p1, pltpu.roll(u, 1, axis=0))
    u2 = jnp.where(first, p2, jnp.where(second, p1, pltpu.roll(u, 2, axis=0)))
    conv = cw[0:1] * u2 + cw[1:2] * u1 + cw[2:3] * u
    return x + _dot((gb.astype(F32) * conv).astype(BF16), wout)


def _conv_out_kernel(u_ref, halo_ref, st_ref, gb_ref, x_ref, cw_ref, wout_ref, o_ref,
                     *, np_tiles, tiles_per_seq, t_new):
    i = pl.program_id(0)
    tm, d = u_ref.shape
    row = lax.broadcasted_iota(jnp.int32, (tm, 1), 0)

    @pl.when(i < np_tiles)
    def _():
        has_past = i % tiles_per_seq > 0
        halo = halo_ref[...]
        p1 = jnp.where(has_past, halo[7:8], 0.0)
        p2 = jnp.where(has_past, halo[6:7], 0.0)
        o_ref[...] = _conv_mix(u_ref[...], p1, p2, row == 0, row == 1, gb_ref[...], cw_ref[...],
                               wout_ref[...], x_ref[...])

    @pl.when(i >= np_tiles)
    def _():
        nb = tm // t_new
        st = st_ref[...]
        p1 = jnp.broadcast_to(st[:, 1:2, :], (nb, t_new, d)).reshape(tm, d)
        p2 = jnp.broadcast_to(st[:, 0:1, :], (nb, t_new, d)).reshape(tm, d)
        t = row % t_new
        o_ref[...] = _conv_mix(u_ref[...], p1, p2, t == 0, t == 1, gb_ref[...], cw_ref[...],
                               wout_ref[...], x_ref[...])


def _row_spec(tm, n, off=0):
    return pl.BlockSpec((tm, n), lambda i, *_: (i + off, 0))


def _col_spec(n, tm):
    return pl.BlockSpec((n, tm), lambda i, *_: (0, i))


def _split_row_specs(tm, n, np_tiles, layers=None):
    def p_map(i, *_):
        return (jnp.minimum(i, np_tiles - 1), 0)

    def s_map(i, *_):
        return (jnp.maximum(i - np_tiles, 0), 0)

    if layers is None:
        return [pl.BlockSpec((tm, n), p_map), pl.BlockSpec((tm, n), s_map)]
    return [pl.BlockSpec((layers, tm, n), lambda i, *_: (0,) + p_map(i)),
            pl.BlockSpec((layers, tm, n), lambda i, *_: (0,) + s_map(i))]


def _mla_layer(xs, j, prev, n_prompt, seq, t_new, tabs, cache_c, cache_rt, g_mix, w_dq, g_q, w_uq, w_dkv, g_kv,
               w_uk, w_uv, w_ob):
    cos, sin, cost, sint = tabs
    t_all, d = sum(a.shape[0] for a in xs), xs[0].shape[1]
    q_lora = w_dq.shape[1]
    kv_lora, n_heads, nope = w_uk.shape
    rope = w_dkv.shape[1] - kv_lora
    half = rope // 2
    q_scale = float((nope + rope) ** -0.5 * math.log2(math.e))
    n_sample = t_all - n_prompt

    wr = w_dkv[:, kv_lora:]
    zr = jnp.zeros((d, ROPE_PAD - rope), F32)
    w_down = jnp.concatenate(
        [w_dq, w_dkv[:, :kv_lora], wr, zr, -wr[:, half:], wr[:, :half], zr], axis=1).astype(BF16)
    wq = w_uq.reshape(q_lora, n_heads, nope + rope)
    wq_r = wq[..., nope:]
    w_qa = jnp.concatenate(
        [wq, jnp.zeros((q_lora, n_heads, HEAD_PAD - nope - rope), F32)], axis=-1
    ).reshape(q_lora, n_heads * HEAD_PAD).astype(BF16)
    w_qb = jnp.concatenate(
        [-wq_r[..., half:], wq_r[..., :half], jnp.zeros((q_lora, n_heads, ROPE_PAD - rope), F32)], axis=-1
    ).reshape(q_lora, n_heads * ROPE_PAD).astype(BF16)
    w_k = w_uk.reshape(kv_lora, n_heads * nope).astype(BF16)
    w_vt = w_uv.reshape(kv_lora, n_heads * nope).T.astype(BF16)
    w_ukt = jnp.transpose(w_uk, (1, 2, 0)).astype(BF16)
    w_uvh = jnp.transpose(w_uv, (1, 0, 2)).astype(BF16)

    tm = _tile(n_sample, 512)
    np_tiles = n_prompt // tm
    n_down = q_lora + kv_lora + 2 * ROPE_PAD
    x_specs = _split_row_specs(tm, d, np_tiles) if len(xs) == 2 else [_row_spec(tm, d)]
    widths = (kv_lora, kv_lora, rope, rope)
    rows = (n_prompt, n_sample, n_prompt, n_sample)
    prev_specs = []
    if prev is not None:
        prev_specs = _split_row_specs(tm, kv_lora, np_tiles, j) + _split_row_specs(tm, rope, np_tiles, j)
    cq, cqt, ckvb, ckvt, krb, *new = pl.pallas_call(
        functools.partial(_mla_down_kernel, n_x=len(xs), n_prev=len(prev_specs), np_tiles=np_tiles,
                          q_lora=q_lora, kv_lora=kv_lora, rope=rope, q_scale=q_scale),
        grid=(t_all // tm,),
        in_specs=x_specs + [_resident((1, d)), _resident((d, n_down)), _resident((1, q_lora)),
                            _resident((1, kv_lora)), _row_spec(tm, ROPE_PAD), _row_spec(tm, ROPE_PAD)]
        + prev_specs,
        out_specs=[_row_spec(tm, q_lora), _col_spec(q_lora, tm), _row_spec(tm, kv_lora),
                   _col_spec(kv_lora, tm), _row_spec(tm, ROPE_PAD)]
        + _split_row_specs(tm, kv_lora, np_tiles, j + 1) + _split_row_specs(tm, rope, np_tiles, j + 1),
        out_shape=[jax.ShapeDtypeStruct((t_all, q_lora), BF16), jax.ShapeDtypeStruct((q_lora, t_all), BF16),
                   jax.ShapeDtypeStruct((t_all, kv_lora), BF16), jax.ShapeDtypeStruct((kv_lora, t_all), BF16),
                   jax.ShapeDtypeStruct((t_all, ROPE_PAD), BF16)]
        + [jax.ShapeDtypeStruct((j + 1, r, w), F32) for r, w in zip(rows, widths)],
        compiler_params=_params("arbitrary"), name="mla_down",
    )(*xs, g_mix.reshape(1, d), w_down, g_q.reshape(1, q_lora), g_kv.reshape(1, kv_lora), cos, sin,
      *(prev or ()))

    tp = _tile(n_prompt, 512)
    w_qat = w_uq.T.astype(BF16)
    w_qbt = jnp.concatenate([-wq_r[..., half:], wq_r[..., :half]], axis=-1
                            ).reshape(q_lora, n_heads * rope).T.astype(BF16)
    qt = pl.pallas_call(
        functools.partial(_mla_qt_kernel, n_heads=n_heads, nope=nope),
        grid=(n_prompt // tp,),
        in_specs=[_col_spec(q_lora, tp), _resident(w_qat.shape), _resident(w_qbt.shape),
                  _col_spec(rope, tp), _col_spec(rope, tp)],
        out_specs=_col_spec(n_heads * HEAD_PAD, tp),
        out_shape=jax.ShapeDtypeStruct((n_heads * HEAD_PAD, n_prompt), BF16),
        compiler_params=_params("parallel"), name="mla_qt",
    )(cqt, w_qat, w_qbt, cost, sint)

    ts = _tile(n_sample, 512)
    off = n_prompt // ts
    q_s = pl.pallas_call(
        functools.partial(_mla_q_kernel, n_heads=n_heads, nope=nope),
        grid=(n_sample // ts,),
        in_specs=[_row_spec(ts, q_lora, off), _resident(w_qa.shape), _resident(w_qb.shape),
                  _row_spec(ts, ROPE_PAD, off), _row_spec(ts, ROPE_PAD, off)],
        out_specs=_row_spec(ts, n_heads * HEAD_PAD),
        out_shape=jax.ShapeDtypeStruct((n_sample, n_heads * HEAD_PAD), BF16),
        compiler_params=_params("parallel"), name="mla_q",
    )(cq, w_qa, w_qb, cos, sin)

    k, vt = pl.pallas_call(
        functools.partial(_mla_kv_kernel, n_heads=n_heads, nope=nope),
        grid=(n_prompt // tp,),
        in_specs=[_row_spec(tp, kv_lora), _col_spec(kv_lora, tp), _row_spec(tp, ROPE_PAD),
                  _resident(w_k.shape), _resident(w_vt.shape)],
        out_specs=[_row_spec(tp, n_heads * HEAD_PAD), _col_spec(n_heads * VT_PAD, tp)],
        out_shape=[jax.ShapeDtypeStruct((n_prompt, n_heads * HEAD_PAD), BF16),
                   jax.ShapeDtypeStruct((n_heads * VT_PAD, n_prompt), BF16)],
        compiler_params=_params("parallel"), name="mla_kv",
    )(ckvb, ckvt, krb, w_k, w_vt)

    blk = _tile(seq, 4096)
    kb = _tile(blk, 256)
    nq = seq // blk
    n_batch = n_prompt // seq
    n_o = n_heads * nope
    o_p = pl.pallas_call(
        functools.partial(_flash_kernel, nq=nq, tq=blk, kb=kb, kd=_tile(kb, 256), cw=_tile(blk, 256),
                          v_head=nope),
        grid=(n_batch, n_heads, nq),
        in_specs=[pl.BlockSpec((HEAD_PAD, blk), lambda b, h, i: (h, b * nq + i)),
                  pl.BlockSpec((seq, HEAD_PAD), lambda b, h, i: (b, h)),
                  pl.BlockSpec((VT_PAD, seq), lambda b, h, i: (h, b))],
        out_specs=pl.BlockSpec((blk, nope), lambda b, h, i: (b * nq + i, h)),
        out_shape=jax.ShapeDtypeStruct((n_prompt, n_o), BF16),
        scratch_shapes=[pltpu.VMEM((kb, blk), F32)],
        compiler_params=_params("parallel", "parallel", "arbitrary"), name="mla_flash",
    )(qt, k, vt)

    past = cache_c.shape[2]
    n_sb = n_sample // t_new
    row0 = n_prompt // t_new
    tk = _tile(past, 512)
    o_s = pl.pallas_call(
        functools.partial(_sample_attn_kernel, n_heads=n_heads, nope=nope, rope=rope, tk=tk, t_new=t_new),
        grid=(n_sb,),
        in_specs=[pl.BlockSpec((t_new, n_heads * HEAD_PAD), lambda b: (b, 0)),
                  _resident(w_ukt.shape), _resident(w_uvh.shape),
                  pl.BlockSpec((None, None, past, kv_lora), lambda b: (j, b, 0, 0)),
                  pl.BlockSpec((None, None, rope, past), lambda b: (j, b, 0, 0)),
                  pl.BlockSpec((t_new, kv_lora), lambda b: (row0 + b, 0)),
                  pl.BlockSpec((t_new, ROPE_PAD), lambda b: (row0 + b, 0))],
        out_specs=pl.BlockSpec((t_new, n_o), lambda b: (b, 0)),
        out_shape=jax.ShapeDtypeStruct((n_sample, n_o), BF16),
        scratch_shapes=[pltpu.VMEM((n_heads * t_new, kv_lora), BF16), pltpu.VMEM((n_heads * t_new, rope), BF16),
                        pltpu.VMEM((past, kv_lora), BF16), pltpu.VMEM((n_heads * t_new, past), F32),
                        pltpu.VMEM((n_heads * t_new, past), BF16)],
        compiler_params=_params("parallel"), name="mla_sample_attn",
    )(q_s, w_ukt, w_uvh, cache_c, cache_rt, ckvb, krb)

    x = pl.pallas_call(
        functools.partial(_proj_res_kernel, n_a=2, n_x=len(xs), np_tiles=np_tiles),
        grid=(t_all // tm,),
        in_specs=[pl.BlockSpec((None, n_o, d), lambda i: (j, 0, 0), pipeline_mode=pl.Buffered(1))]
        + _split_row_specs(tm, n_o, np_tiles) + x_specs,
        out_specs=_row_spec(tm, d),
        out_shape=jax.ShapeDtypeStruct((t_all, d), F32),
        compiler_params=_params("arbitrary"), name="mla_out",
    )(w_ob, o_p, o_s, *xs)
    return (x,), new


def _conv_layer(x, j, n_prompt, seq, t_new, state, g_mix, w_inb, conv_w, w_outb):
    assert conv_w.shape[0] == 3 and state.shape[1] == 2
    t_all, d = x.shape
    tm = _tile(t_all, 512)
    tn = _tile(d, 1024)
    nn = d // tn
    w_out_spec = pl.BlockSpec((None, d, d), lambda *_: (j, 0, 0), pipeline_mode=pl.Buffered(1))
    u, gb = pl.pallas_call(
        _conv_in_kernel,
        grid=(t_all // tm, nn),
        in_specs=[pl.BlockSpec((tm, d), lambda i, n: (i, 0)), _resident((1, d)),
                  pl.BlockSpec((None, d, tn), lambda i, n: (j, 0, n)),
                  pl.BlockSpec((None, d, tn), lambda i, n: (j, 0, nn + n)),
                  pl.BlockSpec((None, d, tn), lambda i, n: (j, 0, 2 * nn + n))],
        out_specs=[pl.BlockSpec((tm, tn), lambda i, n: (i, n)), pl.BlockSpec((tm, tn), lambda i, n: (i, n))],
        out_shape=[jax.ShapeDtypeStruct((t_all, d), F32), jax.ShapeDtypeStruct((t_all, d), BF16)],
        scratch_shapes=[pltpu.VMEM((tm, d), BF16)],
        compiler_params=_params("parallel", "arbitrary"), name="conv_in",
    )(x, g_mix.reshape(1, d), w_inb, w_inb, w_inb)

    tc = _tile(min(seq, t_all - n_prompt), 512)
    assert seq % tc == 0 and tc % t_new == 0 and tc % 8 == 0
    np_tiles = n_prompt // tc
    hb = tc // 8
    nb = tc // t_new
    x_new = pl.pallas_call(
        functools.partial(_conv_out_kernel, np_tiles=np_tiles, tiles_per_seq=seq // tc, t_new=t_new),
        grid=(t_all // tc,),
        in_specs=[_row_spec(tc, d),
                  pl.BlockSpec((8, d), lambda i: (jnp.maximum(i * hb - 1, 0), 0)),
                  pl.BlockSpec((nb, state.shape[1], d), lambda i: (jnp.maximum(i - np_tiles, 0), 0, 0)),
                  _row_spec(tc, d), _row_spec(tc, d), _resident((conv_w.shape[0], d)), w_out_spec],
        out_specs=_row_spec(tc, d),
        out_shape=jax.ShapeDtypeStruct((t_all, d), F32),
        compiler_params=_params("parallel"), name="conv_out",
    )(u, u, state, gb, x, conv_w, w_outb)
    return (x_new,), u


def _ffn_layer(xs, layer, g, w1b, w2b, n_prompt, g_final=None):
    t_all, d = sum(a.shape[0] for a in xs), xs[0].shape[1]
    f = w1b.shape[2]
    tm = _tile(t_all - n_prompt, 512)
    tf = _tile(f, 1024)
    np_tiles = n_prompt // tm
    x_specs = _split_row_specs(tm, d, np_tiles) if len(xs) == 2 else [_row_spec(tm, d)]
    in_specs = x_specs + [_resident((1, d)), pl.BlockSpec((None, d, tf), lambda i, c: (layer, 0, c)),
                          pl.BlockSpec((None, tf, d), lambda i, c: (layer, c, 0))]
    grid = (t_all // tm, f // tf)
    h_scratch = pltpu.VMEM((tm, d), BF16)
    if g_final is None:
        return pl.pallas_call(
            functools.partial(_ffn_kernel, n_x=len(xs), np_tiles=np_tiles),
            grid=grid, in_specs=in_specs, out_specs=_row_spec(tm, d),
            out_shape=jax.ShapeDtypeStruct((t_all, d), F32), scratch_shapes=[h_scratch],
            compiler_params=_params("arbitrary", "arbitrary"), name="ffn",
        )(*xs, g.reshape(1, d), w1b, w2b)
    return pl.pallas_call(
        functools.partial(_ffn_final_kernel, n_x=len(xs), np_tiles=np_tiles),
        grid=grid, in_specs=in_specs + [_resident((1, d))],
        out_specs=_split_row_specs(tm, d, np_tiles),
        out_shape=[jax.ShapeDtypeStruct((n_prompt, d), F32), jax.ShapeDtypeStruct((t_all - n_prompt, d), F32)],
        scratch_shapes=[h_scratch, pltpu.VMEM((tm, d), F32)],
        compiler_params=_params("arbitrary", "arbitrary"), name="ffn_final",
    )(*xs, g.reshape(1, d), w1b, w2b, g_final.reshape(1, d))


def kernel(x_prompt, x_sample, cache_ckv, cache_krope, state_conv, norm_mix, w_dq, g_q, w_uq, w_dkv, g_kv,
           w_uk, w_uv, w_o, w_in, conv_w, w_out, norm_ffn, w_ff1, w_ff2, norm_final):
    n_batch, seq, d = x_prompt.shape
    s_batch, t_new, _ = x_sample.shape
    past = cache_ckv.shape[2]
    depth = norm_mix.shape[0]
    kv_lora = w_uk.shape[1]
    rope = w_dkv.shape[2] - kv_lora
    half = rope // 2
    n_prompt = n_batch * seq
    n_sample = s_batch * t_new

    pos = jnp.concatenate([jnp.tile(jnp.arange(seq, dtype=jnp.int32), n_batch),
                           jnp.tile(past + jnp.arange(t_new, dtype=jnp.int32), s_batch)])
    inv = ROPE_THETA ** (-jnp.arange(0, half, dtype=F32) / half)
    ang_t = inv[:, None] * pos.astype(F32)[None, :]
    cos_t = jnp.tile(jnp.cos(ang_t), (2, 1))
    sin_t = jnp.tile(jnp.sin(ang_t), (2, 1))
    zpad = jnp.zeros((pos.shape[0], ROPE_PAD - rope), F32)
    cos = jnp.concatenate([cos_t.T, zpad], axis=1)
    sin = jnp.concatenate([sin_t.T, zpad], axis=1)
    tabs = (cos, sin, cos_t[:, :n_prompt], sin_t[:, :n_prompt])

    cache_rt = jnp.swapaxes(cache_krope, 2, 3)
    w1b = w_ff1.astype(BF16)
    w2b = w_ff2.astype(BF16)
    w_ob = w_o.astype(BF16)
    w_inb = w_in.astype(BF16)
    w_outb = w_out.astype(BF16)

    xs = (x_prompt.reshape(n_prompt, d), x_sample.reshape(n_sample, d))
    latents, new_u = None, []
    for i in range(depth):
        j = i // 2
        if i % 2 == 0:
            xs, latents = _mla_layer(xs, j, latents, n_prompt, seq, t_new, tabs, cache_ckv, cache_rt,
                                     norm_mix[i], w_dq[j], g_q[j], w_uq[j], w_dkv[j], g_kv[j],
                                     w_uk[j], w_uv[j], w_ob)
        else:
            xs, u = _conv_layer(xs[0], j, n_prompt, seq, t_new, state_conv[j], norm_mix[i], w_inb, conv_w[j],
                                w_outb)
            new_u.append(u)
        if i + 1 < depth:
            xs = (_ffn_layer(xs, i, norm_ffn[i], w1b, w2b, n_prompt),)
        else:
            y_prompt, y_sample = _ffn_layer(xs, i, norm_ffn[i], w1b, w2b, n_prompt, norm_final)

    def tails(u, lo, lead, inner, keep):
        return jnp.stack([lax.slice_in_dim(u, lo + (b + 1) * inner - keep, lo + (b + 1) * inner, axis=0)
                          for b in range(lead)])

    keep = conv_w.shape[1] - 1
    ckv_p, ckv_s, kr_p, kr_s = latents
    n_a = ckv_p.shape[0]
    u_p = jnp.stack([tails(u, 0, n_batch, seq, keep) for u in new_u])
    u_s = jnp.stack([tails(u, n_prompt, s_batch, t_new, keep) for u in new_u])
    return (y_prompt.reshape(n_batch, seq, d), y_sample.reshape(s_batch, t_new, d),
            ckv_p.reshape(n_a, n_batch, seq, -1), kr_p.reshape(n_a, n_batch, seq, -1), u_p,
            ckv_s.reshape(n_a, s_batch, t_new, -1), kr_s.reshape(n_a, s_batch, t_new, -1), u_s)
```

```python
import functools
import math

import jax
import jax.numpy as jnp
from jax import lax
from jax.experimental import pallas as pl
from jax.experimental.pallas import tpu as pltpu

CHUNK = 64
ROPE_THETA = 10000.0
EPS = 1e-6
HEAD_PAD = 256
ROPE_PAD = 128
VT_PAD = 144
VMEM_LIMIT = 56 * 1024 * 1024

F32 = jnp.float32
BF16 = jnp.bfloat16


def _params(*sem):
    return pltpu.CompilerParams(dimension_semantics=sem, vmem_limit_bytes=VMEM_LIMIT)


def _tile(total, pref):
    t = min(total, pref)
    assert total % t == 0, (total, pref)
    return t


def _resident(shape):
    nd = len(shape)
    return pl.BlockSpec(shape, lambda *_: (0,) * nd, pipeline_mode=pl.Buffered(1))


def _rms(xf, g):
    return xf * lax.rsqrt(jnp.mean(xf * xf, axis=-1, keepdims=True) + EPS) * g


def _dot(a, b):
    return jnp.dot(a, b, preferred_element_type=F32)


def _dot_nt(a, b):
    return lax.dot_general(a, b, (((1,), (1,)), ((), ())), preferred_element_type=F32)


def _load_rows(x_refs, np_tiles):
    if len(x_refs) == 1:
        return x_refs[0][...]
    return jnp.where(pl.program_id(0) < np_tiles, x_refs[0][...], x_refs[1][...])


def _store_rows(p_ref, s_ref, val, np_tiles, prev=None):
    is_prompt = pl.program_id(0) < np_tiles

    def put(o_ref, prev_ref):
        if prev is None:
            o_ref[...] = val
        else:
            if prev_ref is not None:
                o_ref[:-1] = prev_ref[...]
            o_ref[o_ref.shape[0] - 1] = val

    @pl.when(is_prompt)
    def _():
        put(p_ref, prev and prev[0])

    @pl.when(jnp.logical_not(is_prompt))
    def _():
        put(s_ref, prev and prev[1])


def _mla_down_kernel(*refs, n_x, n_prev, np_tiles, q_lora, kv_lora, rope, q_scale):
    x_refs, refs = refs[:n_x], refs[n_x:]
    g_ref, w_ref, gq_ref, gkv_ref, cos_ref, sin_ref = refs[:6]
    prev = refs[6:6 + n_prev] if n_prev else (None,) * 4
    cq_ref, cqt_ref, ckvb_ref, ckvt_ref, krb_ref, ckvp_ref, ckvs_ref, krp_ref, krs_ref = refs[6 + n_prev:]
    h = _rms(_load_rows(x_refs, np_tiles), g_ref[...]).astype(BF16)
    y = _dot(h, w_ref[...])
    cq = _rms(y[:, :q_lora], gq_ref[...]) * q_scale
    cq_ref[...] = cq.astype(BF16)
    cqt_ref[...] = cq.T.astype(BF16)
    ckv = _rms(y[:, q_lora:q_lora + kv_lora], gkv_ref[...])
    _store_rows(ckvp_ref, ckvs_ref, ckv, np_tiles, prev[0:2])
    ckvb_ref[...] = ckv.astype(BF16)
    ckvt_ref[...] = ckv.T.astype(BF16)
    o = q_lora + kv_lora
    kr = y[:, o:o + ROPE_PAD] * cos_ref[...] + y[:, o + ROPE_PAD:o + 2 * ROPE_PAD] * sin_ref[...]
    _store_rows(krp_ref, krs_ref, kr[:, :rope], np_tiles, prev[2:4])
    krb_ref[...] = kr.astype(BF16)


def _mla_q_kernel(cq_ref, wa_ref, wb_ref, cos_ref, sin_ref, q_ref, *, n_heads, nope):
    cq = cq_ref[...]
    cos = cos_ref[...]
    sin = sin_ref[...]
    for hp in range(n_heads // 2):
        a = _dot(cq, wa_ref[:, hp * 2 * HEAD_PAD:(hp + 1) * 2 * HEAD_PAD])
        b = _dot(cq, wb_ref[:, hp * 2 * ROPE_PAD:(hp + 1) * 2 * ROPE_PAD])
        for s in range(2):
            c0 = (2 * hp + s) * HEAD_PAD
            q_ref[:, c0:c0 + nope] = a[:, s * HEAD_PAD:s * HEAD_PAD + nope].astype(BF16)
            ar = a[:, s * HEAD_PAD + nope:(s + 1) * HEAD_PAD]
            br = b[:, s * ROPE_PAD:(s + 1) * ROPE_PAD]
            q_ref[:, c0 + nope:c0 + HEAD_PAD] = (ar * cos + br * sin).astype(BF16)


def _mla_qt_kernel(cqt_ref, wat_ref, wbt_ref, cost_ref, sint_ref, qt_ref, *, n_heads, nope):
    cqt = cqt_ref[...]
    cost = cost_ref[...]
    sint = sint_ref[...]
    rope = cost.shape[0]
    hd = nope + rope
    zeros = jnp.zeros((HEAD_PAD - hd, cqt.shape[1]), BF16)
    for hp in range(n_heads // 2):
        a = _dot(wat_ref[hp * 2 * hd:(hp + 1) * 2 * hd, :], cqt)
        b = _dot(wbt_ref[hp * 2 * rope:(hp + 1) * 2 * rope, :], cqt)
        for s in range(2):
            r0 = (2 * hp + s) * HEAD_PAD
            qt_ref[r0:r0 + nope, :] = a[s * hd:s * hd + nope].astype(BF16)
            ar = a[s * hd + nope:(s + 1) * hd]
            br = b[s * rope:(s + 1) * rope]
            qt_ref[r0 + nope:r0 + hd, :] = (ar * cost + br * sint).astype(BF16)
            qt_ref[r0 + hd:r0 + HEAD_PAD, :] = zeros


def _mla_kv_kernel(ckvb_ref, ckvt_ref, krb_ref, wk_ref, wvt_ref, k_ref, vt_ref, *, n_heads, nope):
    c = ckvb_ref[...]
    krb = krb_ref[...]
    tm = c.shape[0]
    vt = _dot(wvt_ref[...], ckvt_ref[...]).astype(BF16)
    ones_row = (lax.broadcasted_iota(jnp.int32, (VT_PAD - nope, tm), 0) == 0).astype(BF16)
    for h in range(n_heads):
        vt_ref[h * VT_PAD:h * VT_PAD + nope, :] = vt[h * nope:(h + 1) * nope]
        vt_ref[h * VT_PAD + nope:(h + 1) * VT_PAD, :] = ones_row
    for hp in range(n_heads // 2):
        kn = _dot(c, wk_ref[:, hp * 2 * nope:(hp + 1) * 2 * nope])
        for s in range(2):
            c0 = (2 * hp + s) * HEAD_PAD
            k_ref[:, c0:c0 + nope] = kn[:, s * nope:(s + 1) * nope].astype(BF16)
            k_ref[:, c0 + nope:c0 + HEAD_PAD] = krb


def _flash_kernel(qt_ref, k_ref, vt_ref, o_ref, s0_ref, *, nq, tq, kb, kd, cw, v_head):
    qi = pl.program_id(2)
    ncol = tq // cw
    main = tuple((off, kb, 0) for off in range(0, tq, kb))
    diag = ((0, kb, 0),) + tuple((off, kd, off // cw) for off in range(kb, tq, kd))

    def scores(base, sub, c):
        off, size, _ = sub
        start = pl.multiple_of(base + off, size)
        return _dot(k_ref[pl.ds(start, size), :], qt_ref[:, c * cw:(c + 1) * cw])

    def update(st, sc, base, sub, c, masked):
        off, size, _ = sub
        start = pl.multiple_of(base + off, size)
        if masked and off + size > c * cw:
            kc = (lax.broadcasted_iota(jnp.int32, (size, cw), 0) + off) // CHUNK
            qc = (lax.broadcasted_iota(jnp.int32, (size, cw), 1) + c * cw) // CHUNK
            sc = jnp.where(kc <= qc, sc, -jnp.inf)
        m_prev, acc_prev = st
        m_new = jnp.maximum(m_prev, jnp.max(sc, axis=0, keepdims=True))
        p = jnp.exp2(sc - m_new).astype(BF16)
        return m_new, jnp.exp2(m_prev - m_new) * acc_prev + _dot(vt_ref[:, pl.ds(start, size)], p)

    def trip(base, state, subs, masked, stage_next):
        state = list(state)
        cur = [None] * ncol
        for t, sub in enumerate(subs):
            nxt = [None] * ncol
            for c in range(sub[2], ncol):
                if t + 1 < len(subs) and c >= subs[t + 1][2]:
                    nxt[c] = scores(base, subs[t + 1], c)
                sc = s0_ref[:, c * cw:(c + 1) * cw] if cur[c] is None else cur[c]
                state[c] = update(state[c], sc, base, sub, c, masked)
                if t + 1 == len(subs) and stage_next:
                    s0_ref[:, c * cw:(c + 1) * cw] = scores(base + tq, main[0], c)
            cur = nxt
        return tuple(state)

    for c in range(ncol):
        s0_ref[:, c * cw:(c + 1) * cw] = scores(0, main[0], c)
    init = tuple((jnp.full((1, cw), -jnp.inf, F32), jnp.zeros((vt_ref.shape[0], cw), F32)) for _ in range(ncol))
    state = init
    if nq > 1:
        state = lax.fori_loop(0, qi, lambda j, st: trip(j * tq, st, main, False, True), init)
    state = trip(qi * tq, state, diag, True, False)
    for c in range(ncol):
        acc = state[c][1]
        o_ref[c * cw:(c + 1) * cw, :] = (acc[:v_head] / acc[v_head:v_head + 1]).T.astype(BF16)


def _lane_fold(a, op):
    out = a[:, :128]
    for i in range(1, a.shape[1] // 128):
        out = op(out, a[:, i * 128:(i + 1) * 128])
    return out


def _sample_attn_kernel(q_ref, wukt_ref, wuv_ref, cc_ref, crt_ref, nc_ref, nr_ref, o_ref,
                        ql_ref, qr_ref, cb_ref, s_ref, p_ref, *, n_heads, nope, rope, tk, t_new):
    past = cc_ref.shape[0]
    for h in range(n_heads):
        c0 = h * HEAD_PAD
        ql_ref[h * t_new:(h + 1) * t_new, :] = _dot(q_ref[:, c0:c0 + nope], wukt_ref[h]).astype(BF16)
        qr_ref[h * t_new:(h + 1) * t_new, :] = q_ref[:, c0 + nope:c0 + nope + rope]
    ql = ql_ref[...]
    qr = qr_ref[...]
    nc = nc_ref[...]
    s_new = _dot_nt(ql, nc) + _dot_nt(qr, nr_ref[:, :rope])
    rows = ql.shape[0]
    nchunk = past // tk
    half = nchunk // 2
    ks = [slice(j * tk, (j + 1) * tk) for j in range(nchunk)]

    def score(j, m_lane):
        cb = cc_ref[ks[j], :].astype(BF16)
        cb_ref[ks[j], :] = cb
        s = _dot_nt(ql, cb) + _dot(qr, crt_ref[:, ks[j]].astype(BF16))
        s_ref[:, ks[j]] = s
        return jnp.maximum(m_lane, _lane_fold(s, jnp.maximum))

    def probs(j, m, l_lane):
        p = jnp.exp2(s_ref[:, ks[j]] - m)
        p_ref[:, ks[j]] = p.astype(BF16)
        return l_lane + _lane_fold(p, jnp.add)

    neg = jnp.full((rows, 128), -jnp.inf, F32)
    zero = jnp.zeros((rows, 128), F32)
    m_lane = neg
    for j in range(half):
        m_lane = score(j, m_lane)
    m_a = jnp.max(m_lane, axis=-1, keepdims=True)
    m_lane, l_a = neg, zero
    for i, j in enumerate(range(half, nchunk)):
        m_lane = score(j, m_lane)
        if i < half:
            l_a = probs(i, m_a, l_a)
    m_b = jnp.maximum(jnp.max(m_lane, axis=-1, keepdims=True), jnp.max(s_new, axis=-1, keepdims=True))
    m = jnp.maximum(m_a, m_b)
    acc_a, l_b = jnp.zeros((rows, cb_ref.shape[1]), F32), zero
    for i, j in enumerate(range(half, nchunk)):
        l_b = probs(j, m, l_b)
        if i < half:
            acc_a = acc_a + _dot(p_ref[:, ks[i]], cb_ref[ks[i], :])
    p_new = jnp.exp2(s_new - m)
    acc_b = _dot(p_new.astype(BF16), nc)
    for j in range(half, nchunk):
        acc_b = acc_b + _dot(p_ref[:, ks[j]], cb_ref[ks[j], :])
    alpha = jnp.exp2(m_a - m)
    l = (alpha * jnp.sum(l_a, axis=-1, keepdims=True) + jnp.sum(l_b, axis=-1, keepdims=True)
         + jnp.sum(p_new, axis=-1, keepdims=True))
    ol = ((alpha * acc_a + acc_b) / l).astype(BF16)
    for h in range(n_heads):
        o_ref[:, h * nope:(h + 1) * nope] = _dot(ol[h * t_new:(h + 1) * t_new, :], wuv_ref[h]).astype(BF16)


def _proj_res_kernel(w_ref, *refs, n_a, n_x, np_tiles):
    a_refs, x_refs, o_ref = refs[:n_a], refs[n_a:n_a + n_x], refs[n_a + n_x]
    o_ref[...] = _load_rows(x_refs, np_tiles) + _dot(_load_rows(a_refs, np_tiles), w_ref[...])


def _ffn_accumulate(x_refs, g_ref, w1_ref, w2_ref, acc_ref, h_ref, np_tiles):
    @pl.when(pl.program_id(1) == 0)
    def _():
        x = _load_rows(x_refs, np_tiles)
        h_ref[...] = _rms(x, g_ref[...]).astype(BF16)
        acc_ref[...] = x

    a = jnp.maximum(_dot(h_ref[...], w1_ref[...]), 0.0)
    acc_ref[...] += _dot((a * a).astype(BF16), w2_ref[...])


def _ffn_kernel(*refs, n_x, np_tiles):
    x_refs, (g_ref, w1_ref, w2_ref, o_ref, h_ref) = refs[:n_x], refs[n_x:]
    _ffn_accumulate(x_refs, g_ref, w1_ref, w2_ref, o_ref, h_ref, np_tiles)


def _ffn_final_kernel(*refs, n_x, np_tiles):
    x_refs, (g_ref, w1_ref, w2_ref, gf_ref, yp_ref, ys_ref, h_ref, acc_ref) = refs[:n_x], refs[n_x:]
    _ffn_accumulate(x_refs, g_ref, w1_ref, w2_ref, acc_ref, h_ref, np_tiles)

    @pl.when(pl.program_id(1) == pl.num_programs(1) - 1)
    def _():
        _store_rows(yp_ref, ys_ref, _rms(acc_ref[...], gf_ref[...]), np_tiles)


def _conv_in_kernel(x_ref, g_ref, wb_ref, wc_ref, wv_ref, u_ref, gb_ref, h_ref):
    @pl.when(pl.program_id(1) == 0)
    def _():
        h_ref[...] = _rms(x_ref[...], g_ref[...]).astype(BF16)

    h = h_ref[...]
    gb_ref[...] = _dot(h, wb_ref[...]).astype(BF16)
    u_ref[...] = _dot(h, wc_ref[...]) * _dot(h, wv_ref[...])


def _conv_mix(u, p1, p2, first, second, gb, cw, wout, x):
    u1 = jnp.where(first, p1, pltpu.roll(u, 1, axis=0))
    u2 = jnp.where(first, p2, jnp.where(second, p1, pltpu.roll(u, 2, axis=0)))
    conv = cw[0:1] * u2 + cw[1:2] * u1 + cw[2:3] * u
    return x + _dot((gb.astype(F32) * conv).astype(BF16), wout)


def _conv_out_kernel(u_ref, halo_ref, st_ref, gb_ref, x_ref, cw_ref, wout_ref, o_ref,
                     *, np_tiles, tiles_per_seq, t_new):
    i = pl.program_id(0)
    tm, d = u_ref.shape
    row = lax.broadcasted_iota(jnp.int32, (tm, 1), 0)

    @pl.when(i < np_tiles)
    def _():
        has_past = i % tiles_per_seq > 0
        halo = halo_ref[...]
        p1 = jnp.where(has_past, halo[7:8], 0.0)
        p2 = jnp.where(has_past, halo[6:7], 0.0)
        o_ref[...] = _conv_mix(u_ref[...], p1, p2, row == 0, row == 1, gb_ref[...], cw_ref[...],
                               wout_ref[...], x_ref[...])

    @pl.when(i >= np_tiles)
    def _():
        nb = tm // t_new
        st = st_ref[...]
        p1 = jnp.broadcast_to(st[:, 1:2, :], (nb, t_new, d)).reshape(tm, d)
        p2 = jnp.broadcast_to(st[:, 0:1, :], (nb, t_new, d)).reshape(tm, d)
        t = row % t_new
        o_ref[...] = _conv_mix(u_ref[...], p1, p2, t == 0, t == 1, gb_ref[...], cw_ref[...],
                               wout_ref[...], x_ref[...])


def _row_spec(tm, n, off=0):
    return pl.BlockSpec((tm, n), lambda i, *_: (i + off, 0))


def _col_spec(n, tm):
    return pl.BlockSpec((n, tm), lambda i, *_: (0, i))


def _split_row_specs(tm, n, np_tiles, layers=None):
    def p_map(i, *_):
        return (jnp.minimum(i, np_tiles - 1), 0)

    def s_map(i, *_):
        return (jnp.maximum(i - np_tiles, 0), 0)

    if layers is None:
        return [pl.BlockSpec((tm, n), p_map), pl.BlockSpec((tm, n), s_map)]
    return [pl.BlockSpec((layers, tm, n), lambda i, *_: (0,) + p_map(i)),
            pl.BlockSpec((layers, tm, n), lambda i, *_: (0,) + s_map(i))]


def _mla_layer(xs, j, prev, n_prompt, seq, t_new, tabs, cache_c, cache_rt, g_mix, w_dq, g_q, w_uq, w_dkv, g_kv,
               w_uk, w_uv, w_ob):
    cos, sin, cost, sint = tabs
    t_all, d = sum(a.shape[0] for a in xs), xs[0].shape[1]
    q_lora = w_dq.shape[1]
    kv_lora, n_heads, nope = w_uk.shape
    rope = w_dkv.shape[1] - kv_lora
    half = rope // 2
    q_scale = float((nope + rope) ** -0.5 * math.log2(math.e))
    n_sample = t_all - n_prompt

    wr = w_dkv[:, kv_lora:]
    zr = jnp.zeros((d, ROPE_PAD - rope), F32)
    w_down = jnp.concatenate(
        [w_dq, w_dkv[:, :kv_lora], wr, zr, -wr[:, half:], wr[:, :half], zr], axis=1).astype(BF16)
    wq = w_uq.reshape(q_lora, n_heads, nope + rope)
    wq_r = wq[..., nope:]
    w_qa = jnp.concatenate(
        [wq, jnp.zeros((q_lora, n_heads, HEAD_PAD - nope - rope), F32)], axis=-1
    ).reshape(q_lora, n_heads * HEAD_PAD).astype(BF16)
    w_qb = jnp.concatenate(
        [-wq_r[..., half:], wq_r[..., :half], jnp.zeros((q_lora, n_heads, ROPE_PAD - rope), F32)], axis=-1
    ).reshape(q_lora, n_heads * ROPE_PAD).astype(BF16)
    w_k = w_uk.reshape(kv_lora, n_heads * nope).astype(BF16)
    w_vt = w_uv.reshape(kv_lora, n_heads * nope).T.astype(BF16)
    w_ukt = jnp.transpose(w_uk, (1, 2, 0)).astype(BF16)
    w_uvh = jnp.transpose(w_uv, (1, 0, 2)).astype(BF16)

    tm = _tile(n_sample, 512)
    np_tiles = n_prompt // tm
    n_down = q_lora + kv_lora + 2 * ROPE_PAD
    x_specs = _split_row_specs(tm, d, np_tiles) if len(xs) == 2 else [_row_spec(tm, d)]
    widths = (kv_lora, kv_lora, rope, rope)
    rows = (n_prompt, n_sample, n_prompt, n_sample)
    prev_specs = []
    if prev is not None:
        prev_specs = _split_row_specs(tm, kv_lora, np_tiles, j) + _split_row_specs(tm, rope, np_tiles, j)
    cq, cqt, ckvb, ckvt, krb, *new = pl.pallas_call(
        functools.partial(_mla_down_kernel, n_x=len(xs), n_prev=len(prev_specs), np_tiles=np_tiles,
                          q_lora=q_lora, kv_lora=kv_lora, rope=rope, q_scale=q_scale),
        grid=(t_all // tm,),
        in_specs=x_specs + [_resident((1, d)), _resident((d, n_down)), _resident((1, q_lora)),
                            _resident((1, kv_lora)), _row_spec(tm, ROPE_PAD), _row_spec(tm, ROPE_PAD)]
        + prev_specs,
        out_specs=[_row_spec(tm, q_lora), _col_spec(q_lora, tm), _row_spec(tm, kv_lora),
                   _col_spec(kv_lora, tm), _row_spec(tm, ROPE_PAD)]
        + _split_row_specs(tm, kv_lora, np_tiles, j + 1) + _split_row_specs(tm, rope, np_tiles, j + 1),
        out_shape=[jax.ShapeDtypeStruct((t_all, q_lora), BF16), jax.ShapeDtypeStruct((q_lora, t_all), BF16),
                   jax.ShapeDtypeStruct((t_all, kv_lora), BF16), jax.ShapeDtypeStruct((kv_lora, t_all), BF16),
                   jax.ShapeDtypeStruct((t_all, ROPE_PAD), BF16)]
        + [jax.ShapeDtypeStruct((j + 1, r, w), F32) for r, w in zip(rows, widths)],
        compiler_params=_params("arbitrary"), name="mla_down",
    )(*xs, g_mix.reshape(1, d), w_down, g_q.reshape(1, q_lora), g_kv.reshape(1, kv_lora), cos, sin,
      *(prev or ()))

    tp = _tile(n_prompt, 512)
    w_qat = w_uq.T.astype(BF16)
    w_qbt = jnp.concatenate([-wq_r[..., half:], wq_r[..., :half]], axis=-1
                            ).reshape(q_lora, n_heads * rope).T.astype(BF16)
    qt = pl.pallas_call(
        functools.partial(_mla_qt_kernel, n_heads=n_heads, nope=nope),
        grid=(n_prompt // tp,),
        in_specs=[_col_spec(q_lora, tp), _resident(w_qat.shape), _resident(w_qbt.shape),
                  _col_spec(rope, tp), _col_spec(rope, tp)],
        out_specs=_col_spec(n_heads * HEAD_PAD, tp),
        out_shape=jax.ShapeDtypeStruct((n_heads * HEAD_PAD, n_prompt), BF16),
        compiler_params=_params("parallel"), name="mla_qt",
    )(cqt, w_qat, w_qbt, cost, sint)

    ts = _tile(n_sample, 512)
    off = n_prompt // ts
    q_s = pl.pallas_call(
        functools.partial(_mla_q_kernel, n_heads=n_heads, nope=nope),
        grid=(n_sample // ts,),
        in_specs=[_row_spec(ts, q_lora, off), _resident(w_qa.shape), _resident(w_qb.shape),
                  _row_spec(ts, ROPE_PAD, off), _row_spec(ts, ROPE_PAD, off)],
        out_specs=_row_spec(ts, n_heads * HEAD_PAD),
        out_shape=jax.ShapeDtypeStruct((n_sample, n_heads * HEAD_PAD), BF16),
        compiler_params=_params("parallel"), name="mla_q",
    )(cq, w_qa, w_qb, cos, sin)

    k, vt = pl.pallas_call(
        functools.partial(_mla_kv_kernel, n_heads=n_heads, nope=nope),
        grid=(n_prompt // tp,),
        in_specs=[_row_spec(tp, kv_lora), _col_spec(kv_lora, tp), _row_spec(tp, ROPE_PAD),
                  _resident(w_k.shape), _resident(w_vt.shape)],
        out_specs=[_row_spec(tp, n_heads * HEAD_PAD), _col_spec(n_heads * VT_PAD, tp)],
        out_shape=[jax.ShapeDtypeStruct((n_prompt, n_heads * HEAD_PAD), BF16),
                   jax.ShapeDtypeStruct((n_heads * VT_PAD, n_prompt), BF16)],
        compiler_params=_params("parallel"), name="mla_kv",
    )(ckvb, ckvt, krb, w_k, w_vt)

    blk = _tile(seq, 4096)
    kb = _tile(blk, 256)
    nq = seq // blk
    n_batch = n_prompt // seq
    n_o = n_heads * nope
    o_p = pl.pallas_call(
        functools.partial(_flash_kernel, nq=nq, tq=blk, kb=kb, kd=_tile(kb, 256), cw=_tile(blk, 256),
                          v_head=nope),
        grid=(n_batch, n_heads, nq),
        in_specs=[pl.BlockSpec((HEAD_PAD, blk), lambda b, h, i: (h, b * nq + i)),
                  pl.BlockSpec((seq, HEAD_PAD), lambda b, h, i: (b, h)),
                  pl.BlockSpec((VT_PAD, seq), lambda b, h, i: (h, b))],
        out_specs=pl.BlockSpec((blk, nope), lambda b, h, i: (b * nq + i, h)),
        out_shape=jax.ShapeDtypeStruct((n_prompt, n_o), BF16),
        scratch_shapes=[pltpu.VMEM((kb, blk), F32)],
        compiler_params=_params("parallel", "parallel", "arbitrary"), name="mla_flash",
    )(qt, k, vt)

    past = cache_c.shape[2]
    n_sb = n_sample // t_new
    row0 = n_prompt // t_new
    tk = _tile(past, 512)
    o_s = pl.pallas_call(
        functools.partial(_sample_attn_kernel, n_heads=n_heads, nope=nope, rope=rope, tk=tk, t_new=t_new),
        grid=(n_sb,),
        in_specs=[pl.BlockSpec((t_new, n_heads * HEAD_PAD), lambda b: (b, 0)),
                  _resident(w_ukt.shape), _resident(w_uvh.shape),
                  pl.BlockSpec((None, None, past, kv_lora), lambda b: (j, b, 0, 0)),
                  pl.BlockSpec((None, None, rope, past), lambda b: (j, b, 0, 0)),
                  pl.BlockSpec((t_new, kv_lora), lambda b: (row0 + b, 0)),
                  pl.BlockSpec((t_new, ROPE_PAD), lambda b: (row0 + b, 0))],
        out_specs=pl.BlockSpec((t_new, n_o), lambda b: (b, 0)),
        out_shape=jax.ShapeDtypeStruct((n_sample, n_o), BF16),
        scratch_shapes=[pltpu.VMEM((n_heads * t_new, kv_lora), BF16), pltpu.VMEM((n_heads * t_new, rope), BF16),
                        pltpu.VMEM((past, kv_lora), BF16), pltpu.VMEM((n_heads * t_new, past), F32),
                        pltpu.VMEM((n_heads * t_new, past), BF16)],
        compiler_params=_params("parallel"), name="mla_sample_attn",
    )(q_s, w_ukt, w_uvh, cache_c, cache_rt, ckvb, krb)

    x = pl.pallas_call(
        functools.partial(_proj_res_kernel, n_a=2, n_x=len(xs), np_tiles=np_tiles),
        grid=(t_all // tm,),
        in_specs=[pl.BlockSpec((None, n_o, d), lambda i: (j, 0, 0), pipeline_mode=pl.Buffered(1))]
        + _split_row_specs(tm, n_o, np_tiles) + x_specs,
        out_specs=_row_spec(tm, d),
        out_shape=jax.ShapeDtypeStruct((t_all, d), F32),
        compiler_params=_params("arbitrary"), name="mla_out",
    )(w_ob, o_p, o_s, *xs)
    return (x,), new


def _conv_layer(x, j, n_prompt, seq, t_new, state, g_mix, w_inb, conv_w, w_outb):
    assert conv_w.shape[0] == 3 and state.shape[1] == 2
    t_all, d = x.shape
    tm = _tile(t_all, 512)
    tn = _tile(d, 1024)
    nn = d // tn
    w_out_spec = pl.BlockSpec((None, d, d), lambda *_: (j, 0, 0), pipeline_mode=pl.Buffered(1))
    u, gb = pl.pallas_call(
        _conv_in_kernel,
        grid=(t_all // tm, nn),
        in_specs=[pl.BlockSpec((tm, d), lambda i, n: (i, 0)), _resident((1, d)),
                  pl.BlockSpec((None, d, tn), lambda i, n: (j, 0, n)),
                  pl.BlockSpec((None, d, tn), lambda i, n: (j, 0, nn + n)),
                  pl.BlockSpec((None, d, tn), lambda i, n: (j, 0, 2 * nn + n))],
        out_specs=[pl.BlockSpec((tm, tn), lambda i, n: (i, n)), pl.BlockSpec((tm, tn), lambda i, n: (i, n))],
        out_shape=[jax.ShapeDtypeStruct((t_all, d), F32), jax.ShapeDtypeStruct((t_all, d), BF16)],
        scratch_shapes=[pltpu.VMEM((tm, d), BF16)],
        compiler_params=_params("parallel", "arbitrary"), name="conv_in",
    )(x, g_mix.reshape(1, d), w_inb, w_inb, w_inb)

    tc = _tile(min(seq, t_all - n_prompt), 512)
    assert seq % tc == 0 and tc % t_new == 0 and tc % 8 == 0
    np_tiles = n_prompt // tc
    hb = tc // 8
    nb = tc // t_new
    x_new = pl.pallas_call(
        functools.partial(_conv_out_kernel, np_tiles=np_tiles, tiles_per_seq=seq // tc, t_new=t_new),
        grid=(t_all // tc,),
        in_specs=[_row_spec(tc, d),
                  pl.BlockSpec((8, d), lambda i: (jnp.maximum(i * hb - 1, 0), 0)),
                  pl.BlockSpec((nb, state.shape[1], d), lambda i: (jnp.maximum(i - np_tiles, 0), 0, 0)),
                  _row_spec(tc, d), _row_spec(tc, d), _resident((conv_w.shape[0], d)), w_out_spec],
        out_specs=_row_spec(tc, d),
        out_shape=jax.ShapeDtypeStruct((t_all, d), F32),
        compiler_params=_params("parallel"), name="conv_out",
    )(u, u, state, gb, x, conv_w, w_outb)
    return (x_new,), u


def _ffn_layer(xs, layer, g, w1b, w2b, n_prompt, g_final=None):
    t_all, d = sum(a.shape[0] for a in xs), xs[0].shape[1]
    f = w1b.shape[2]
    tm = _tile(t_all - n_prompt, 1024 if g_final is None else 512)
    tf = _tile(f, 1024 * 512 // tm)
    np_tiles = n_prompt // tm
    x_specs = _split_row_specs(tm, d, np_tiles) if len(xs) == 2 else [_row_spec(tm, d)]
    in_specs = x_specs + [_resident((1, d)), pl.BlockSpec((None, d, tf), lambda i, c: (layer, 0, c)),
                          pl.BlockSpec((None, tf, d), lambda i, c: (layer, c, 0))]
    grid = (t_all // tm, f // tf)
    h_scratch = pltpu.VMEM((tm, d), BF16)
    if g_final is None:
        return pl.pallas_call(
            functools.partial(_ffn_kernel, n_x=len(xs), np_tiles=np_tiles),
            grid=grid, in_specs=in_specs, out_specs=_row_spec(tm, d),
            out_shape=jax.ShapeDtypeStruct((t_all, d), F32), scratch_shapes=[h_scratch],
            compiler_params=_params("arbitrary", "arbitrary"), name="ffn",
        )(*xs, g.reshape(1, d), w1b, w2b)
    return pl.pallas_call(
        functools.partial(_ffn_final_kernel, n_x=len(xs), np_tiles=np_tiles),
        grid=grid, in_specs=in_specs + [_resident((1, d))],
        out_specs=_split_row_specs(tm, d, np_tiles),
        out_shape=[jax.ShapeDtypeStruct((n_prompt, d), F32), jax.ShapeDtypeStruct((t_all - n_prompt, d), F32)],
        scratch_shapes=[h_scratch, pltpu.VMEM((tm, d), F32)],
        compiler_params=_params("arbitrary", "arbitrary"), name="ffn_final",
    )(*xs, g.reshape(1, d), w1b, w2b, g_final.reshape(1, d))


def kernel(x_prompt, x_sample, cache_ckv, cache_krope, state_conv, norm_mix, w_dq, g_q, w_uq, w_dkv, g_kv,
           w_uk, w_uv, w_o, w_in, conv_w, w_out, norm_ffn, w_ff1, w_ff2, norm_final):
    n_batch, seq, d = x_prompt.shape
    s_batch, t_new, _ = x_sample.shape
    past = cache_ckv.shape[2]
    depth = norm_mix.shape[0]
    kv_lora = w_uk.shape[1]
    rope = w_dkv.shape[2] - kv_lora
    half = rope // 2
    n_prompt = n_batch * seq
    n_sample = s_batch * t_new

    pos = jnp.concatenate([jnp.tile(jnp.arange(seq, dtype=jnp.int32), n_batch),
                           jnp.tile(past + jnp.arange(t_new, dtype=jnp.int32), s_batch)])
    inv = ROPE_THETA ** (-jnp.arange(0, half, dtype=F32) / half)
    ang_t = inv[:, None] * pos.astype(F32)[None, :]
    cos_t = jnp.tile(jnp.cos(ang_t), (2, 1))
    sin_t = jnp.tile(jnp.sin(ang_t), (2, 1))
    zpad = jnp.zeros((pos.shape[0], ROPE_PAD - rope), F32)
    cos = jnp.concatenate([cos_t.T, zpad], axis=1)
    sin = jnp.concatenate([sin_t.T, zpad], axis=1)
    tabs = (cos, sin, cos_t[:, :n_prompt], sin_t[:, :n_prompt])

    cache_rt = jnp.swapaxes(cache_krope, 2, 3)
    w1b = w_ff1.astype(BF16)
    w2b = w_ff2.astype(BF16)
    w_ob = w_o.astype(BF16)
    w_inb = w_in.astype(BF16)
    w_outb = w_out.astype(BF16)

    xs = (x_prompt.reshape(n_prompt, d), x_sample.reshape(n_sample, d))
    latents, new_u = None, []
    for i in range(depth):
        j = i // 2
        if i % 2 == 0:
            xs, latents = _mla_layer(xs, j, latents, n_prompt, seq, t_new, tabs, cache_ckv, cache_rt,
                                     norm_mix[i], w_dq[j], g_q[j], w_uq[j], w_dkv[j], g_kv[j],
                                     w_uk[j], w_uv[j], w_ob)
        else:
            xs, u = _conv_layer(xs[0], j, n_prompt, seq, t_new, state_conv[j], norm_mix[i], w_inb, conv_w[j],
                                w_outb)
            new_u.append(u)
        if i + 1 < depth:
            xs = (_ffn_layer(xs, i, norm_ffn[i], w1b, w2b, n_prompt),)
        else:
            y_prompt, y_sample = _ffn_layer(xs, i, norm_ffn[i], w1b, w2b, n_prompt, norm_final)

    def tails(u, lo, lead, inner, keep):
        return jnp.stack([lax.slice_in_dim(u, lo + (b + 1) * inner - keep, lo + (b + 1) * inner, axis=0)
                          for b in range(lead)])

    keep = conv_w.shape[1] - 1
    ckv_p, ckv_s, kr_p, kr_s = latents
    n_a = ckv_p.shape[0]
    u_p = jnp.stack([tails(u, 0, n_batch, seq, keep) for u in new_u])
    u_s = jnp.stack([tails(u, n_prompt, s_batch, t_new, keep) for u in new_u])
    return (y_prompt.reshape(n_batch, seq, d), y_sample.reshape(s_batch, t_new, d),
            ckv_p.reshape(n_a, n_batch, seq, -1), kr_p.reshape(n_a, n_batch, seq, -1), u_p,
            ckv_s.reshape(n_a, s_batch, t_new, -1), kr_s.reshape(n_a, s_batch, t_new, -1), u_s)
```

```python
import functools
import math

import jax
import jax.numpy as jnp
from jax import lax
from jax.experimental import pallas as pl
from jax.experimental.pallas import tpu as pltpu

CHUNK = 64
ROPE_THETA = 10000.0
EPS = 1e-6
HEAD_PAD = 256
ROPE_PAD = 128
VT_PAD = 144
VMEM_LIMIT = 56 * 1024 * 1024

F32 = jnp.float32
BF16 = jnp.bfloat16


def _params(*sem):
    return pltpu.CompilerParams(dimension_semantics=sem, vmem_limit_bytes=VMEM_LIMIT)


def _tile(total, pref):
    t = min(total, pref)
    assert total % t == 0, (total, pref)
    return t


def _resident(shape):
    nd = len(shape)
    return pl.BlockSpec(shape, lambda *_: (0,) * nd, pipeline_mode=pl.Buffered(1))


def _rms(xf, g):
    return xf * lax.rsqrt(jnp.mean(xf * xf, axis=-1, keepdims=True) + EPS) * g


def _dot(a, b):
    return jnp.dot(a, b, preferred_element_type=F32)


def _dot_nt(a, b):
    return lax.dot_general(a, b, (((1,), (1,)), ((), ())), preferred_element_type=F32)


def _load_rows(x_refs, np_tiles):
    if len(x_refs) == 1:
        return x_refs[0][...]
    return jnp.where(pl.program_id(0) < np_tiles, x_refs[0][...], x_refs[1][...])


def _store_rows(p_ref, s_ref, val, np_tiles, prev=None):
    is_prompt = pl.program_id(0) < np_tiles

    def put(o_ref, prev_ref):
        if prev is None:
            o_ref[...] = val
        else:
            if prev_ref is not None:
                o_ref[:-1] = prev_ref[...]
            o_ref[o_ref.shape[0] - 1] = val

    @pl.when(is_prompt)
    def _():
        put(p_ref, prev and prev[0])

    @pl.when(jnp.logical_not(is_prompt))
    def _():
        put(s_ref, prev and prev[1])


def _mla_down_kernel(*refs, n_x, n_prev, np_tiles, q_lora, kv_lora, rope, q_scale):
    x_refs, refs = refs[:n_x], refs[n_x:]
    g_ref, w_ref, gq_ref, gkv_ref, cos_ref, sin_ref = refs[:6]
    prev = refs[6:6 + n_prev] if n_prev else (None,) * 4
    cq_ref, cqt_ref, ckvb_ref, ckvt_ref, krb_ref, ckvp_ref, ckvs_ref, krp_ref, krs_ref = refs[6 + n_prev:]
    h = _rms(_load_rows(x_refs, np_tiles), g_ref[...]).astype(BF16)
    y = _dot(h, w_ref[...])
    cq = _rms(y[:, :q_lora], gq_ref[...]) * q_scale
    cq_ref[...] = cq.astype(BF16)
    cqt_ref[...] = cq.T.astype(BF16)
    ckv = _rms(y[:, q_lora:q_lora + kv_lora], gkv_ref[...])
    _store_rows(ckvp_ref, ckvs_ref, ckv, np_tiles, prev[0:2])
    ckvb_ref[...] = ckv.astype(BF16)
    ckvt_ref[...] = ckv.T.astype(BF16)
    o = q_lora + kv_lora
    kr = y[:, o:o + ROPE_PAD] * cos_ref[...] + y[:, o + ROPE_PAD:o + 2 * ROPE_PAD] * sin_ref[...]
    _store_rows(krp_ref, krs_ref, kr[:, :rope], np_tiles, prev[2:4])
    krb_ref[...] = kr.astype(BF16)


def _mla_q_kernel(cq_ref, wa_ref, wb_ref, cos_ref, sin_ref, q_ref, *, n_heads, nope):
    cq = cq_ref[...]
    cos = cos_ref[...]
    sin = sin_ref[...]
    for hp in range(n_heads // 2):
        a = _dot(cq, wa_ref[:, hp * 2 * HEAD_PAD:(hp + 1) * 2 * HEAD_PAD])
        b = _dot(cq, wb_ref[:, hp * 2 * ROPE_PAD:(hp + 1) * 2 * ROPE_PAD])
        for s in range(2):
            c0 = (2 * hp + s) * HEAD_PAD
            q_ref[:, c0:c0 + nope] = a[:, s * HEAD_PAD:s * HEAD_PAD + nope].astype(BF16)
            ar = a[:, s * HEAD_PAD + nope:(s + 1) * HEAD_PAD]
            br = b[:, s * ROPE_PAD:(s + 1) * ROPE_PAD]
            q_ref[:, c0 + nope:c0 + HEAD_PAD] = (ar * cos + br * sin).astype(BF16)


def _mla_qt_kernel(cqt_ref, wat_ref, wbt_ref, cost_ref, sint_ref, qt_ref, *, n_heads, nope):
    cqt = cqt_ref[...]
    cost = cost_ref[...]
    sint = sint_ref[...]
    rope = cost.shape[0]
    hd = nope + rope
    zeros = jnp.zeros((HEAD_PAD - hd, cqt.shape[1]), BF16)
    for hp in range(n_heads // 2):
        a = _dot(wat_ref[hp * 2 * hd:(hp + 1) * 2 * hd, :], cqt)
        b = _dot(wbt_ref[hp * 2 * rope:(hp + 1) * 2 * rope, :], cqt)
        for s in range(2):
            r0 = (2 * hp + s) * HEAD_PAD
            qt_ref[r0:r0 + nope, :] = a[s * hd:s * hd + nope].astype(BF16)
            ar = a[s * hd + nope:(s + 1) * hd]
            br = b[s * rope:(s + 1) * rope]
            qt_ref[r0 + nope:r0 + hd, :] = (ar * cost + br * sint).astype(BF16)
            qt_ref[r0 + hd:r0 + HEAD_PAD, :] = zeros


def _mla_kv_kernel(ckvb_ref, ckvt_ref, krb_ref, wk_ref, wvt_ref, k_ref, vt_ref, *, n_heads, nope):
    c = ckvb_ref[...]
    krb = krb_ref[...]
    tm = c.shape[0]
    vt = _dot(wvt_ref[...], ckvt_ref[...]).astype(BF16)
    ones_row = (lax.broadcasted_iota(jnp.int32, (VT_PAD - nope, tm), 0) == 0).astype(BF16)
    for h in range(n_heads):
        vt_ref[h * VT_PAD:h * VT_PAD + nope, :] = vt[h * nope:(h + 1) * nope]
        vt_ref[h * VT_PAD + nope:(h + 1) * VT_PAD, :] = ones_row
    for hp in range(n_heads // 2):
        kn = _dot(c, wk_ref[:, hp * 2 * nope:(hp + 1) * 2 * nope])
        for s in range(2):
            c0 = (2 * hp + s) * HEAD_PAD
            k_ref[:, c0:c0 + nope] = kn[:, s * nope:(s + 1) * nope].astype(BF16)
            k_ref[:, c0 + nope:c0 + HEAD_PAD] = krb


def _cast_chunk(src_refs, dst_refs, step, n_chunks):
    @pl.when(step < n_chunks)
    def _():
        for src, dst in zip(src_refs, dst_refs):
            dst[...] = src[...].astype(BF16)


def _flash_kernel(qt_ref, k_ref, vt_ref, w1_ref, w2_ref, o_ref, w1b_ref, w2b_ref, s0_ref,
                  *, nq, tq, kb, kd, cw, v_head, n_chunks):
    step = (pl.program_id(0) * pl.num_programs(1) + pl.program_id(1)) * nq + pl.program_id(2)
    _cast_chunk((w1_ref, w2_ref), (w1b_ref, w2b_ref), step, n_chunks)
    qi = pl.program_id(2)
    ncol = tq // cw
    main = tuple((off, kb, 0) for off in range(0, tq, kb))
    diag = ((0, kb, 0),) + tuple((off, kd, off // cw) for off in range(kb, tq, kd))

    def scores(base, sub, c):
        off, size, _ = sub
        start = pl.multiple_of(base + off, size)
        return _dot(k_ref[pl.ds(start, size), :], qt_ref[:, c * cw:(c + 1) * cw])

    def update(st, sc, base, sub, c, masked):
        off, size, _ = sub
        start = pl.multiple_of(base + off, size)
        if masked and off + size > c * cw:
            kc = (lax.broadcasted_iota(jnp.int32, (size, cw), 0) + off) // CHUNK
            qc = (lax.broadcasted_iota(jnp.int32, (size, cw), 1) + c * cw) // CHUNK
            sc = jnp.where(kc <= qc, sc, -jnp.inf)
        m_prev, acc_prev = st
        m_new = jnp.maximum(m_prev, jnp.max(sc, axis=0, keepdims=True))
        p = jnp.exp2(sc - m_new).astype(BF16)
        return m_new, jnp.exp2(m_prev - m_new) * acc_prev + _dot(vt_ref[:, pl.ds(start, size)], p)

    def trip(base, state, subs, masked, stage_next):
        state = list(state)
        cur = [None] * ncol
        for t, sub in enumerate(subs):
            nxt = [None] * ncol
            for c in range(sub[2], ncol):
                if t + 1 < len(subs) and c >= subs[t + 1][2]:
                    nxt[c] = scores(base, subs[t + 1], c)
                sc = s0_ref[:, c * cw:(c + 1) * cw] if cur[c] is None else cur[c]
                state[c] = update(state[c], sc, base, sub, c, masked)
                if t + 1 == len(subs) and stage_next:
                    s0_ref[:, c * cw:(c + 1) * cw] = scores(base + tq, main[0], c)
            cur = nxt
        return tuple(state)

    for c in range(ncol):
        s0_ref[:, c * cw:(c + 1) * cw] = scores(0, main[0], c)
    init = tuple((jnp.full((1, cw), -jnp.inf, F32), jnp.zeros((vt_ref.shape[0], cw), F32)) for _ in range(ncol))
    state = init
    if nq > 1:
        state = lax.fori_loop(0, qi, lambda j, st: trip(j * tq, st, main, False, True), init)
    state = trip(qi * tq, state, diag, True, False)
    for c in range(ncol):
        acc = state[c][1]
        o_ref[c * cw:(c + 1) * cw, :] = (acc[:v_head] / acc[v_head:v_head + 1]).T.astype(BF16)


def _lane_fold(a, op):
    out = a[:, :128]
    for i in range(1, a.shape[1] // 128):
        out = op(out, a[:, i * 128:(i + 1) * 128])
    return out


def _sample_attn_kernel(q_ref, wukt_ref, wuv_ref, cc_ref, crt_ref, nc_ref, nr_ref, o_ref,
                        ql_ref, qr_ref, cb_ref, s_ref, p_ref, *, n_heads, nope, rope, tk, t_new):
    past = cc_ref.shape[0]
    for h in range(n_heads):
        c0 = h * HEAD_PAD
        ql_ref[h * t_new:(h + 1) * t_new, :] = _dot(q_ref[:, c0:c0 + nope], wukt_ref[h]).astype(BF16)
        qr_ref[h * t_new:(h + 1) * t_new, :] = q_ref[:, c0 + nope:c0 + nope + rope]
    ql = ql_ref[...]
    qr = qr_ref[...]
    nc = nc_ref[...]
    s_new = _dot_nt(ql, nc) + _dot_nt(qr, nr_ref[:, :rope])
    rows = ql.shape[0]
    nchunk = past // tk
    half = nchunk // 2
    ks = [slice(j * tk, (j + 1) * tk) for j in range(nchunk)]

    def score(j, m_lane):
        cb = cc_ref[ks[j], :].astype(BF16)
        cb_ref[ks[j], :] = cb
        s = _dot_nt(ql, cb) + _dot(qr, crt_ref[:, ks[j]].astype(BF16))
        s_ref[:, ks[j]] = s
        return jnp.maximum(m_lane, _lane_fold(s, jnp.maximum))

    def probs(j, m, l_lane):
        p = jnp.exp2(s_ref[:, ks[j]] - m)
        p_ref[:, ks[j]] = p.astype(BF16)
        return l_lane + _lane_fold(p, jnp.add)

    neg = jnp.full((rows, 128), -jnp.inf, F32)
    zero = jnp.zeros((rows, 128), F32)
    m_lane = neg
    for j in range(half):
        m_lane = score(j, m_lane)
    m_a = jnp.max(m_lane, axis=-1, keepdims=True)
    m_lane, l_a = neg, zero
    for i, j in enumerate(range(half, nchunk)):
        m_lane = score(j, m_lane)
        if i < half:
            l_a = probs(i, m_a, l_a)
    m_b = jnp.maximum(jnp.max(m_lane, axis=-1, keepdims=True), jnp.max(s_new, axis=-1, keepdims=True))
    m = jnp.maximum(m_a, m_b)
    acc_a, l_b = jnp.zeros((rows, cb_ref.shape[1]), F32), zero
    for i, j in enumerate(range(half, nchunk)):
        l_b = probs(j, m, l_b)
        if i < half:
            acc_a = acc_a + _dot(p_ref[:, ks[i]], cb_ref[ks[i], :])
    p_new = jnp.exp2(s_new - m)
    acc_b = _dot(p_new.astype(BF16), nc)
    for j in range(half, nchunk):
        acc_b = acc_b + _dot(p_ref[:, ks[j]], cb_ref[ks[j], :])
    alpha = jnp.exp2(m_a - m)
    l = (alpha * jnp.sum(l_a, axis=-1, keepdims=True) + jnp.sum(l_b, axis=-1, keepdims=True)
         + jnp.sum(p_new, axis=-1, keepdims=True))
    ol = ((alpha * acc_a + acc_b) / l).astype(BF16)
    for h in range(n_heads):
        o_ref[:, h * nope:(h + 1) * nope] = _dot(ol[h * t_new:(h + 1) * t_new, :], wuv_ref[h]).astype(BF16)


def _proj_res_kernel(w_ref, *refs, n_a, n_x, np_tiles):
    a_refs, x_refs, o_ref = refs[:n_a], refs[n_a:n_a + n_x], refs[n_a + n_x]
    o_ref[...] = _load_rows(x_refs, np_tiles) + _dot(_load_rows(a_refs, np_tiles), w_ref[...])


def _ffn_accumulate(x_refs, g_ref, w1_ref, w2_ref, acc_ref, h_ref, np_tiles):
    @pl.when(pl.program_id(1) == 0)
    def _():
        x = _load_rows(x_refs, np_tiles)
        h_ref[...] = _rms(x, g_ref[...]).astype(BF16)
        acc_ref[...] = x

    a = jnp.maximum(_dot(h_ref[...], w1_ref[...]), 0.0)
    acc_ref[...] += _dot((a * a).astype(BF16), w2_ref[...])


def _ffn_kernel(*refs, n_x, np_tiles):
    x_refs, (g_ref, w1_ref, w2_ref, o_ref, h_ref) = refs[:n_x], refs[n_x:]
    _ffn_accumulate(x_refs, g_ref, w1_ref, w2_ref, o_ref, h_ref, np_tiles)


def _ffn_final_kernel(*refs, n_x, np_tiles):
    x_refs, (g_ref, w1_ref, w2_ref, gf_ref, yp_ref, ys_ref, h_ref, acc_ref) = refs[:n_x], refs[n_x:]
    _ffn_accumulate(x_refs, g_ref, w1_ref, w2_ref, acc_ref, h_ref, np_tiles)

    @pl.when(pl.program_id(1) == pl.num_programs(1) - 1)
    def _():
        _store_rows(yp_ref, ys_ref, _rms(acc_ref[...], gf_ref[...]), np_tiles)


def _conv_in_kernel(x_ref, g_ref, wb_ref, wc_ref, wv_ref, w1_ref, w2_ref, u_ref, gb_ref, w1b_ref, w2b_ref,
                    h_ref, *, n_chunks):
    step = pl.program_id(0) * pl.num_programs(1) + pl.program_id(1)
    _cast_chunk((w1_ref, w2_ref), (w1b_ref, w2b_ref), step, n_chunks)

    @pl.when(pl.program_id(1) == 0)
    def _():
        h_ref[...] = _rms(x_ref[...], g_ref[...]).astype(BF16)

    h = h_ref[...]
    gb_ref[...] = _dot(h, wb_ref[...]).astype(BF16)
    u_ref[...] = _dot(h, wc_ref[...]) * _dot(h, wv_ref[...])


def _conv_mix(u, p1, p2, first, second, gb, cw, wout, x):
    u1 = jnp.where(first, p1, pltpu.roll(u, 1, axis=0))
    u2 = jnp.where(first, p2, jnp.where(second, p1, pltpu.roll(u, 2, axis=0)))
    conv = cw[0:1] * u2 + cw[1:2] * u1 + cw[2:3] * u
    return x + _dot((gb.astype(F32) * conv).astype(BF16), wout)


def _conv_out_kernel(u_ref, halo_ref, st_ref, gb_ref, x_ref, cw_ref, wout_ref, o_ref,
                     *, np_tiles, tiles_per_seq, t_new):
    i = pl.program_id(0)
    tm, d = u_ref.shape
    row = lax.broadcasted_iota(jnp.int32, (tm, 1), 0)

    @pl.when(i < np_tiles)
    def _():
        has_past = i % tiles_per_seq > 0
        halo = halo_ref[...]
        p1 = jnp.where(has_past, halo[7:8], 0.0)
        p2 = jnp.where(has_past, halo[6:7], 0.0)
        o_ref[...] = _conv_mix(u_ref[...], p1, p2, row == 0, row == 1, gb_ref[...], cw_ref[...],
                               wout_ref[...], x_ref[...])

    @pl.when(i >= np_tiles)
    def _():
        nb = tm // t_new
        st = st_ref[...]
        p1 = jnp.broadcast_to(st[:, 1:2, :], (nb, t_new, d)).reshape(tm, d)
        p2 = jnp.broadcast_to(st[:, 0:1, :], (nb, t_new, d)).reshape(tm, d)
        t = row % t_new
        o_ref[...] = _conv_mix(u_ref[...], p1, p2, t == 0, t == 1, gb_ref[...], cw_ref[...],
                               wout_ref[...], x_ref[...])


def _row_spec(tm, n, off=0):
    return pl.BlockSpec((tm, n), lambda i, *_: (i + off, 0))


def _col_spec(n, tm):
    return pl.BlockSpec((n, tm), lambda i, *_: (0, i))


def _cast_plan(w1, w2, layer, steps, step_of):
    _, d, f = w1.shape
    n_chunks = max(c for c in range(1, steps + 1) if f % c == 0 and (f // c) % 128 == 0)
    fc = f // n_chunks

    def chunk(*ids):
        return jnp.minimum(step_of(*ids), n_chunks - 1)

    in_specs = [pl.BlockSpec((None, d, fc), lambda *ids: (layer, 0, chunk(*ids))),
                pl.BlockSpec((None, fc, d), lambda *ids: (layer, chunk(*ids), 0))]
    out_specs = [pl.BlockSpec((d, fc), lambda *ids: (0, chunk(*ids))),
                 pl.BlockSpec((fc, d), lambda *ids: (chunk(*ids), 0))]
    out_shape = [jax.ShapeDtypeStruct((d, f), BF16), jax.ShapeDtypeStruct((f, d), BF16)]
    return n_chunks, in_specs, out_specs, out_shape


def _split_row_specs(tm, n, np_tiles, layers=None):
    def p_map(i, *_):
        return (jnp.minimum(i, np_tiles - 1), 0)

    def s_map(i, *_):
        return (jnp.maximum(i - np_tiles, 0), 0)

    if layers is None:
        return [pl.BlockSpec((tm, n), p_map), pl.BlockSpec((tm, n), s_map)]
    return [pl.BlockSpec((layers, tm, n), lambda i, *_: (0,) + p_map(i)),
            pl.BlockSpec((layers, tm, n), lambda i, *_: (0,) + s_map(i))]


def _mla_layer(xs, j, prev, n_prompt, seq, t_new, tabs, cache_c, cache_rt, g_mix, w_dq, g_q, w_uq, w_dkv, g_kv,
               w_uk, w_uv, w_ob, w_ff1, w_ff2, ffn_layer):
    cos, sin, cost, sint = tabs
    t_all, d = sum(a.shape[0] for a in xs), xs[0].shape[1]
    q_lora = w_dq.shape[1]
    kv_lora, n_heads, nope = w_uk.shape
    rope = w_dkv.shape[1] - kv_lora
    half = rope // 2
    q_scale = float((nope + rope) ** -0.5 * math.log2(math.e))
    n_sample = t_all - n_prompt

    wr = w_dkv[:, kv_lora:]
    zr = jnp.zeros((d, ROPE_PAD - rope), F32)
    w_down = jnp.concatenate(
        [w_dq, w_dkv[:, :kv_lora], wr, zr, -wr[:, half:], wr[:, :half], zr], axis=1).astype(BF16)
    wq = w_uq.reshape(q_lora, n_heads, nope + rope)
    wq_r = wq[..., nope:]
    w_qa = jnp.concatenate(
        [wq, jnp.zeros((q_lora, n_heads, HEAD_PAD - nope - rope), F32)], axis=-1
    ).reshape(q_lora, n_heads * HEAD_PAD).astype(BF16)
    w_qb = jnp.concatenate(
        [-wq_r[..., half:], wq_r[..., :half], jnp.zeros((q_lora, n_heads, ROPE_PAD - rope), F32)], axis=-1
    ).reshape(q_lora, n_heads * ROPE_PAD).astype(BF16)
    w_k = w_uk.reshape(kv_lora, n_heads * nope).astype(BF16)
    w_vt = w_uv.reshape(kv_lora, n_heads * nope).T.astype(BF16)
    w_ukt = jnp.transpose(w_uk, (1, 2, 0)).astype(BF16)
    w_uvh = jnp.transpose(w_uv, (1, 0, 2)).astype(BF16)

    tm = _tile(n_sample, 512)
    np_tiles = n_prompt // tm
    n_down = q_lora + kv_lora + 2 * ROPE_PAD
    x_specs = _split_row_specs(tm, d, np_tiles) if len(xs) == 2 else [_row_spec(tm, d)]
    widths = (kv_lora, kv_lora, rope, rope)
    rows = (n_prompt, n_sample, n_prompt, n_sample)
    prev_specs = []
    if prev is not None:
        prev_specs = _split_row_specs(tm, kv_lora, np_tiles, j) + _split_row_specs(tm, rope, np_tiles, j)
    cq, cqt, ckvb, ckvt, krb, *new = pl.pallas_call(
        functools.partial(_mla_down_kernel, n_x=len(xs), n_prev=len(prev_specs), np_tiles=np_tiles,
                          q_lora=q_lora, kv_lora=kv_lora, rope=rope, q_scale=q_scale),
        grid=(t_all // tm,),
        in_specs=x_specs + [_resident((1, d)), _resident((d, n_down)), _resident((1, q_lora)),
                            _resident((1, kv_lora)), _row_spec(tm, ROPE_PAD), _row_spec(tm, ROPE_PAD)]
        + prev_specs,
        out_specs=[_row_spec(tm, q_lora), _col_spec(q_lora, tm), _row_spec(tm, kv_lora),
                   _col_spec(kv_lora, tm), _row_spec(tm, ROPE_PAD)]
        + _split_row_specs(tm, kv_lora, np_tiles, j + 1) + _split_row_specs(tm, rope, np_tiles, j + 1),
        out_shape=[jax.ShapeDtypeStruct((t_all, q_lora), BF16), jax.ShapeDtypeStruct((q_lora, t_all), BF16),
                   jax.ShapeDtypeStruct((t_all, kv_lora), BF16), jax.ShapeDtypeStruct((kv_lora, t_all), BF16),
                   jax.ShapeDtypeStruct((t_all, ROPE_PAD), BF16)]
        + [jax.ShapeDtypeStruct((j + 1, r, w), F32) for r, w in zip(rows, widths)],
        compiler_params=_params("arbitrary"), name="mla_down",
    )(*xs, g_mix.reshape(1, d), w_down, g_q.reshape(1, q_lora), g_kv.reshape(1, kv_lora), cos, sin,
      *(prev or ()))

    tp = _tile(n_prompt, 512)
    w_qat = w_uq.T.astype(BF16)
    w_qbt = jnp.concatenate([-wq_r[..., half:], wq_r[..., :half]], axis=-1
                            ).reshape(q_lora, n_heads * rope).T.astype(BF16)
    qt = pl.pallas_call(
        functools.partial(_mla_qt_kernel, n_heads=n_heads, nope=nope),
        grid=(n_prompt // tp,),
        in_specs=[_col_spec(q_lora, tp), _resident(w_qat.shape), _resident(w_qbt.shape),
                  _col_spec(rope, tp), _col_spec(rope, tp)],
        out_specs=_col_spec(n_heads * HEAD_PAD, tp),
        out_shape=jax.ShapeDtypeStruct((n_heads * HEAD_PAD, n_prompt), BF16),
        compiler_params=_params("parallel"), name="mla_qt",
    )(cqt, w_qat, w_qbt, cost, sint)

    ts = _tile(n_sample, 512)
    off = n_prompt // ts
    q_s = pl.pallas_call(
        functools.partial(_mla_q_kernel, n_heads=n_heads, nope=nope),
        grid=(n_sample // ts,),
        in_specs=[_row_spec(ts, q_lora, off), _resident(w_qa.shape), _resident(w_qb.shape),
                  _row_spec(ts, ROPE_PAD, off), _row_spec(ts, ROPE_PAD, off)],
        out_specs=_row_spec(ts, n_heads * HEAD_PAD),
        out_shape=jax.ShapeDtypeStruct((n_sample, n_heads * HEAD_PAD), BF16),
        compiler_params=_params("parallel"), name="mla_q",
    )(cq, w_qa, w_qb, cos, sin)

    k, vt = pl.pallas_call(
        functools.partial(_mla_kv_kernel, n_heads=n_heads, nope=nope),
        grid=(n_prompt // tp,),
        in_specs=[_row_spec(tp, kv_lora), _col_spec(kv_lora, tp), _row_spec(tp, ROPE_PAD),
                  _resident(w_k.shape), _resident(w_vt.shape)],
        out_specs=[_row_spec(tp, n_heads * HEAD_PAD), _col_spec(n_heads * VT_PAD, tp)],
        out_shape=[jax.ShapeDtypeStruct((n_prompt, n_heads * HEAD_PAD), BF16),
                   jax.ShapeDtypeStruct((n_heads * VT_PAD, n_prompt), BF16)],
        compiler_params=_params("parallel"), name="mla_kv",
    )(ckvb, ckvt, krb, w_k, w_vt)

    blk = _tile(seq, 4096)
    kb = _tile(blk, 256)
    nq = seq // blk
    n_batch = n_prompt // seq
    n_o = n_heads * nope
    n_chunks, cast_in, cast_out, cast_shape = _cast_plan(
        w_ff1, w_ff2, ffn_layer, n_batch * n_heads * nq, lambda b, h, i: (b * n_heads + h) * nq + i)
    o_p, w1b, w2b = pl.pallas_call(
        functools.partial(_flash_kernel, nq=nq, tq=blk, kb=kb, kd=_tile(kb, 256), cw=_tile(blk, 256),
                          v_head=nope, n_chunks=n_chunks),
        grid=(n_batch, n_heads, nq),
        in_specs=[pl.BlockSpec((HEAD_PAD, blk), lambda b, h, i: (h, b * nq + i)),
                  pl.BlockSpec((seq, HEAD_PAD), lambda b, h, i: (b, h)),
                  pl.BlockSpec((VT_PAD, seq), lambda b, h, i: (h, b))] + cast_in,
        out_specs=[pl.BlockSpec((blk, nope), lambda b, h, i: (b * nq + i, h))] + cast_out,
        out_shape=[jax.ShapeDtypeStruct((n_prompt, n_o), BF16)] + cast_shape,
        scratch_shapes=[pltpu.VMEM((kb, blk), F32)],
        compiler_params=_params("arbitrary", "arbitrary", "arbitrary"), name="mla_flash",
    )(qt, k, vt, w_ff1, w_ff2)

    past = cache_c.shape[2]
    n_sb = n_sample // t_new
    row0 = n_prompt // t_new
    tk = _tile(past, 512)
    o_s = pl.pallas_call(
        functools.partial(_sample_attn_kernel, n_heads=n_heads, nope=nope, rope=rope, tk=tk, t_new=t_new),
        grid=(n_sb,),
        in_specs=[pl.BlockSpec((t_new, n_heads * HEAD_PAD), lambda b: (b, 0)),
                  _resident(w_ukt.shape), _resident(w_uvh.shape),
                  pl.BlockSpec((None, None, past, kv_lora), lambda b: (j, b, 0, 0)),
                  pl.BlockSpec((None, None, rope, past), lambda b: (j, b, 0, 0)),
                  pl.BlockSpec((t_new, kv_lora), lambda b: (row0 + b, 0)),
                  pl.BlockSpec((t_new, ROPE_PAD), lambda b: (row0 + b, 0))],
        out_specs=pl.BlockSpec((t_new, n_o), lambda b: (b, 0)),
        out_shape=jax.ShapeDtypeStruct((n_sample, n_o), BF16),
        scratch_shapes=[pltpu.VMEM((n_heads * t_new, kv_lora), BF16), pltpu.VMEM((n_heads * t_new, rope), BF16),
                        pltpu.VMEM((past, kv_lora), BF16), pltpu.VMEM((n_heads * t_new, past), F32),
                        pltpu.VMEM((n_heads * t_new, past), BF16)],
        compiler_params=_params("parallel"), name="mla_sample_attn",
    )(q_s, w_ukt, w_uvh, cache_c, cache_rt, ckvb, krb)

    x = pl.pallas_call(
        functools.partial(_proj_res_kernel, n_a=2, n_x=len(xs), np_tiles=np_tiles),
        grid=(t_all // tm,),
        in_specs=[pl.BlockSpec((None, n_o, d), lambda i: (j, 0, 0), pipeline_mode=pl.Buffered(1))]
        + _split_row_specs(tm, n_o, np_tiles) + x_specs,
        out_specs=_row_spec(tm, d),
        out_shape=jax.ShapeDtypeStruct((t_all, d), F32),
        compiler_params=_params("arbitrary"), name="mla_out",
    )(w_ob, o_p, o_s, *xs)
    return (x,), new, (w1b, w2b)


def _conv_layer(x, j, n_prompt, seq, t_new, state, g_mix, w_inb, conv_w, w_outb, w_ff1, w_ff2, ffn_layer):
    assert conv_w.shape[0] == 3 and state.shape[1] == 2
    t_all, d = x.shape
    tm = _tile(t_all, 512)
    tn = _tile(d, 1024)
    nn = d // tn
    w_out_spec = pl.BlockSpec((None, d, d), lambda *_: (j, 0, 0), pipeline_mode=pl.Buffered(1))
    n_chunks, cast_in, cast_out, cast_shape = _cast_plan(
        w_ff1, w_ff2, ffn_layer, (t_all // tm) * nn, lambda i, n: i * nn + n)
    u, gb, w1b, w2b = pl.pallas_call(
        functools.partial(_conv_in_kernel, n_chunks=n_chunks),
        grid=(t_all // tm, nn),
        in_specs=[pl.BlockSpec((tm, d), lambda i, n: (i, 0)), _resident((1, d)),
                  pl.BlockSpec((None, d, tn), lambda i, n: (j, 0, n)),
                  pl.BlockSpec((None, d, tn), lambda i, n: (j, 0, nn + n)),
                  pl.BlockSpec((None, d, tn), lambda i, n: (j, 0, 2 * nn + n))] + cast_in,
        out_specs=[pl.BlockSpec((tm, tn), lambda i, n: (i, n)), pl.BlockSpec((tm, tn), lambda i, n: (i, n))]
        + cast_out,
        out_shape=[jax.ShapeDtypeStruct((t_all, d), F32), jax.ShapeDtypeStruct((t_all, d), BF16)] + cast_shape,
        scratch_shapes=[pltpu.VMEM((tm, d), BF16)],
        compiler_params=_params("arbitrary", "arbitrary"), name="conv_in",
    )(x, g_mix.reshape(1, d), w_inb, w_inb, w_inb, w_ff1, w_ff2)

    tc = _tile(min(seq, t_all - n_prompt), 512)
    assert seq % tc == 0 and tc % t_new == 0 and tc % 8 == 0
    np_tiles = n_prompt // tc
    hb = tc // 8
    nb = tc // t_new
    x_new = pl.pallas_call(
        functools.partial(_conv_out_kernel, np_tiles=np_tiles, tiles_per_seq=seq // tc, t_new=t_new),
        grid=(t_all // tc,),
        in_specs=[_row_spec(tc, d),
                  pl.BlockSpec((8, d), lambda i: (jnp.maximum(i * hb - 1, 0), 0)),
                  pl.BlockSpec((nb, state.shape[1], d), lambda i: (jnp.maximum(i - np_tiles, 0), 0, 0)),
                  _row_spec(tc, d), _row_spec(tc, d), _resident((conv_w.shape[0], d)), w_out_spec],
        out_specs=_row_spec(tc, d),
        out_shape=jax.ShapeDtypeStruct((t_all, d), F32),
        compiler_params=_params("parallel"), name="conv_out",
    )(u, u, state, gb, x, conv_w, w_outb)
    return (x_new,), u, (w1b, w2b)


def _ffn_layer(xs, g, w1b, w2b, n_prompt, g_final=None):
    t_all, d = sum(a.shape[0] for a in xs), xs[0].shape[1]
    f = w1b.shape[1]
    tm = _tile(t_all - n_prompt, 512)
    tf = _tile(f, 1024)
    np_tiles = n_prompt // tm
    x_specs = _split_row_specs(tm, d, np_tiles) if len(xs) == 2 else [_row_spec(tm, d)]
    in_specs = x_specs + [_resident((1, d)), pl.BlockSpec((d, tf), lambda i, c: (0, c)),
                          pl.BlockSpec((tf, d), lambda i, c: (c, 0))]
    grid = (t_all // tm, f // tf)
    h_scratch = pltpu.VMEM((tm, d), BF16)
    if g_final is None:
        return pl.pallas_call(
            functools.partial(_ffn_kernel, n_x=len(xs), np_tiles=np_tiles),
            grid=grid, in_specs=in_specs, out_specs=_row_spec(tm, d),
            out_shape=jax.ShapeDtypeStruct((t_all, d), F32), scratch_shapes=[h_scratch],
            compiler_params=_params("arbitrary", "arbitrary"), name="ffn",
        )(*xs, g.reshape(1, d), w1b, w2b)
    return pl.pallas_call(
        functools.partial(_ffn_final_kernel, n_x=len(xs), np_tiles=np_tiles),
        grid=grid, in_specs=in_specs + [_resident((1, d))],
        out_specs=_split_row_specs(tm, d, np_tiles),
        out_shape=[jax.ShapeDtypeStruct((n_prompt, d), F32), jax.ShapeDtypeStruct((t_all - n_prompt, d), F32)],
        scratch_shapes=[h_scratch, pltpu.VMEM((tm, d), F32)],
        compiler_params=_params("arbitrary", "arbitrary"), name="ffn_final",
    )(*xs, g.reshape(1, d), w1b, w2b, g_final.reshape(1, d))


def kernel(x_prompt, x_sample, cache_ckv, cache_krope, state_conv, norm_mix, w_dq, g_q, w_uq, w_dkv, g_kv,
           w_uk, w_uv, w_o, w_in, conv_w, w_out, norm_ffn, w_ff1, w_ff2, norm_final):
    n_batch, seq, d = x_prompt.shape
    s_batch, t_new, _ = x_sample.shape
    past = cache_ckv.shape[2]
    depth = norm_mix.shape[0]
    kv_lora = w_uk.shape[1]
    rope = w_dkv.shape[2] - kv_lora
    half = rope // 2
    n_prompt = n_batch * seq
    n_sample = s_batch * t_new

    pos = jnp.concatenate([jnp.tile(jnp.arange(seq, dtype=jnp.int32), n_batch),
                           jnp.tile(past + jnp.arange(t_new, dtype=jnp.int32), s_batch)])
    inv = ROPE_THETA ** (-jnp.arange(0, half, dtype=F32) / half)
    ang_t = inv[:, None] * pos.astype(F32)[None, :]
    cos_t = jnp.tile(jnp.cos(ang_t), (2, 1))
    sin_t = jnp.tile(jnp.sin(ang_t), (2, 1))
    zpad = jnp.zeros((pos.shape[0], ROPE_PAD - rope), F32)
    cos = jnp.concatenate([cos_t.T, zpad], axis=1)
    sin = jnp.concatenate([sin_t.T, zpad], axis=1)
    tabs = (cos, sin, cos_t[:, :n_prompt], sin_t[:, :n_prompt])

    cache_rt = jnp.swapaxes(cache_krope, 2, 3)
    w_ob = w_o.astype(BF16)
    w_inb = w_in.astype(BF16)
    w_outb = w_out.astype(BF16)

    xs = (x_prompt.reshape(n_prompt, d), x_sample.reshape(n_sample, d))
    latents, new_u = None, []
    for i in range(depth):
        j = i // 2
        if i % 2 == 0:
            xs, latents, (w1b, w2b) = _mla_layer(
                xs, j, latents, n_prompt, seq, t_new, tabs, cache_ckv, cache_rt, norm_mix[i], w_dq[j], g_q[j],
                w_uq[j], w_dkv[j], g_kv[j], w_uk[j], w_uv[j], w_ob, w_ff1, w_ff2, i)
        else:
            xs, u, (w1b, w2b) = _conv_layer(xs[0], j, n_prompt, seq, t_new, state_conv[j], norm_mix[i], w_inb,
                                            conv_w[j], w_outb, w_ff1, w_ff2, i)
            new_u.append(u)
        if i + 1 < depth:
            xs = (_ffn_layer(xs, norm_ffn[i], w1b, w2b, n_prompt),)
        else:
            y_prompt, y_sample = _ffn_layer(xs, norm_ffn[i], w1b, w2b, n_prompt, norm_final)

    def tails(u, lo, lead, inner, keep):
        return jnp.stack([lax.slice_in_dim(u, lo + (b + 1) * inner - keep, lo + (b + 1) * inner, axis=0)
                          for b in range(lead)])

    keep = conv_w.shape[1] - 1
    ckv_p, ckv_s, kr_p, kr_s = latents
    n_a = ckv_p.shape[0]
    u_p = jnp.stack([tails(u, 0, n_batch, seq, keep) for u in new_u])
    u_s = jnp.stack([tails(u, n_prompt, s_batch, t_new, keep) for u in new_u])
    return (y_prompt.reshape(n_batch, seq, d), y_sample.reshape(s_batch, t_new, d),
            ckv_p.reshape(n_a, n_batch, seq, -1), kr_p.reshape(n_a, n_batch, seq, -1), u_p,
            ckv_s.reshape(n_a, s_batch, t_new, -1), kr_s.reshape(n_a, s_batch, t_new, -1), u_s)
```

```python
import functools
import math

import jax
import jax.numpy as jnp
from jax import lax
from jax.experimental import pallas as pl
from jax.experimental.pallas import tpu as pltpu

CHUNK = 64
ROPE_THETA = 10000.0
EPS = 1e-6
HEAD_PAD = 256
ROPE_PAD = 128
VT_PAD = 144
VMEM_LIMIT = 56 * 1024 * 1024

F32 = jnp.float32
BF16 = jnp.bfloat16


def _params(*sem):
    return pltpu.CompilerParams(dimension_semantics=sem, vmem_limit_bytes=VMEM_LIMIT)


def _tile(total, pref):
    t = min(total, pref)
    assert total % t == 0, (total, pref)
    return t


def _resident(shape):
    nd = len(shape)
    return pl.BlockSpec(shape, lambda *_: (0,) * nd, pipeline_mode=pl.Buffered(1))


def _rms(xf, g):
    return xf * lax.rsqrt(jnp.mean(xf * xf, axis=-1, keepdims=True) + EPS) * g


def _dot(a, b):
    return jnp.dot(a, b, preferred_element_type=F32)


def _dot_nt(a, b):
    return lax.dot_general(a, b, (((1,), (1,)), ((), ())), preferred_element_type=F32)


def _load_rows(x_refs, np_tiles):
    if len(x_refs) == 1:
        return x_refs[0][...]
    return jnp.where(pl.program_id(0) < np_tiles, x_refs[0][...], x_refs[1][...])


def _store_rows(p_ref, s_ref, val, np_tiles, prev=None):
    is_prompt = pl.program_id(0) < np_tiles

    def put(o_ref, prev_ref):
        if prev is None:
            o_ref[...] = val
        else:
            if prev_ref is not None:
                o_ref[:-1] = prev_ref[...]
            o_ref[o_ref.shape[0] - 1] = val

    @pl.when(is_prompt)
    def _():
        put(p_ref, prev and prev[0])

    @pl.when(jnp.logical_not(is_prompt))
    def _():
        put(s_ref, prev and prev[1])


def _mla_down_kernel(*refs, n_x, n_prev, np_tiles, q_lora, kv_lora, rope, q_scale):
    x_refs, refs = refs[:n_x], refs[n_x:]
    g_ref, w_ref, gq_ref, gkv_ref, cos_ref, sin_ref = refs[:6]
    prev = refs[6:6 + n_prev] if n_prev else (None,) * 4
    cq_ref, cqt_ref, ckvb_ref, ckvt_ref, krb_ref, ckvp_ref, ckvs_ref, krp_ref, krs_ref = refs[6 + n_prev:]
    h = _rms(_load_rows(x_refs, np_tiles), g_ref[...]).astype(BF16)
    y = _dot(h, w_ref[...])
    cq = _rms(y[:, :q_lora], gq_ref[...]) * q_scale
    cq_ref[...] = cq.astype(BF16)
    cqt_ref[...] = cq.T.astype(BF16)
    ckv = _rms(y[:, q_lora:q_lora + kv_lora], gkv_ref[...])
    _store_rows(ckvp_ref, ckvs_ref, ckv, np_tiles, prev[0:2])
    ckvb_ref[...] = ckv.astype(BF16)
    ckvt_ref[...] = ckv.T.astype(BF16)
    o = q_lora + kv_lora
    kr = y[:, o:o + ROPE_PAD] * cos_ref[...] + y[:, o + ROPE_PAD:o + 2 * ROPE_PAD] * sin_ref[...]
    _store_rows(krp_ref, krs_ref, kr[:, :rope], np_tiles, prev[2:4])
    krb_ref[...] = kr.astype(BF16)


def _mla_q_kernel(cq_ref, wa_ref, wb_ref, cos_ref, sin_ref, q_ref, *, n_heads, nope):
    cq = cq_ref[...]
    cos = cos_ref[...]
    sin = sin_ref[...]
    for hp in range(n_heads // 2):
        a = _dot(cq, wa_ref[:, hp * 2 * HEAD_PAD:(hp + 1) * 2 * HEAD_PAD])
        b = _dot(cq, wb_ref[:, hp * 2 * ROPE_PAD:(hp + 1) * 2 * ROPE_PAD])
        for s in range(2):
            c0 = (2 * hp + s) * HEAD_PAD
            q_ref[:, c0:c0 + nope] = a[:, s * HEAD_PAD:s * HEAD_PAD + nope].astype(BF16)
            ar = a[:, s * HEAD_PAD + nope:(s + 1) * HEAD_PAD]
            br = b[:, s * ROPE_PAD:(s + 1) * ROPE_PAD]
            q_ref[:, c0 + nope:c0 + HEAD_PAD] = (ar * cos + br * sin).astype(BF16)


def _mla_qt_kernel(cqt_ref, wat_ref, wbt_ref, cost_ref, sint_ref, qt_ref, *, n_heads, nope):
    cqt = cqt_ref[...]
    cost = cost_ref[...]
    sint = sint_ref[...]
    rope = cost.shape[0]
    hd = nope + rope
    zeros = jnp.zeros((HEAD_PAD - hd, cqt.shape[1]), BF16)
    for hp in range(n_heads // 2):
        a = _dot(wat_ref[hp * 2 * hd:(hp + 1) * 2 * hd, :], cqt)
        b = _dot(wbt_ref[hp * 2 * rope:(hp + 1) * 2 * rope, :], cqt)
        for s in range(2):
            r0 = (2 * hp + s) * HEAD_PAD
            qt_ref[r0:r0 + nope, :] = a[s * hd:s * hd + nope].astype(BF16)
            ar = a[s * hd + nope:(s + 1) * hd]
            br = b[s * rope:(s + 1) * rope]
            qt_ref[r0 + nope:r0 + hd, :] = (ar * cost + br * sint).astype(BF16)
            qt_ref[r0 + hd:r0 + HEAD_PAD, :] = zeros


def _mla_kv_kernel(ckvb_ref, ckvt_ref, krb_ref, wk_ref, wvt_ref, k_ref, vt_ref, *, n_heads, nope):
    c = ckvb_ref[...]
    krb = krb_ref[...]
    tm = c.shape[0]
    vt = _dot(wvt_ref[...], ckvt_ref[...]).astype(BF16)
    ones_row = (lax.broadcasted_iota(jnp.int32, (VT_PAD - nope, tm), 0) == 0).astype(BF16)
    for h in range(n_heads):
        vt_ref[h * VT_PAD:h * VT_PAD + nope, :] = vt[h * nope:(h + 1) * nope]
        vt_ref[h * VT_PAD + nope:(h + 1) * VT_PAD, :] = ones_row
    for hp in range(n_heads // 2):
        kn = _dot(c, wk_ref[:, hp * 2 * nope:(hp + 1) * 2 * nope])
        for s in range(2):
            c0 = (2 * hp + s) * HEAD_PAD
            k_ref[:, c0:c0 + nope] = kn[:, s * nope:(s + 1) * nope].astype(BF16)
            k_ref[:, c0 + nope:c0 + HEAD_PAD] = krb


def _cast_chunk(src_refs, dst_refs, step, n_chunks):
    if isinstance(n_chunks, int):
        n_chunks = (n_chunks,) * len(src_refs)
    for src, dst, n in zip(src_refs, dst_refs, n_chunks):
        @pl.when(step < n)
        def _(src=src, dst=dst):
            dst[...] = src[...].astype(BF16)


def _flash_kernel(qt_ref, k_ref, vt_ref, *refs, nq, tq, kb, kd, cw, v_head, n_chunks):
    n_jobs = len(n_chunks)
    src_refs, o_ref, dst_refs, s0_ref = refs[:n_jobs], refs[n_jobs], refs[n_jobs + 1:2 * n_jobs + 1], refs[-1]
    step = (pl.program_id(0) * pl.num_programs(1) + pl.program_id(1)) * nq + pl.program_id(2)
    _cast_chunk(src_refs, dst_refs, step, n_chunks)
    qi = pl.program_id(2)
    ncol = tq // cw
    main = tuple((off, kb, 0) for off in range(0, tq, kb))
    diag = ((0, kb, 0),) + tuple((off, kd, off // cw) for off in range(kb, tq, kd))

    def scores(base, sub, c):
        off, size, _ = sub
        start = pl.multiple_of(base + off, size)
        return _dot(k_ref[pl.ds(start, size), :], qt_ref[:, c * cw:(c + 1) * cw])

    def update(st, sc, base, sub, c, masked):
        off, size, _ = sub
        start = pl.multiple_of(base + off, size)
        if masked and off + size > c * cw:
            kc = (lax.broadcasted_iota(jnp.int32, (size, cw), 0) + off) // CHUNK
            qc = (lax.broadcasted_iota(jnp.int32, (size, cw), 1) + c * cw) // CHUNK
            sc = jnp.where(kc <= qc, sc, -jnp.inf)
        m_prev, acc_prev = st
        m_new = jnp.maximum(m_prev, jnp.max(sc, axis=0, keepdims=True))
        p = jnp.exp2(sc - m_new).astype(BF16)
        return m_new, jnp.exp2(m_prev - m_new) * acc_prev + _dot(vt_ref[:, pl.ds(start, size)], p)

    def trip(base, state, subs, masked, stage_next):
        state = list(state)
        cur = [None] * ncol
        for t, sub in enumerate(subs):
            nxt = [None] * ncol
            for c in range(sub[2], ncol):
                if t + 1 < len(subs) and c >= subs[t + 1][2]:
                    nxt[c] = scores(base, subs[t + 1], c)
                sc = s0_ref[:, c * cw:(c + 1) * cw] if cur[c] is None else cur[c]
                state[c] = update(state[c], sc, base, sub, c, masked)
                if t + 1 == len(subs) and stage_next:
                    s0_ref[:, c * cw:(c + 1) * cw] = scores(base + tq, main[0], c)
            cur = nxt
        return tuple(state)

    for c in range(ncol):
        s0_ref[:, c * cw:(c + 1) * cw] = scores(0, main[0], c)
    init = tuple((jnp.full((1, cw), -jnp.inf, F32), jnp.zeros((vt_ref.shape[0], cw), F32)) for _ in range(ncol))
    state = init
    if nq > 1:
        state = lax.fori_loop(0, qi, lambda j, st: trip(j * tq, st, main, False, True), init)
    state = trip(qi * tq, state, diag, True, False)
    for c in range(ncol):
        acc = state[c][1]
        o_ref[c * cw:(c + 1) * cw, :] = (acc[:v_head] / acc[v_head:v_head + 1]).T.astype(BF16)


def _lane_fold(a, op):
    out = a[:, :128]
    for i in range(1, a.shape[1] // 128):
        out = op(out, a[:, i * 128:(i + 1) * 128])
    return out


def _sample_attn_kernel(q_ref, wukt_ref, wuv_ref, cc_ref, crt_ref, nc_ref, nr_ref, o_ref,
                        ql_ref, qr_ref, cb_ref, s_ref, p_ref, *, n_heads, nope, rope, tk, t_new):
    past = cc_ref.shape[0]
    for h in range(n_heads):
        c0 = h * HEAD_PAD
        ql_ref[h * t_new:(h + 1) * t_new, :] = _dot(q_ref[:, c0:c0 + nope], wukt_ref[h]).astype(BF16)
        qr_ref[h * t_new:(h + 1) * t_new, :] = q_ref[:, c0 + nope:c0 + nope + rope]
    ql = ql_ref[...]
    qr = qr_ref[...]
    nc = nc_ref[...]
    s_new = _dot_nt(ql, nc) + _dot_nt(qr, nr_ref[:, :rope])
    rows = ql.shape[0]
    nchunk = past // tk
    half = nchunk // 2
    ks = [slice(j * tk, (j + 1) * tk) for j in range(nchunk)]

    def score(j, m_lane):
        cb = cc_ref[ks[j], :].astype(BF16)
        cb_ref[ks[j], :] = cb
        s = _dot_nt(ql, cb) + _dot(qr, crt_ref[:, ks[j]].astype(BF16))
        s_ref[:, ks[j]] = s
        return jnp.maximum(m_lane, _lane_fold(s, jnp.maximum))

    def probs(j, m, l_lane):
        p = jnp.exp2(s_ref[:, ks[j]] - m)
        p_ref[:, ks[j]] = p.astype(BF16)
        return l_lane + _lane_fold(p, jnp.add)

    neg = jnp.full((rows, 128), -jnp.inf, F32)
    zero = jnp.zeros((rows, 128), F32)
    m_lane = neg
    for j in range(half):
        m_lane = score(j, m_lane)
    m_a = jnp.max(m_lane, axis=-1, keepdims=True)
    m_lane, l_a = neg, zero
    for i, j in enumerate(range(half, nchunk)):
        m_lane = score(j, m_lane)
        if i < half:
            l_a = probs(i, m_a, l_a)
    m_b = jnp.maximum(jnp.max(m_lane, axis=-1, keepdims=True), jnp.max(s_new, axis=-1, keepdims=True))
    m = jnp.maximum(m_a, m_b)
    acc_a, l_b = jnp.zeros((rows, cb_ref.shape[1]), F32), zero
    for i, j in enumerate(range(half, nchunk)):
        l_b = probs(j, m, l_b)
        if i < half:
            acc_a = acc_a + _dot(p_ref[:, ks[i]], cb_ref[ks[i], :])
    p_new = jnp.exp2(s_new - m)
    acc_b = _dot(p_new.astype(BF16), nc)
    for j in range(half, nchunk):
        acc_b = acc_b + _dot(p_ref[:, ks[j]], cb_ref[ks[j], :])
    alpha = jnp.exp2(m_a - m)
    l = (alpha * jnp.sum(l_a, axis=-1, keepdims=True) + jnp.sum(l_b, axis=-1, keepdims=True)
         + jnp.sum(p_new, axis=-1, keepdims=True))
    ol = ((alpha * acc_a + acc_b) / l).astype(BF16)
    for h in range(n_heads):
        o_ref[:, h * nope:(h + 1) * nope] = _dot(ol[h * t_new:(h + 1) * t_new, :], wuv_ref[h]).astype(BF16)


def _proj_res_kernel(w_ref, *refs, n_a, n_x, np_tiles):
    a_refs, x_refs, o_ref = refs[:n_a], refs[n_a:n_a + n_x], refs[n_a + n_x]
    o_ref[...] = _load_rows(x_refs, np_tiles) + _dot(_load_rows(a_refs, np_tiles), w_ref[...])


def _ffn_accumulate(x_refs, g_ref, w1_ref, w2_ref, acc_ref, h_ref, np_tiles):
    @pl.when(pl.program_id(1) == 0)
    def _():
        x = _load_rows(x_refs, np_tiles)
        h_ref[...] = _rms(x, g_ref[...]).astype(BF16)
        acc_ref[...] = x

    a = jnp.maximum(_dot(h_ref[...], w1_ref[...]), 0.0)
    acc_ref[...] += _dot((a * a).astype(BF16), w2_ref[...])


def _ffn_kernel(*refs, n_x, np_tiles):
    x_refs, (g_ref, w1_ref, w2_ref, o_ref, h_ref) = refs[:n_x], refs[n_x:]
    _ffn_accumulate(x_refs, g_ref, w1_ref, w2_ref, o_ref, h_ref, np_tiles)


def _ffn_final_kernel(*refs, n_x, np_tiles):
    x_refs, (g_ref, w1_ref, w2_ref, gf_ref, yp_ref, ys_ref, h_ref, acc_ref) = refs[:n_x], refs[n_x:]
    _ffn_accumulate(x_refs, g_ref, w1_ref, w2_ref, acc_ref, h_ref, np_tiles)

    @pl.when(pl.program_id(1) == pl.num_programs(1) - 1)
    def _():
        _store_rows(yp_ref, ys_ref, _rms(acc_ref[...], gf_ref[...]), np_tiles)


def _conv_in_kernel(x_ref, g_ref, wb_ref, wc_ref, wv_ref, u_ref, gb_ref, h_ref):
    @pl.when(pl.program_id(1) == 0)
    def _():
        h_ref[...] = _rms(x_ref[...], g_ref[...]).astype(BF16)

    h = h_ref[...]
    gb_ref[...] = _dot(h, wb_ref[...]).astype(BF16)
    u_ref[...] = _dot(h, wc_ref[...]) * _dot(h, wv_ref[...])


def _conv_mix(u, p1, p2, first, second, gb, cw, wout, x):
    u1 = jnp.where(first, p1, pltpu.roll(u, 1, axis=0))
    u2 = jnp.where(first, p2, jnp.where(second, p1, pltpu.roll(u, 2, axis=0)))
    conv = cw[0:1] * u2 + cw[1:2] * u1 + cw[2:3] * u
    return x + _dot((gb.astype(F32) * conv).astype(BF16), wout)


def _conv_out_kernel(u_ref, halo_ref, st_ref, gb_ref, x_ref, cw_ref, wout_ref, o_ref,
                     *, np_tiles, tiles_per_seq, t_new):
    i = pl.program_id(0)
    tm, d = u_ref.shape
    row = lax.broadcasted_iota(jnp.int32, (tm, 1), 0)

    @pl.when(i < np_tiles)
    def _():
        has_past = i % tiles_per_seq > 0
        halo = halo_ref[...]
        p1 = jnp.where(has_past, halo[7:8], 0.0)
        p2 = jnp.where(has_past, halo[6:7], 0.0)
        o_ref[...] = _conv_mix(u_ref[...], p1, p2, row == 0, row == 1, gb_ref[...], cw_ref[...],
                               wout_ref[...], x_ref[...])

    @pl.when(i >= np_tiles)
    def _():
        nb = tm // t_new
        st = st_ref[...]
        p1 = jnp.broadcast_to(st[:, 1:2, :], (nb, t_new, d)).reshape(tm, d)
        p2 = jnp.broadcast_to(st[:, 0:1, :], (nb, t_new, d)).reshape(tm, d)
        t = row % t_new
        o_ref[...] = _conv_mix(u_ref[...], p1, p2, t == 0, t == 1, gb_ref[...], cw_ref[...],
                               wout_ref[...], x_ref[...])


def _row_spec(tm, n, off=0):
    return pl.BlockSpec((tm, n), lambda i, *_: (i + off, 0))


def _col_spec(n, tm):
    return pl.BlockSpec((n, tm), lambda i, *_: (0, i))


def _cast_job(w, layer, axis, steps, step_of):
    _, r, c = w.shape
    n = (r, c)[axis]
    n_chunks = max(k for k in range(1, steps + 1) if n % k == 0 and (n // k) % 128 == 0)
    blk = (n // n_chunks, c) if axis == 0 else (r, n // n_chunks)

    def idx(*ids):
        k = jnp.minimum(step_of(*ids), n_chunks - 1)
        return (k, 0) if axis == 0 else (0, k)

    return (n_chunks, pl.BlockSpec((None,) + blk, lambda *ids: (layer,) + idx(*ids)),
            pl.BlockSpec(blk, idx), jax.ShapeDtypeStruct((r, c), BF16))


def _cast_plan(jobs, steps, step_of):
    plans = [_cast_job(w, layer, axis, steps, step_of) for w, layer, axis in jobs]
    return tuple(p[0] for p in plans), [p[1] for p in plans], [p[2] for p in plans], [p[3] for p in plans]


def _split_row_specs(tm, n, np_tiles, layers=None):
    def p_map(i, *_):
        return (jnp.minimum(i, np_tiles - 1), 0)

    def s_map(i, *_):
        return (jnp.maximum(i - np_tiles, 0), 0)

    if layers is None:
        return [pl.BlockSpec((tm, n), p_map), pl.BlockSpec((tm, n), s_map)]
    return [pl.BlockSpec((layers, tm, n), lambda i, *_: (0,) + p_map(i)),
            pl.BlockSpec((layers, tm, n), lambda i, *_: (0,) + s_map(i))]


def _mla_layer(xs, j, prev, n_prompt, seq, t_new, tabs, cache_c, cache_rt, g_mix, w_dq, g_q, w_uq, w_dkv, g_kv,
               w_uk, w_uv, cast_jobs):
    cos, sin, cost, sint = tabs
    t_all, d = sum(a.shape[0] for a in xs), xs[0].shape[1]
    q_lora = w_dq.shape[1]
    kv_lora, n_heads, nope = w_uk.shape
    rope = w_dkv.shape[1] - kv_lora
    half = rope // 2
    q_scale = float((nope + rope) ** -0.5 * math.log2(math.e))
    n_sample = t_all - n_prompt

    wr = w_dkv[:, kv_lora:]
    zr = jnp.zeros((d, ROPE_PAD - rope), F32)
    w_down = jnp.concatenate(
        [w_dq, w_dkv[:, :kv_lora], wr, zr, -wr[:, half:], wr[:, :half], zr], axis=1).astype(BF16)
    wq = w_uq.reshape(q_lora, n_heads, nope + rope)
    wq_r = wq[..., nope:]
    w_qa = jnp.concatenate(
        [wq, jnp.zeros((q_lora, n_heads, HEAD_PAD - nope - rope), F32)], axis=-1
    ).reshape(q_lora, n_heads * HEAD_PAD).astype(BF16)
    w_qb = jnp.concatenate(
        [-wq_r[..., half:], wq_r[..., :half], jnp.zeros((q_lora, n_heads, ROPE_PAD - rope), F32)], axis=-1
    ).reshape(q_lora, n_heads * ROPE_PAD).astype(BF16)
    w_k = w_uk.reshape(kv_lora, n_heads * nope).astype(BF16)
    w_vt = w_uv.reshape(kv_lora, n_heads * nope).T.astype(BF16)
    w_ukt = jnp.transpose(w_uk, (1, 2, 0)).astype(BF16)
    w_uvh = jnp.transpose(w_uv, (1, 0, 2)).astype(BF16)

    tm = _tile(n_sample, 512)
    np_tiles = n_prompt // tm
    n_down = q_lora + kv_lora + 2 * ROPE_PAD
    x_specs = _split_row_specs(tm, d, np_tiles) if len(xs) == 2 else [_row_spec(tm, d)]
    widths = (kv_lora, kv_lora, rope, rope)
    rows = (n_prompt, n_sample, n_prompt, n_sample)
    prev_specs = []
    if prev is not None:
        prev_specs = _split_row_specs(tm, kv_lora, np_tiles, j) + _split_row_specs(tm, rope, np_tiles, j)
    cq, cqt, ckvb, ckvt, krb, *new = pl.pallas_call(
        functools.partial(_mla_down_kernel, n_x=len(xs), n_prev=len(prev_specs), np_tiles=np_tiles,
                          q_lora=q_lora, kv_lora=kv_lora, rope=rope, q_scale=q_scale),
        grid=(t_all // tm,),
        in_specs=x_specs + [_resident((1, d)), _resident((d, n_down)), _resident((1, q_lora)),
                            _resident((1, kv_lora)), _row_spec(tm, ROPE_PAD), _row_spec(tm, ROPE_PAD)]
        + prev_specs,
        out_specs=[_row_spec(tm, q_lora), _col_spec(q_lora, tm), _row_spec(tm, kv_lora),
                   _col_spec(kv_lora, tm), _row_spec(tm, ROPE_PAD)]
        + _split_row_specs(tm, kv_lora, np_tiles, j + 1) + _split_row_specs(tm, rope, np_tiles, j + 1),
        out_shape=[jax.ShapeDtypeStruct((t_all, q_lora), BF16), jax.ShapeDtypeStruct((q_lora, t_all), BF16),
                   jax.ShapeDtypeStruct((t_all, kv_lora), BF16), jax.ShapeDtypeStruct((kv_lora, t_all), BF16),
                   jax.ShapeDtypeStruct((t_all, ROPE_PAD), BF16)]
        + [jax.ShapeDtypeStruct((j + 1, r, w), F32) for r, w in zip(rows, widths)],
        compiler_params=_params("arbitrary"), name="mla_down",
    )(*xs, g_mix.reshape(1, d), w_down, g_q.reshape(1, q_lora), g_kv.reshape(1, kv_lora), cos, sin,
      *(prev or ()))

    tp = _tile(n_prompt, 512)
    w_qat = w_uq.T.astype(BF16)
    w_qbt = jnp.concatenate([-wq_r[..., half:], wq_r[..., :half]], axis=-1
                            ).reshape(q_lora, n_heads * rope).T.astype(BF16)
    qt = pl.pallas_call(
        functools.partial(_mla_qt_kernel, n_heads=n_heads, nope=nope),
        grid=(n_prompt // tp,),
        in_specs=[_col_spec(q_lora, tp), _resident(w_qat.shape), _resident(w_qbt.shape),
                  _col_spec(rope, tp), _col_spec(rope, tp)],
        out_specs=_col_spec(n_heads * HEAD_PAD, tp),
        out_shape=jax.ShapeDtypeStruct((n_heads * HEAD_PAD, n_prompt), BF16),
        compiler_params=_params("parallel"), name="mla_qt",
    )(cqt, w_qat, w_qbt, cost, sint)

    ts = _tile(n_sample, 512)
    off = n_prompt // ts
    q_s = pl.pallas_call(
        functools.partial(_mla_q_kernel, n_heads=n_heads, nope=nope),
        grid=(n_sample // ts,),
        in_specs=[_row_spec(ts, q_lora, off), _resident(w_qa.shape), _resident(w_qb.shape),
                  _row_spec(ts, ROPE_PAD, off), _row_spec(ts, ROPE_PAD, off)],
        out_specs=_row_spec(ts, n_heads * HEAD_PAD),
        out_shape=jax.ShapeDtypeStruct((n_sample, n_heads * HEAD_PAD), BF16),
        compiler_params=_params("parallel"), name="mla_q",
    )(cq, w_qa, w_qb, cos, sin)

    k, vt = pl.pallas_call(
        functools.partial(_mla_kv_kernel, n_heads=n_heads, nope=nope),
        grid=(n_prompt // tp,),
        in_specs=[_row_spec(tp, kv_lora), _col_spec(kv_lora, tp), _row_spec(tp, ROPE_PAD),
                  _resident(w_k.shape), _resident(w_vt.shape)],
        out_specs=[_row_spec(tp, n_heads * HEAD_PAD), _col_spec(n_heads * VT_PAD, tp)],
        out_shape=[jax.ShapeDtypeStruct((n_prompt, n_heads * HEAD_PAD), BF16),
                   jax.ShapeDtypeStruct((n_heads * VT_PAD, n_prompt), BF16)],
        compiler_params=_params("parallel"), name="mla_kv",
    )(ckvb, ckvt, krb, w_k, w_vt)

    blk = _tile(seq, 4096)
    kb = _tile(blk, 256)
    nq = seq // blk
    n_batch = n_prompt // seq
    n_o = n_heads * nope
    n_chunks, cast_in, cast_out, cast_shape = _cast_plan(
        cast_jobs, n_batch * n_heads * nq, lambda b, h, i: (b * n_heads + h) * nq + i)
    o_p, *casted = pl.pallas_call(
        functools.partial(_flash_kernel, nq=nq, tq=blk, kb=kb, kd=_tile(kb, 256), cw=_tile(blk, 256),
                          v_head=nope, n_chunks=n_chunks),
        grid=(n_batch, n_heads, nq),
        in_specs=[pl.BlockSpec((HEAD_PAD, blk), lambda b, h, i: (h, b * nq + i)),
                  pl.BlockSpec((seq, HEAD_PAD), lambda b, h, i: (b, h)),
                  pl.BlockSpec((VT_PAD, seq), lambda b, h, i: (h, b))] + cast_in,
        out_specs=[pl.BlockSpec((blk, nope), lambda b, h, i: (b * nq + i, h))] + cast_out,
        out_shape=[jax.ShapeDtypeStruct((n_prompt, n_o), BF16)] + cast_shape,
        scratch_shapes=[pltpu.VMEM((kb, blk), F32)],
        compiler_params=_params("arbitrary", "arbitrary", "arbitrary"), name="mla_flash",
    )(qt, k, vt, *[job[0] for job in cast_jobs])
    w_ob = casted[0]

    past = cache_c.shape[2]
    n_sb = n_sample // t_new
    row0 = n_prompt // t_new
    tk = _tile(past, 512)
    o_s = pl.pallas_call(
        functools.partial(_sample_attn_kernel, n_heads=n_heads, nope=nope, rope=rope, tk=tk, t_new=t_new),
        grid=(n_sb,),
        in_specs=[pl.BlockSpec((t_new, n_heads * HEAD_PAD), lambda b: (b, 0)),
                  _resident(w_ukt.shape), _resident(w_uvh.shape),
                  pl.BlockSpec((None, None, past, kv_lora), lambda b: (j, b, 0, 0)),
                  pl.BlockSpec((None, None, rope, past), lambda b: (j, b, 0, 0)),
                  pl.BlockSpec((t_new, kv_lora), lambda b: (row0 + b, 0)),
                  pl.BlockSpec((t_new, ROPE_PAD), lambda b: (row0 + b, 0))],
        out_specs=pl.BlockSpec((t_new, n_o), lambda b: (b, 0)),
        out_shape=jax.ShapeDtypeStruct((n_sample, n_o), BF16),
        scratch_shapes=[pltpu.VMEM((n_heads * t_new, kv_lora), BF16), pltpu.VMEM((n_heads * t_new, rope), BF16),
                        pltpu.VMEM((past, kv_lora), BF16), pltpu.VMEM((n_heads * t_new, past), F32),
                        pltpu.VMEM((n_heads * t_new, past), BF16)],
        compiler_params=_params("parallel"), name="mla_sample_attn",
    )(q_s, w_ukt, w_uvh, cache_c, cache_rt, ckvb, krb)

    x = pl.pallas_call(
        functools.partial(_proj_res_kernel, n_a=2, n_x=len(xs), np_tiles=np_tiles),
        grid=(t_all // tm,),
        in_specs=[_resident((n_o, d))] + _split_row_specs(tm, n_o, np_tiles) + x_specs,
        out_specs=_row_spec(tm, d),
        out_shape=jax.ShapeDtypeStruct((t_all, d), F32),
        compiler_params=_params("arbitrary"), name="mla_out",
    )(w_ob, o_p, o_s, *xs)
    return (x,), new, casted[1:]


def _conv_layer(x, n_prompt, seq, t_new, state, g_mix, w_inb, conv_w, w_outb):
    assert conv_w.shape[0] == 3 and state.shape[1] == 2
    t_all, d = x.shape
    tm = _tile(t_all, 512)
    tn = _tile(d, 1024)
    nn = d // tn
    w_out_spec = _resident((d, d))
    u, gb = pl.pallas_call(
        _conv_in_kernel,
        grid=(t_all // tm, nn),
        in_specs=[pl.BlockSpec((tm, d), lambda i, n: (i, 0)), _resident((1, d)),
                  pl.BlockSpec((d, tn), lambda i, n: (0, n)),
                  pl.BlockSpec((d, tn), lambda i, n: (0, nn + n)),
                  pl.BlockSpec((d, tn), lambda i, n: (0, 2 * nn + n))],
        out_specs=[pl.BlockSpec((tm, tn), lambda i, n: (i, n)), pl.BlockSpec((tm, tn), lambda i, n: (i, n))],
        out_shape=[jax.ShapeDtypeStruct((t_all, d), F32), jax.ShapeDtypeStruct((t_all, d), BF16)],
        scratch_shapes=[pltpu.VMEM((tm, d), BF16)],
        compiler_params=_params("parallel", "arbitrary"), name="conv_in",
    )(x, g_mix.reshape(1, d), w_inb, w_inb, w_inb)

    tc = _tile(min(seq, t_all - n_prompt), 512)
    assert seq % tc == 0 and tc % t_new == 0 and tc % 8 == 0
    np_tiles = n_prompt // tc
    hb = tc // 8
    nb = tc // t_new
    x_new = pl.pallas_call(
        functools.partial(_conv_out_kernel, np_tiles=np_tiles, tiles_per_seq=seq // tc, t_new=t_new),
        grid=(t_all // tc,),
        in_specs=[_row_spec(tc, d),
                  pl.BlockSpec((8, d), lambda i: (jnp.maximum(i * hb - 1, 0), 0)),
                  pl.BlockSpec((nb, state.shape[1], d), lambda i: (jnp.maximum(i - np_tiles, 0), 0, 0)),
                  _row_spec(tc, d), _row_spec(tc, d), _resident((conv_w.shape[0], d)), w_out_spec],
        out_specs=_row_spec(tc, d),
        out_shape=jax.ShapeDtypeStruct((t_all, d), F32),
        compiler_params=_params("parallel"), name="conv_out",
    )(u, u, state, gb, x, conv_w, w_outb)
    return (x_new,), u


def _ffn_layer(xs, g, w1b, w2b, n_prompt, g_final=None):
    t_all, d = sum(a.shape[0] for a in xs), xs[0].shape[1]
    f = w1b.shape[1]
    tm = _tile(t_all - n_prompt, 512)
    tf = _tile(f, 1024)
    np_tiles = n_prompt // tm
    x_specs = _split_row_specs(tm, d, np_tiles) if len(xs) == 2 else [_row_spec(tm, d)]
    in_specs = x_specs + [_resident((1, d)), pl.BlockSpec((d, tf), lambda i, c: (0, c)),
                          pl.BlockSpec((tf, d), lambda i, c: (c, 0))]
    grid = (t_all // tm, f // tf)
    h_scratch = pltpu.VMEM((tm, d), BF16)
    if g_final is None:
        return pl.pallas_call(
            functools.partial(_ffn_kernel, n_x=len(xs), np_tiles=np_tiles),
            grid=grid, in_specs=in_specs, out_specs=_row_spec(tm, d),
            out_shape=jax.ShapeDtypeStruct((t_all, d), F32), scratch_shapes=[h_scratch],
            compiler_params=_params("arbitrary", "arbitrary"), name="ffn",
        )(*xs, g.reshape(1, d), w1b, w2b)
    return pl.pallas_call(
        functools.partial(_ffn_final_kernel, n_x=len(xs), np_tiles=np_tiles),
        grid=grid, in_specs=in_specs + [_resident((1, d))],
        out_specs=_split_row_specs(tm, d, np_tiles),
        out_shape=[jax.ShapeDtypeStruct((n_prompt, d), F32), jax.ShapeDtypeStruct((t_all - n_prompt, d), F32)],
        scratch_shapes=[h_scratch, pltpu.VMEM((tm, d), F32)],
        compiler_params=_params("arbitrary", "arbitrary"), name="ffn_final",
    )(*xs, g.reshape(1, d), w1b, w2b, g_final.reshape(1, d))


def kernel(x_prompt, x_sample, cache_ckv, cache_krope, state_conv, norm_mix, w_dq, g_q, w_uq, w_dkv, g_kv,
           w_uk, w_uv, w_o, w_in, conv_w, w_out, norm_ffn, w_ff1, w_ff2, norm_final):
    n_batch, seq, d = x_prompt.shape
    s_batch, t_new, _ = x_sample.shape
    past = cache_ckv.shape[2]
    depth = norm_mix.shape[0]
    kv_lora = w_uk.shape[1]
    rope = w_dkv.shape[2] - kv_lora
    half = rope // 2
    n_prompt = n_batch * seq
    n_sample = s_batch * t_new

    pos = jnp.concatenate([jnp.tile(jnp.arange(seq, dtype=jnp.int32), n_batch),
                           jnp.tile(past + jnp.arange(t_new, dtype=jnp.int32), s_batch)])
    inv = ROPE_THETA ** (-jnp.arange(0, half, dtype=F32) / half)
    ang_t = inv[:, None] * pos.astype(F32)[None, :]
    cos_t = jnp.tile(jnp.cos(ang_t), (2, 1))
    sin_t = jnp.tile(jnp.sin(ang_t), (2, 1))
    zpad = jnp.zeros((pos.shape[0], ROPE_PAD - rope), F32)
    cos = jnp.concatenate([cos_t.T, zpad], axis=1)
    sin = jnp.concatenate([sin_t.T, zpad], axis=1)
    tabs = (cos, sin, cos_t[:, :n_prompt], sin_t[:, :n_prompt])

    cache_rt = jnp.swapaxes(cache_krope, 2, 3)
    xs = (x_prompt.reshape(n_prompt, d), x_sample.reshape(n_sample, d))
    latents, new_u = None, []
    for i in range(depth):
        j = i // 2
        if i % 2 == 0:
            jobs = [(w_o, j, 1), (w_ff1, i, 1), (w_ff2, i, 0)]
            if i + 1 < depth:
                jobs += [(w_in, j, 1), (w_out, j, 1), (w_ff1, i + 1, 1), (w_ff2, i + 1, 0)]
            xs, latents, casted = _mla_layer(
                xs, j, latents, n_prompt, seq, t_new, tabs, cache_ckv, cache_rt, norm_mix[i], w_dq[j], g_q[j],
                w_uq[j], w_dkv[j], g_kv[j], w_uk[j], w_uv[j], jobs)
            w1b, w2b, *nxt = casted
        else:
            w_inb, w_outb, w1b, w2b = nxt
            xs, u = _conv_layer(xs[0], n_prompt, seq, t_new, state_conv[j], norm_mix[i], w_inb, conv_w[j],
                                w_outb)
            new_u.append(u)
        if i + 1 < depth:
            xs = (_ffn_layer(xs, norm_ffn[i], w1b, w2b, n_prompt),)
        else:
            y_prompt, y_sample = _ffn_layer(xs, norm_ffn[i], w1b, w2b, n_prompt, norm_final)

    def tails(u, lo, lead, inner, keep):
        return jnp.stack([lax.slice_in_dim(u, lo + (b + 1) * inner - keep, lo + (b + 1) * inner, axis=0)
                          for b in range(lead)])

    keep = conv_w.shape[1] - 1
    ckv_p, ckv_s, kr_p, kr_s = latents
    n_a = ckv_p.shape[0]
    u_p = jnp.stack([tails(u, 0, n_batch, seq, keep) for u in new_u])
    u_s = jnp.stack([tails(u, n_prompt, s_batch, t_new, keep) for u in new_u])
    return (y_prompt.reshape(n_batch, seq, d), y_sample.reshape(s_batch, t_new, d),
            ckv_p.reshape(n_a, n_batch, seq, -1), kr_p.reshape(n_a, n_batch, seq, -1), u_p,
            ckv_s.reshape(n_a, s_batch, t_new, -1), kr_s.reshape(n_a, s_batch, t_new, -1), u_s)
```
